```python
import math
import jax
import jax.numpy as jnp
from jax import lax
import numpy as np

D_MODEL = 2048
BATCH = 4
SEQ = 2048
DEPTH = 2
DEC_BATCH = 2
DEC_SEQ = 8192
PAST_LEN = 128

GRID_W = 64
CHUNK = 128
A_GROUPS = 8
A_WIDTH = D_MODEL // 2
A_GDIM = A_WIDTH // A_GROUPS
B_HEADS = 16
B_DH = 64
B_WIDTH = B_HEADS * B_DH
NA_KH_MAX = 8
NA_KW = 16
C_HEADS = 16
C_DH = D_MODEL // C_HEADS
C_WIDTH = C_HEADS * C_DH
C_PATTERNS = ((128, 1), (512, 4), (2048, 16))
T5_BUCKETS = 32
T5_MAX_DIST = 1024
MOE_GROUPS = 4
MOE_PER_GROUP = 8
MOE_EXPERTS = MOE_GROUPS * MOE_PER_GROUP
MOE_TOPK = 2
D_EXPERT = D_MODEL // 2
MOE_BLOCK = 128

N_EVEN = (DEPTH + 1) // 2
N_ODD = DEPTH // 2
EPS = 1e-6

kernel_name = 'hybrid_gmlp_natten_dilated_hmoe_encoder'


def rms_norm(x, g):
    xf = x.astype(jnp.float32)
    y = xf * lax.rsqrt(jnp.mean(xf * xf, axis=-1, keepdims=True) + EPS)
    return (y * g.astype(jnp.float32)).astype(x.dtype)


def layer_norm(x, g, b):
    xf = x.astype(jnp.float32)
    mu = jnp.mean(xf, axis=-1, keepdims=True)
    xc = xf - mu
    var = jnp.mean(xc * xc, axis=-1, keepdims=True)
    y = xc * lax.rsqrt(var + EPS) * g.astype(jnp.float32) + b.astype(jnp.float32)
    return y.astype(x.dtype)


def chunked_spatial_gating(u, v, ln_g, ln_b, w_s, b_s):
    bn, L, _ = u.shape
    v = layer_norm(v, ln_g, ln_b)
    vc = v.reshape(bn, L // CHUNK, CHUNK, A_GROUPS, A_GDIM)
    f = jnp.einsum('gij,bnjgc->bnigc', w_s.astype(v.dtype), vc) + b_s.T[:, :, None].astype(v.dtype)
    return u * f.reshape(bn, L, A_WIDTH)


def neighborhood_attention(q, k, v, gq, gk, rpb):
    bn, L, H, dh = q.shape
    rows = L // GRID_W
    kh = min(NA_KH_MAX, rows)
    q = rms_norm(q, gq) * (dh ** -0.5)
    k = rms_norm(k, gk)
    col = jnp.arange(GRID_W)
    col_start = jnp.clip(col - NA_KW // 2, 0, GRID_W - NA_KW)
    dc = col[None, :] - col[:, None]
    col_valid = (col[None, :] >= col_start[:, None]) & (col[None, :] < col_start[:, None] + NA_KW)
    dc_idx = jnp.clip(dc + NA_KW - 1, 0, 2 * NA_KW - 2)
    bias_c = rpb.astype(jnp.float32)[:, :, dc_idx]
    mask = jnp.tile(col_valid, (1, kh))
    q_rows = jnp.moveaxis(q.reshape(bn, rows, GRID_W, H, dh), 1, 0)

    def row_step(args):
        r, q_row = args
        rs = jnp.clip(r - kh // 2, 0, rows - kh)
        k_s = lax.dynamic_slice_in_dim(k, rs * GRID_W, kh * GRID_W, axis=1)
        v_s = lax.dynamic_slice_in_dim(v, rs * GRID_W, kh * GRID_W, axis=1)
        s = jnp.einsum('bqhd,bkhd->bhqk', q_row, k_s, preferred_element_type=jnp.float32)
        dr_idx = rs + jnp.arange(kh) - r + NA_KH_MAX - 1
        bias = bias_c[:, dr_idx].transpose(0, 2, 1, 3).reshape(H, GRID_W, kh * GRID_W)
        s = jnp.where(mask, s + bias, -jnp.inf)
        p = jax.nn.softmax(s, axis=-1).astype(v.dtype)
        return jnp.einsum('bhqk,bkhd->bqhd', p, v_s)

    out = lax.map(row_step, (jnp.arange(rows), q_rows))
    return jnp.moveaxis(out, 0, 1).reshape(bn, L, H * dh)


def t5_bucket(rel):
    nb = T5_BUCKETS // 2
    max_exact = nb // 2
    ret = jnp.where(rel > 0, nb, 0)
    n = jnp.abs(rel)
    large = max_exact + (jnp.log(jnp.maximum(n, 1).astype(jnp.float32) / max_exact)
                         / math.log(T5_MAX_DIST / max_exact) * (nb - max_exact)).astype(jnp.int32)
    large = jnp.minimum(large, nb - 1)
    return ret + jnp.where(n < max_exact, n, large)


def dilated_branch(q, k, v, t5_table, window, dil):
    bn, L, H, dh = q.shape
    rad = window // (2 * dil)
    n = L // dil
    nb = -(-n // rad)
    npad = nb * rad

    def to_sub(t, lead, trail):
        t = t.reshape(bn, n, dil, H, dh).transpose(0, 2, 1, 3, 4)
        return jnp.pad(t, ((0, 0), (0, 0), (lead, trail), (0, 0), (0, 0)))

    def band(t):
        t = t.reshape(bn, dil, nb + 2, rad, H, dh)
        return jnp.concatenate([t[:, :, :-2], t[:, :, 1:-1], t[:, :, 2:]], axis=3)

    qb = to_sub(q, 0, npad - n).reshape(bn, dil, nb, rad, H, dh)
    kb = band(to_sub(k, rad, npad - n + rad))
    vb = band(to_sub(v, rad, npad - n + rad))
    qi = jnp.arange(rad)[:, None]
    kj = jnp.arange(3 * rad)[None, :]
    rel = kj - rad - qi
    bias = t5_table.astype(jnp.float32)[t5_bucket(rel * dil)].transpose(2, 0, 1)
    m_k = jnp.arange(nb)[:, None] * rad - rad + kj
    valid = (jnp.abs(rel) <= rad)[None] & ((m_k >= 0) & (m_k < n))[:, None, :]
    s = jnp.einsum('bsnqhd,bsnkhd->bsnhqk', qb, kb, preferred_element_type=jnp.float32) + bias
    s = jnp.where(valid[None, None, :, None], s, -jnp.inf)
    m = jnp.max(s, axis=-1, keepdims=True)
    e = jnp.exp(s - m)
    l = jnp.sum(e, axis=-1, keepdims=True)
    o = jnp.einsum('bsnhqk,bsnkhd->bsnqhd', (e / l).astype(v.dtype), vb)
    lse = (m + jnp.log(l))[..., 0].transpose(0, 1, 2, 4, 3)
    o = o.reshape(bn, dil, npad, H, dh)[:, :, :n].transpose(0, 2, 1, 3, 4).reshape(bn, L, H, dh)
    lse = lse.reshape(bn, dil, npad, H)[:, :, :n].transpose(0, 2, 1, 3).reshape(bn, L, H)
    return o, lse


def even_mixer(h, w_in, w_out, a_ln_g, a_ln_b, a_w_s, a_b_s, b_q_gain, b_k_gain, b_rpb):
    bn, L, _ = h.shape
    z = h @ w_in
    cuts = [A_WIDTH, 2 * A_WIDTH, 2 * A_WIDTH + B_WIDTH, 2 * A_WIDTH + 2 * B_WIDTH]
    a_u, a_v, b_q, b_k, b_v = jnp.split(z, cuts, axis=-1)
    a_out = chunked_spatial_gating(jax.nn.gelu(a_u), jax.nn.gelu(a_v), a_ln_g, a_ln_b, a_w_s, a_b_s)
    hs = (bn, L, B_HEADS, B_DH)
    b_out = neighborhood_attention(b_q.reshape(hs), b_k.reshape(hs), b_v.reshape(hs), b_q_gain, b_k_gain, b_rpb)
    return jnp.concatenate([a_out, b_out], axis=-1) @ w_out


def odd_mixer(h, w_in, w_out, c_q_gain, c_k_gain, t5_table):
    bn, L, _ = h.shape
    q, k, v = jnp.split(h @ w_in, 3, axis=-1)
    hs = (bn, L, C_HEADS, C_DH)
    q = rms_norm(q.reshape(hs), c_q_gain) * (C_DH ** -0.5)
    k = rms_norm(k.reshape(hs), c_k_gain)
    v = v.reshape(hs)
    branches = [dilated_branch(q, k, v, t5_table, w, d) for (w, d) in C_PATTERNS]
    o = jnp.stack([br[0] for br in branches])
    lse = jnp.stack([br[1] for br in branches])
    alpha = jax.nn.softmax(lse, axis=0).astype(o.dtype)
    out = jnp.einsum('pblh,pblhd->blhd', alpha, o)
    return out.reshape(bn, L, C_WIDTH) @ w_out


def expert_ffn_sorted(x, expert, gates, w_gate, w_up, w_down):
    N, D = x.shape
    M = N * MOE_TOPK
    flat_e = expert.reshape(M)
    flat_tok = jnp.arange(M, dtype=jnp.int32) // MOE_TOPK
    order = jnp.argsort(flat_e)
    e_sorted = flat_e[order]
    tok_sorted = flat_tok[order]
    gate_sorted = gates.reshape(M)[order]
    counts = jnp.bincount(flat_e, length=MOE_EXPERTS)
    start = jnp.cumsum(counts) - counts
    padded = (counts + MOE_BLOCK - 1) // MOE_BLOCK * MOE_BLOCK
    pad_end = jnp.cumsum(padded)
    pad_start = pad_end - padded
    dest = pad_start[e_sorted] + jnp.arange(M) - start[e_sorted]
    n_blocks = (M + MOE_EXPERTS * (MOE_BLOCK - 1) + MOE_BLOCK - 1) // MOE_BLOCK
    P = n_blocks * MOE_BLOCK
    slot_tok = jnp.full((P,), N, jnp.int32).at[dest].set(tok_sorted)
    x_pad = jnp.concatenate([x, jnp.zeros((1, D), x.dtype)], axis=0)
    xb = x_pad[slot_tok].reshape(n_blocks, MOE_BLOCK, D)
    block_e = jnp.minimum(jnp.searchsorted(pad_end, jnp.arange(n_blocks) * MOE_BLOCK, side='right'),
                          MOE_EXPERTS - 1)

    def block_ffn(args):
        xi, e = args
        hmid = jax.nn.silu(xi @ w_gate[e]) * (xi @ w_up[e])
        return hmid @ w_down[e]

    yb = lax.map(block_ffn, (xb, block_e)).reshape(P, D)
    y_slot = yb[dest] * gate_sorted[:, None].astype(yb.dtype)
    return jax.ops.segment_sum(y_slot, tok_sorted, num_segments=N)


def hier_moe(h, wr_g, br_g, wr_e, br_e, w_gate, w_up, w_down):
    bn, L, D = h.shape
    x = h.reshape(bn * L, D)
    N = x.shape[0]
    lg = (x @ wr_g).astype(jnp.float32) + br_g.astype(jnp.float32)
    g_sel = jnp.argmax(lg, axis=-1)
    p_g = jnp.take_along_axis(jax.nn.softmax(lg, axis=-1), g_sel[:, None], axis=-1)
    le = ((x @ wr_e).astype(jnp.float32) + br_e.astype(jnp.float32)).reshape(N, MOE_GROUPS, MOE_PER_GROUP)
    le_sel = jnp.take_along_axis(le, g_sel[:, None, None], axis=1)[:, 0]
    top_v, top_i = lax.top_k(le_sel, MOE_TOPK)
    gates = p_g * jax.nn.softmax(top_v, axis=-1)
    expert = (g_sel[:, None] * MOE_PER_GROUP + top_i).astype(jnp.int32)
    y = expert_ffn_sorted(x, expert, gates, w_gate, w_up, w_down)
    return y.reshape(bn, L, D)


def encoder_trunk(x, norm_mix, norm_ffn, ev_w_in, ev_w_out, a_ln_g, a_ln_b, a_w_s, a_b_s,
                  b_q_gain, b_k_gain, b_rpb, od_w_in, od_w_out, c_q_gain, c_k_gain, t5_table,
                  moe_wr_g, moe_br_g, moe_wr_e, moe_br_e, moe_w_gate, moe_w_up, moe_w_down):
    for l in range(DEPTH):
        i = l // 2
        h = rms_norm(x, norm_mix[l])
        if l % 2 == 0:
            x = x + even_mixer(h, ev_w_in[i], ev_w_out[i], a_ln_g[i], a_ln_b[i], a_w_s[i], a_b_s[i],
                               b_q_gain[i], b_k_gain[i], b_rpb[i])
        else:
            x = x + odd_mixer(h, od_w_in[i], od_w_out[i], c_q_gain[i], c_k_gain[i], t5_table)
        h = rms_norm(x, norm_ffn[l])
        x = x + hier_moe(h, moe_wr_g[l], moe_br_g[l], moe_wr_e[l], moe_br_e[l],
                         moe_w_gate[l], moe_w_up[l], moe_w_down[l])
    return x


def setup_inputs(seed: int = 0) -> dict:
    key = jax.random.key(seed)
    ks = jax.random.split(key, 26)

    def nrm(k, shape, scale):
        return jax.random.normal(k, shape, jnp.float32) * scale

    ev_in = 2 * A_WIDTH + 3 * B_WIDTH
    return {
        'x_prompt': nrm(ks[0], (BATCH, SEQ, D_MODEL), 1.0),
        'x_sample': nrm(ks[1], (DEC_BATCH, DEC_SEQ, D_MODEL), 1.0),
        'norm_mix': 1.0 + nrm(ks[2], (DEPTH, D_MODEL), 0.02),
        'norm_ffn': 1.0 + nrm(ks[3], (DEPTH, D_MODEL), 0.02),
        'ev_w_in': nrm(ks[4], (N_EVEN, D_MODEL, ev_in), D_MODEL ** -0.5),
        'ev_w_out': nrm(ks[5], (N_EVEN, A_WIDTH + B_WIDTH, D_MODEL), (A_WIDTH + B_WIDTH) ** -0.5),
        'a_ln_g': 1.0 + nrm(ks[6], (N_EVEN, A_WIDTH), 0.02),
        'a_ln_b': nrm(ks[7], (N_EVEN, A_WIDTH), 0.02),
        'a_w_s': nrm(ks[8], (N_EVEN, A_GROUPS, CHUNK, CHUNK), CHUNK ** -0.5),
        'a_b_s': 1.0 + nrm(ks[9], (N_EVEN, A_GROUPS, CHUNK), 0.02),
        'b_q_gain': 1.0 + nrm(ks[10], (N_EVEN, B_DH), 0.02),
        'b_k_gain': 1.0 + nrm(ks[11], (N_EVEN, B_DH), 0.02),
        'b_rpb': nrm(ks[12], (N_EVEN, B_HEADS, 2 * NA_KH_MAX - 1, 2 * NA_KW - 1), 0.1),
        'od_w_in': nrm(ks[13], (N_ODD, D_MODEL, 3 * C_WIDTH), D_MODEL ** -0.5),
        'od_w_out': nrm(ks[14], (N_ODD, C_WIDTH, D_MODEL), C_WIDTH ** -0.5),
        'c_q_gain': 1.0 + nrm(ks[15], (N_ODD, C_DH), 0.02),
        'c_k_gain': 1.0 + nrm(ks[16], (N_ODD, C_DH), 0.02),
        't5_table': nrm(ks[17], (T5_BUCKETS, C_HEADS), 0.1),
        'moe_wr_g': nrm(ks[18], (DEPTH, D_MODEL, MOE_GROUPS), D_MODEL ** -0.5),
        'moe_br_g': nrm(ks[19], (DEPTH, MOE_GROUPS), 0.01),
        'moe_wr_e': nrm(ks[20], (DEPTH, D_MODEL, MOE_EXPERTS), D_MODEL ** -0.5),
        'moe_br_e': nrm(ks[21], (DEPTH, MOE_EXPERTS), 0.01),
        'moe_w_gate': nrm(ks[22], (DEPTH, MOE_EXPERTS, D_MODEL, D_EXPERT), D_MODEL ** -0.5),
        'moe_w_up': nrm(ks[23], (DEPTH, MOE_EXPERTS, D_MODEL, D_EXPERT), D_MODEL ** -0.5),
        'moe_w_down': nrm(ks[24], (DEPTH, MOE_EXPERTS, D_EXPERT, D_MODEL), D_EXPERT ** -0.5),
    }


def reference(x_prompt, x_sample, norm_mix, norm_ffn, ev_w_in, ev_w_out, a_ln_g, a_ln_b, a_w_s, a_b_s,
              b_q_gain, b_k_gain, b_rpb, od_w_in, od_w_out, c_q_gain, c_k_gain, t5_table,
              moe_wr_g, moe_br_g, moe_wr_e, moe_br_e, moe_w_gate, moe_w_up, moe_w_down):
    y_prompt = encoder_trunk(x_prompt, norm_mix, norm_ffn, ev_w_in, ev_w_out, a_ln_g, a_ln_b, a_w_s, a_b_s,
                             b_q_gain, b_k_gain, b_rpb, od_w_in, od_w_out, c_q_gain, c_k_gain, t5_table,
                             moe_wr_g, moe_br_g, moe_wr_e, moe_br_e, moe_w_gate, moe_w_up, moe_w_down)
    y_sample = encoder_trunk(x_sample, norm_mix, norm_ffn, ev_w_in, ev_w_out, a_ln_g, a_ln_b, a_w_s, a_b_s,
                             b_q_gain, b_k_gain, b_rpb, od_w_in, od_w_out, c_q_gain, c_k_gain, t5_table,
                             moe_wr_g, moe_br_g, moe_wr_e, moe_br_e, moe_w_gate, moe_w_up, moe_w_down)
    return (y_prompt, y_sample)
```

```python
import functools
import math

import jax
import jax.numpy as jnp
from jax import lax
from jax.experimental import pallas as pl
from jax.experimental.pallas import tpu as pltpu

F32 = jnp.float32
BF16 = jnp.bfloat16
EPS = 1e-6
NEG = -1e30

V7X_VMEM_LIMIT_BYTES = 56 * 1024 * 1024
LANES = 128

GRID_W = 64
CHUNK = 128
A_GROUPS = 8
NA_KH = 8
NA_KW = 16
B_DH = 64
C_DH = 128
C_PATTERNS = ((128, 1), (512, 4), (2048, 16))
C_RAD = 64
C_TQ = 128
C_SUPER = C_TQ * 16
T5_BUCKETS = 32
T5_MAX_DIST = 1024
MOE_GROUPS = 4
MOE_PER_GROUP = 8
MOE_EXPERTS = MOE_GROUPS * MOE_PER_GROUP
MOE_TOPK = 2
MOE_TM = 256
ROUTER_LANES = 128


def _params(*sem):
    return pltpu.CompilerParams(dimension_semantics=sem, vmem_limit_bytes=V7X_VMEM_LIMIT_BYTES)


def _rmsnorm_kernel(x_ref, g_ref, o_ref):
    x = x_ref[...]
    y = x * lax.rsqrt(jnp.mean(x * x, axis=-1, keepdims=True) + EPS)
    o_ref[...] = (y * g_ref[...]).astype(o_ref.dtype)


def rmsnorm(x, g, out_dtype, tm=512):
    n, d = x.shape
    return pl.pallas_call(
        _rmsnorm_kernel,
        grid=(n // tm,),
        in_specs=[pl.BlockSpec((tm, d), lambda i: (i, 0)), pl.BlockSpec((1, d), lambda i: (0, 0))],
        out_specs=pl.BlockSpec((tm, d), lambda i: (i, 0)),
        out_shape=jax.ShapeDtypeStruct((n, d), out_dtype),
        compiler_params=_params("parallel"),
        name="rmsnorm",
    )(x, g.reshape(1, d))


def _matmul_kernel(*refs, n_x, has_res, k_chunk):
    x_refs = refs[:n_x]
    w_refs = refs[n_x:2 * n_x]
    pos = 2 * n_x
    res_ref = refs[pos] if has_res else None
    pos += int(has_res)
    o_ref = refs[pos]
    wb_refs = refs[pos + 1:]

    @pl.when(pl.program_id(1) == 0)
    def _():
        for w_ref, wb_ref in zip(w_refs, wb_refs):
            def cast(c, carry, w_ref=w_ref, wb_ref=wb_ref):
                sl = pl.ds(pl.multiple_of(c * k_chunk, k_chunk), k_chunk)
                wb_ref[sl, :] = w_ref[sl, :].astype(BF16)
                return carry
            lax.fori_loop(0, w_ref.shape[0] // k_chunk, cast, 0)

    acc = None
    for x_ref, wb_ref in zip(x_refs, wb_refs):
        part = jnp.dot(x_ref[...], wb_ref[...], preferred_element_type=F32)
        acc = part if acc is None else acc + part
    if has_res:
        acc = res_ref[...] + acc
    o_ref[...] = acc.astype(o_ref.dtype)


def matmul(xs, ws, residual=None, out_dtype=F32, tm=512, tn=1024):
    n = xs[0].shape[0]
    m = ws[0].shape[1]
    tn = min(tn, m)
    in_specs = [pl.BlockSpec((tm, x.shape[1]), lambda j, i: (i, 0)) for x in xs]
    in_specs += [pl.BlockSpec((w.shape[0], tn), lambda j, i: (0, j)) for w in ws]
    args = list(xs) + list(ws)
    if residual is not None:
        in_specs.append(pl.BlockSpec((tm, tn), lambda j, i: (i, j)))
        args.append(residual)
    kern = functools.partial(_matmul_kernel, n_x=len(xs), has_res=residual is not None, k_chunk=256)
    return pl.pallas_call(
        kern,
        grid=(m // tn, n // tm),
        in_specs=in_specs,
        out_specs=pl.BlockSpec((tm, tn), lambda j, i: (i, j)),
        out_shape=jax.ShapeDtypeStruct((n, m), out_dtype),
        scratch_shapes=[pltpu.VMEM((w.shape[0], tn), BF16) for w in ws],
        compiler_params=_params("arbitrary", "arbitrary"),
        name="matmul",
    )(*args)


def _mixer_a_kernel(u_ref, v_ref, lng_ref, lnb_ref, ws_ref, bs_ref, o_ref):
    tm = u_ref.shape[0]
    gd = u_ref.shape[1] // A_GROUPS
    for c in range(tm // CHUNK):
        rows = slice(c * CHUNK, (c + 1) * CHUNK)
        u = jax.nn.gelu(u_ref[rows, :])
        v = jax.nn.gelu(v_ref[rows, :])
        mu = jnp.mean(v, axis=-1, keepdims=True)
        vc = v - mu
        var = jnp.mean(vc * vc, axis=-1, keepdims=True)
        v = vc * lax.rsqrt(var + EPS) * lng_ref[...] + lnb_ref[...]
        for g in range(A_GROUPS):
            cols = slice(g * gd, (g + 1) * gd)
            f = jnp.dot(ws_ref[g].astype(BF16), v[:, cols].astype(BF16), preferred_element_type=F32)
            f = f + bs_ref[g]
            o_ref[rows, cols] = (u[:, cols] * f).astype(o_ref.dtype)


def mixer_a(z, ln_g, ln_b, w_s, b_s, tm=512):
    n = z.shape[0]
    aw = ln_g.shape[0]
    gd = aw // A_GROUPS
    bs_b = jnp.broadcast_to(b_s[:, :, None], (A_GROUPS, CHUNK, gd))
    return pl.pallas_call(
        _mixer_a_kernel,
        grid=(n // tm,),
        in_specs=[
            pl.BlockSpec((tm, aw), lambda i: (i, 0)),
            pl.BlockSpec((tm, aw), lambda i: (i, 1)),
            pl.BlockSpec((1, aw), lambda i: (0, 0)),
            pl.BlockSpec((1, aw), lambda i: (0, 0)),
            pl.BlockSpec((A_GROUPS, CHUNK, CHUNK), lambda i: (0, 0, 0)),
            pl.BlockSpec((A_GROUPS, CHUNK, gd), lambda i: (0, 0, 0)),
        ],
        out_specs=pl.BlockSpec((tm, aw), lambda i: (i, 0)),
        out_shape=jax.ShapeDtypeStruct((n, aw), BF16),
        compiler_params=_params("parallel"),
        name="mixer_a",
    )(z, z, ln_g.reshape(1, aw), ln_b.reshape(1, aw), w_s, bs_b)


def _pair_rmsnorm(x, gain):
    sq = x * x
    lane = lax.broadcasted_iota(jnp.int32, x.shape, 1)
    lo = lane < B_DH
    s_lo = jnp.sum(jnp.where(lo, sq, 0.0), axis=-1, keepdims=True)
    s_hi = jnp.sum(jnp.where(lo, 0.0, sq), axis=-1, keepdims=True)
    ms = jnp.where(lo, s_lo, s_hi) * (1.0 / B_DH)
    return x * lax.rsqrt(ms + EPS) * gain


def _natten_kernel(q_ref, k_ref, v_ref, gq_ref, gk_ref, bias_ref, o_ref, kn_ref, vb_ref, *, rows_per_step, rows):
    seq = k_ref.shape[0]
    t = pl.program_id(2)
    prep = 256

    @pl.when(t == 0)
    def _():
        def body(c, carry):
            sl = pl.ds(pl.multiple_of(c * prep, prep), prep)
            kn_ref[sl, :] = _pair_rmsnorm(k_ref[sl, :], gk_ref[...]).astype(BF16)
            vb_ref[sl, :] = v_ref[sl, :].astype(BF16)
            return carry
        lax.fori_loop(0, seq // prep, body, 0)

    q = (_pair_rmsnorm(q_ref[...], gq_ref[...]) * (B_DH ** -0.5)).astype(BF16)
    win = NA_KH * GRID_W
    for rr in range(rows_per_step):
        r = t * rows_per_step + rr
        rs = jnp.clip(r - NA_KH // 2, 0, rows - NA_KH)
        case = r - rs
        ksl = pl.ds(pl.multiple_of(rs * GRID_W, GRID_W), win)
        kk = kn_ref[ksl, :]
        vv = vb_ref[ksl, :]
        outs = []
        for hh in range(2):
            lanes = slice(hh * B_DH, (hh + 1) * B_DH)
            qh = q[rr * GRID_W:(rr + 1) * GRID_W, lanes]
            s = lax.dot_general(qh, kk[:, lanes], (((1,), (1,)), ((), ())), preferred_element_type=F32)
            s = s + bias_ref[hh, case]
            m = jnp.max(s, axis=-1, keepdims=True)
            e = jnp.exp(s - m)
            l = jnp.sum(e, axis=-1, keepdims=True)
            o = jnp.dot(e.astype(BF16), vv[:, lanes], preferred_element_type=F32)
            outs.append(o / l)
        o_ref[rr * GRID_W:(rr + 1) * GRID_W, :] = jnp.concatenate(outs, axis=-1).astype(o_ref.dtype)


def natten_bias_table(rpb):
    col = jnp.arange(GRID_W)
    col_start = jnp.clip(col - NA_KW // 2, 0, GRID_W - NA_KW)
    col_valid = (col[None, :] >= col_start[:, None]) & (col[None, :] < col_start[:, None] + NA_KW)
    dc_idx = jnp.clip(col[None, :] - col[:, None] + NA_KW - 1, 0, 2 * NA_KW - 2)
    bias_c = rpb.astype(F32)[:, :, dc_idx]
    dr = jnp.arange(NA_KH)[None, :] - jnp.arange(NA_KH)[:, None] + NA_KH - 1
    tab = bias_c[:, dr]
    tab = tab.transpose(0, 1, 3, 2, 4).reshape(rpb.shape[0], NA_KH, GRID_W, NA_KH * GRID_W)
    return jnp.where(jnp.tile(col_valid, (1, NA_KH)), tab, NEG)


def natten(z, col0, row0, batch, seq, gq, gk, bias_tab, rows_per_step=8):
    heads = bias_tab.shape[0]
    hp = heads // 2
    rows = seq // GRID_W
    assert rows >= NA_KH and rows % rows_per_step == 0 and row0 % seq == 0
    tq = rows_per_step * GRID_W
    steps = rows // rows_per_step
    qb0 = row0 // tq
    sb0 = row0 // seq
    kern = functools.partial(_natten_kernel, rows_per_step=rows_per_step, rows=rows)
    gq2 = jnp.tile(gq, 2).reshape(1, 2 * B_DH)
    gk2 = jnp.tile(gk, 2).reshape(1, 2 * B_DH)
    return pl.pallas_call(
        kern,
        grid=(batch, hp, steps),
        in_specs=[
            pl.BlockSpec((tq, LANES), lambda b, h, t: (qb0 + b * steps + t, col0 + h)),
            pl.BlockSpec((seq, LANES), lambda b, h, t: (sb0 + b, col0 + hp + h)),
            pl.BlockSpec((seq, LANES), lambda b, h, t: (sb0 + b, col0 + 2 * hp + h)),
            pl.BlockSpec((1, LANES), lambda b, h, t: (0, 0)),
            pl.BlockSpec((1, LANES), lambda b, h, t: (0, 0)),
            pl.BlockSpec((2, NA_KH, GRID_W, NA_KH * GRID_W), lambda b, h, t: (h, 0, 0, 0)),
        ],
        out_specs=pl.BlockSpec((tq, LANES), lambda b, h, t: (b * steps + t, h)),
        out_shape=jax.ShapeDtypeStruct((batch * seq, heads * B_DH), BF16),
        scratch_shapes=[pltpu.VMEM((seq, LANES), BF16), pltpu.VMEM((seq, LANES), BF16)],
        compiler_params=_params("parallel", "parallel", "arbitrary"),
        name="natten",
    )(z, z, z, gq2, gk2, bias_tab)


def t5_bucket(rel):
    nb = T5_BUCKETS // 2
    max_exact = nb // 2
    ret = jnp.where(rel > 0, nb, 0)
    n = jnp.abs(rel)
    large = max_exact + (jnp.log(jnp.maximum(n, 1).astype(F32) / max_exact)
                         / math.log(T5_MAX_DIST / max_exact) * (nb - max_exact)).astype(jnp.int32)
    large = jnp.minimum(large, nb - 1)
    return ret + jnp.where(n < max_exact, n, large)


def dilated_bias_table(t5_table, dil, tk):
    q = jnp.arange(C_TQ)[:, None]
    k = jnp.arange(tk)[None, :]
    cases = []
    for off in (0, C_RAD, 2 * C_RAD):
        rel = k - q - off
        b = t5_table.astype(F32)[t5_bucket(rel * dil)]
        b = jnp.where((jnp.abs(rel) <= C_RAD)[..., None], b, NEG)
        cases.append(b.transpose(2, 0, 1))
    return jnp.stack(cases, axis=1)


def _row_rmsnorm(x, gain):
    return x * lax.rsqrt(jnp.mean(x * x, axis=-1, keepdims=True) + EPS) * gain


def _dilated_kernel(q_ref, k_ref, v_ref, gq_ref, gk_ref, b0_ref, b1_ref, b2_ref, o_ref,
                    kn_ref, qn_ref, oacc_ref, lse_ref, *, seq):
    t = pl.program_id(2)
    prep = 256

    @pl.when(t == 0)
    def _():
        def body(c, carry):
            sl = pl.ds(pl.multiple_of(c * prep, prep), prep)
            kn_ref[sl, :] = _row_rmsnorm(k_ref[sl, :], gk_ref[...])
            return carry
        lax.fori_loop(0, seq // prep, body, 0)

    qn_ref[...] = _row_rmsnorm(q_ref[...], gq_ref[...]) * (C_DH ** -0.5)

    for p, ((window, dil), b_ref) in enumerate(zip(C_PATTERNS, (b0_ref, b1_ref, b2_ref))):
        n_sub = seq // dil
        tk = b_ref.shape[-1]
        span = C_TQ * dil

        def sub_tile(idx, carry, p=p, dil=dil, n_sub=n_sub, tk=tk, span=span, b_ref=b_ref):
            u = idx // dil
            s = idx % dil
            qpos = u * span + s
            j0 = t * (C_SUPER // dil) + u * C_TQ
            ws = jnp.clip(j0 - C_RAD, 0, n_sub - tk)
            case = (j0 - ws) // C_RAD
            kpos = ws * dil + s
            if dil == 1:
                qsl = pl.ds(pl.multiple_of(qpos, C_TQ), C_TQ)
                ksl = pl.ds(pl.multiple_of(kpos, C_RAD), tk)
            else:
                qsl = pl.ds(qpos, C_TQ, stride=dil)
                ksl = pl.ds(kpos, tk, stride=dil)
            q = qn_ref[qsl, :].astype(BF16)
            k = kn_ref[ksl, :].astype(BF16)
            v = v_ref[ksl, :].astype(BF16)
            sc = lax.dot_general(q, k, (((1,), (1,)), ((), ())), preferred_element_type=F32)
            sc = sc + b_ref[0, case]
            m = jnp.max(sc, axis=-1, keepdims=True)
            e = jnp.exp(sc - m)
            l = jnp.sum(e, axis=-1, keepdims=True)
            o = jnp.dot(e.astype(BF16), v, preferred_element_type=F32) / l
            oacc_ref[p, qsl, :] = o
            lse_ref[p, qsl, :] = jnp.broadcast_to(m + jnp.log(l), o.shape)
            return carry

        lax.fori_loop(0, C_SUPER // C_TQ, sub_tile, 0)

    l0, l1, l2 = lse_ref[0], lse_ref[1], lse_ref[2]
    mx = jnp.maximum(jnp.maximum(l0, l1), l2)
    w0, w1, w2 = jnp.exp(l0 - mx), jnp.exp(l1 - mx), jnp.exp(l2 - mx)
    num = w0 * oacc_ref[0] + w1 * oacc_ref[1] + w2 * oacc_ref[2]
    o_ref[...] = (num / (w0 + w1 + w2)).astype(o_ref.dtype)


def dilated_attention(qkv, row0, batch, seq, gq, gk, tabs):
    heads = tabs[0].shape[0]
    assert seq % C_SUPER == 0 and row0 % seq == 0
    steps = seq // C_SUPER
    qb0 = row0 // C_SUPER
    sb0 = row0 // seq
    kern = functools.partial(_dilated_kernel, seq=seq)
    tab_specs = [pl.BlockSpec((1,) + tuple(tb.shape[1:]), lambda b, h, t: (h, 0, 0, 0)) for tb in tabs]
    return pl.pallas_call(
        kern,
        grid=(batch, heads, steps),
        in_specs=[
            pl.BlockSpec((C_SUPER, LANES), lambda b, h, t: (qb0 + b * steps + t, h)),
            pl.BlockSpec((seq, LANES), lambda b, h, t: (sb0 + b, heads + h)),
            pl.BlockSpec((seq, LANES), lambda b, h, t: (sb0 + b, 2 * heads + h)),
            pl.BlockSpec((1, LANES), lambda b, h, t: (0, 0)),
            pl.BlockSpec((1, LANES), lambda b, h, t: (0, 0)),
        ] + tab_specs,
        out_specs=pl.BlockSpec((C_SUPER, LANES), lambda b, h, t: (b * steps + t, h)),
        out_shape=jax.ShapeDtypeStruct((batch * seq, heads * C_DH), BF16),
        scratch_shapes=[
            pltpu.VMEM((seq, LANES), F32),
            pltpu.VMEM((C_SUPER, LANES), F32),
            pltpu.VMEM((len(C_PATTERNS), C_SUPER, LANES), F32),
            pltpu.VMEM((len(C_PATTERNS), C_SUPER, LANES), F32),
        ],
        compiler_params=_params("parallel", "parallel", "arbitrary"),
        name="dilated",
    )(qkv, qkv, qkv, gq.reshape(1, C_DH), gk.reshape(1, C_DH), *tabs)


def _norm_router_kernel(x_ref, g_ref, wr_ref, br_ref, h_ref, logit_ref):
    x = x_ref[...]
    h = x * lax.rsqrt(jnp.mean(x * x, axis=-1, keepdims=True) + EPS) * g_ref[...]
    h_ref[...] = h
    logit_ref[...] = jnp.dot(h, wr_ref[...], preferred_element_type=F32,
                             precision=lax.Precision.HIGHEST) + br_ref[...]


def norm_router(x, g, wr, br, tm=256):
    n, d = x.shape
    return pl.pallas_call(
        _norm_router_kernel,
        grid=(n // tm,),
        in_specs=[
            pl.BlockSpec((tm, d), lambda i: (i, 0)),
            pl.BlockSpec((1, d), lambda i: (0, 0)),
            pl.BlockSpec((d, ROUTER_LANES), lambda i: (0, 0)),
            pl.BlockSpec((1, ROUTER_LANES), lambda i: (0, 0)),
        ],
        out_specs=[pl.BlockSpec((tm, d), lambda i: (i, 0)), pl.BlockSpec((tm, ROUTER_LANES), lambda i: (i, 0))],
        out_shape=[jax.ShapeDtypeStruct((n, d), F32), jax.ShapeDtypeStruct((n, ROUTER_LANES), F32)],
        compiler_params=_params("parallel"),
        name="norm_router",
    )(x, g.reshape(1, d), wr, br)


def _row_gather_copy(src_hbm, dst_ref, sem, src_row, dst_row):
    return pltpu.make_async_copy(src_hbm.at[pl.ds(src_row, 1), :], dst_ref.at[pl.ds(dst_row, 1), :], sem)


def _gather_rows(idx_ref, base, src_hbm, dst_ref, sem, unroll=8):
    rows = dst_ref.shape[0]

    def issue(r, carry):
        _row_gather_copy(src_hbm, dst_ref, sem, idx_ref[base + r], r).start()
        return carry
    lax.fori_loop(0, rows, issue, 0, unroll=unroll)

    def drain(r, carry):
        _row_gather_copy(src_hbm, dst_ref, sem, 0, r).wait()
        return carry
    lax.fori_loop(0, rows, drain, 0, unroll=unroll)


def _gather_kernel(idx_ref, src_hbm, o_ref, sem):
    _gather_rows(idx_ref, pl.program_id(0) * o_ref.shape[0], src_hbm, o_ref, sem.at[0])


def gather_rows(src, idx, tm=MOE_TM):
    p = idx.shape[0]
    d = src.shape[1]
    return pl.pallas_call(
        _gather_kernel,
        grid_spec=pltpu.PrefetchScalarGridSpec(
            num_scalar_prefetch=1,
            grid=(p // tm,),
            in_specs=[pl.BlockSpec(memory_space=pl.ANY)],
            out_specs=pl.BlockSpec((tm, d), lambda i, idx: (i, 0)),
            scratch_shapes=[pltpu.SemaphoreType.DMA((1,))],
        ),
        out_shape=jax.ShapeDtypeStruct((p, d), src.dtype),
        compiler_params=_params("arbitrary"),
        name="moe_gather",
    )(idx, src)


def _ffn_kernel(be_ref, nused_ref, x_ref, gate_ref, wg_ref, wu_ref, wd_ref, o_ref):
    live = pl.program_id(0) < nused_ref[0]

    @pl.when(live)
    def _():
        x = x_ref[...].astype(BF16)
        g = jnp.dot(x, wg_ref[0], preferred_element_type=F32)
        u = jnp.dot(x, wu_ref[0], preferred_element_type=F32)
        h = (jax.nn.silu(g) * u).astype(BF16)
        y = jnp.dot(h, wd_ref[0], preferred_element_type=F32)
        o_ref[...] = y * gate_ref[...]

    @pl.when(jnp.logical_not(live))
    def _():
        o_ref[...] = jnp.zeros_like(o_ref)


def expert_ffn(xb, gate_slot, block_e, n_used, wg, wu, wd, tm=MOE_TM):
    p, d = xb.shape
    de = wg.shape[2]
    return pl.pallas_call(
        _ffn_kernel,
        grid_spec=pltpu.PrefetchScalarGridSpec(
            num_scalar_prefetch=2,
            grid=(p // tm,),
            in_specs=[
                pl.BlockSpec((tm, d), lambda i, be, nu: (i, 0)),
                pl.BlockSpec((tm, 1), lambda i, be, nu: (i, 0)),
                pl.BlockSpec((1, d, de), lambda i, be, nu: (be[i], 0, 0)),
                pl.BlockSpec((1, d, de), lambda i, be, nu: (be[i], 0, 0)),
                pl.BlockSpec((1, de, d), lambda i, be, nu: (be[i], 0, 0)),
            ],
            out_specs=pl.BlockSpec((tm, d), lambda i, be, nu: (i, 0)),
        ),
        out_shape=jax.ShapeDtypeStruct((p, d), F32),
        compiler_params=_params("arbitrary"),
        name="moe_ffn",
    )(block_e, n_used, xb, gate_slot, wg, wu, wd)


def _combine_kernel(d0_ref, d1_ref, x_ref, yb_hbm, o_ref, a_ref, b_ref, sem):
    base = pl.program_id(0) * x_ref.shape[0]
    _gather_rows(d0_ref, base, yb_hbm, a_ref, sem.at[0])
    _gather_rows(d1_ref, base, yb_hbm, b_ref, sem.at[1])
    o_ref[...] = x_ref[...] + (a_ref[...] + b_ref[...])


def moe_combine(x, yb, d0, d1, tm=256):
    n, d = x.shape
    return pl.pallas_call(
        _combine_kernel,
        grid_spec=pltpu.PrefetchScalarGridSpec(
            num_scalar_prefetch=2,
            grid=(n // tm,),
            in_specs=[pl.BlockSpec((tm, d), lambda i, a, b: (i, 0)), pl.BlockSpec(memory_space=pl.ANY)],
            out_specs=pl.BlockSpec((tm, d), lambda i, a, b: (i, 0)),
            scratch_shapes=[pltpu.VMEM((tm, d), F32), pltpu.VMEM((tm, d), F32), pltpu.SemaphoreType.DMA((2,))],
        ),
        out_shape=jax.ShapeDtypeStruct((n, d), F32),
        compiler_params=_params("arbitrary"),
        name="moe_combine",
    )(d0, d1, x, yb)


def moe_routing(logits, tm=MOE_TM):
    n = logits.shape[0]
    m = n * MOE_TOPK
    lg = logits[:, :MOE_GROUPS]
    le = logits[:, MOE_GROUPS:MOE_GROUPS + MOE_EXPERTS].reshape(n, MOE_GROUPS, MOE_PER_GROUP)
    g_sel = jnp.argmax(lg, axis=-1)
    p_g = jnp.take_along_axis(jax.nn.softmax(lg, axis=-1), g_sel[:, None], axis=-1)
    le_sel = jnp.take_along_axis(le, g_sel[:, None, None], axis=1)[:, 0]
    top_v, top_i = lax.top_k(le_sel, MOE_TOPK)
    gates = p_g * jax.nn.softmax(top_v, axis=-1)
    expert = (g_sel[:, None] * MOE_PER_GROUP + top_i).astype(jnp.int32)

    flat_e = expert.reshape(m)
    onehot = (flat_e[:, None] == jnp.arange(MOE_EXPERTS, dtype=jnp.int32)[None, :]).astype(jnp.int32)
    csum = jnp.cumsum(onehot, axis=0)
    rank = jnp.sum(csum * onehot, axis=1) - 1
    counts = csum[-1]
    padded = (counts + tm - 1) // tm * tm
    pad_end = jnp.cumsum(padded)
    pad_start = pad_end - padded
    dest = (pad_start[flat_e] + rank).astype(jnp.int32)
    n_blocks = (m + MOE_EXPERTS * (tm - 1) + tm - 1) // tm
    p = n_blocks * tm
    flat_tok = jnp.arange(m, dtype=jnp.int32) // MOE_TOPK
    slot_tok = jnp.zeros((p,), jnp.int32).at[dest].set(flat_tok)
    gate_slot = jnp.zeros((p,), F32).at[dest].set(gates.reshape(m))
    block_e = jnp.minimum(jnp.searchsorted(pad_end, jnp.arange(n_blocks, dtype=jnp.int32) * tm, side='right'),
                          MOE_EXPERTS - 1).astype(jnp.int32)
    n_used = (pad_end[-1] // tm).astype(jnp.int32).reshape(1)
    dest2 = dest.reshape(n, MOE_TOPK)
    return slot_tok, gate_slot.reshape(p, 1), block_e, n_used, dest2[:, 0], dest2[:, 1]


def hier_moe_residual(x, g, wr_g, br_g, wr_e, br_e, w_gate, w_up, w_down):
    d = x.shape[1]
    pad = ROUTER_LANES - MOE_GROUPS - MOE_EXPERTS
    wr = jnp.concatenate([wr_g, wr_e, jnp.zeros((d, pad), F32)], axis=1)
    br = jnp.concatenate([br_g, br_e, jnp.zeros((pad,), F32)]).reshape(1, ROUTER_LANES)
    h, logits = norm_router(x, g, wr, br)
    slot_tok, gate_slot, block_e, n_used, d0, d1 = moe_routing(logits)
    xb = gather_rows(h, slot_tok)
    yb = expert_ffn(xb, gate_slot, block_e, n_used, w_gate.astype(BF16), w_up.astype(BF16), w_down.astype(BF16))
    return moe_combine(x, yb, d0, d1)


def kernel(x_prompt, x_sample, norm_mix, norm_ffn, ev_w_in, ev_w_out, a_ln_g, a_ln_b, a_w_s, a_b_s, b_q_gain, b_k_gain, b_rpb, od_w_in, od_w_out, c_q_gain, c_k_gain, t5_table, moe_wr_g, moe_br_g, moe_wr_e, moe_br_e, moe_w_gate, moe_w_up, moe_w_down):
    d = x_prompt.shape[-1]
    segs = [(x_prompt.shape[0], x_prompt.shape[1]), (x_sample.shape[0], x_sample.shape[1])]
    x = jnp.concatenate([x_prompt.reshape(-1, d), x_sample.reshape(-1, d)], axis=0)
    depth = norm_mix.shape[0]
    for l in range(depth):
        i = l // 2
        h = rmsnorm(x, norm_mix[l], BF16)
        if l % 2 == 0:
            aw = a_ln_g.shape[1]
            z = matmul([h], [ev_w_in[i]])
            a_out = mixer_a(z, a_ln_g[i], a_ln_b[i], a_w_s[i], a_b_s[i])
            tab = natten_bias_table(b_rpb[i])
            b_parts, row0 = [], 0
            for batch, seq in segs:
                b_parts.append(natten(z, 2 * aw // LANES, row0, batch, seq, b_q_gain[i], b_k_gain[i], tab))
                row0 += batch * seq
            b_out = jnp.concatenate(b_parts, axis=0)
            w_out = ev_w_out[i]
            x = matmul([a_out, b_out], [w_out[:aw], w_out[aw:]], residual=x)
        else:
            qkv = matmul([h], [od_w_in[i]])
            c_parts, row0 = [], 0
            for batch, seq in segs:
                tabs = [dilated_bias_table(t5_table, dil, min(C_TQ + 2 * C_RAD, seq // dil)) for _, dil in C_PATTERNS]
                c_parts.append(dilated_attention(qkv, row0, batch, seq, c_q_gain[i], c_k_gain[i], tabs))
                row0 += batch * seq
            x = matmul([jnp.concatenate(c_parts, axis=0)], [od_w_out[i]], residual=x)
        x = hier_moe_residual(x, norm_ffn[l], moe_wr_g[l], moe_br_g[l], moe_wr_e[l], moe_br_e[l],
                              moe_w_gate[l], moe_w_up[l], moe_w_down[l])
    n0 = segs[0][0] * segs[0][1]
    return (x[:n0].reshape(x_prompt.shape), x[n0:].reshape(x_sample.shape))
```

```python
import functools
import math

import jax
import jax.numpy as jnp
from jax import lax
from jax.experimental import pallas as pl
from jax.experimental.pallas import tpu as pltpu

F32 = jnp.float32
BF16 = jnp.bfloat16
EPS = 1e-6
NEG = -1e30

V7X_VMEM_LIMIT_BYTES = 56 * 1024 * 1024
LANES = 128

GRID_W = 64
CHUNK = 128
A_GROUPS = 8
NA_KH = 8
NA_KW = 16
B_DH = 64
C_DH = 128
C_PATTERNS = ((128, 1), (512, 4), (2048, 16))
C_RAD = 64
C_TQ = 128
C_TK = C_TQ + 2 * C_RAD
C_SUPER = C_TQ * 16
T5_BUCKETS = 32
T5_MAX_DIST = 1024
MOE_GROUPS = 4
MOE_PER_GROUP = 8
MOE_EXPERTS = MOE_GROUPS * MOE_PER_GROUP
MOE_TOPK = 2
MOE_TM = 256
ROUTER_LANES = 128


def _params(*sem):
    return pltpu.CompilerParams(dimension_semantics=sem, vmem_limit_bytes=V7X_VMEM_LIMIT_BYTES)


def _row_sources(arrays, tm, width, col_of):
    specs, spans, off = [], [], 0
    for a in arrays:
        nblk = a.shape[0] // tm
        specs.append(pl.BlockSpec(
            (tm, width), lambda *g, off=off, nblk=nblk: (jnp.clip(g[-1] - off, 0, nblk - 1), col_of(*g))))
        spans.append((off, nblk))
        off += nblk
    return specs, spans


def _active_rows(i, refs, spans):
    val = refs[-1][...]
    for ref, (off, nblk) in reversed(list(zip(refs[:-1], spans[:-1]))):
        val = jnp.where(i < off + nblk, ref[...], val)
    return val


def _rmsnorm_kernel(*refs, spans):
    x_refs, (g_ref, o_ref) = refs[:len(spans)], refs[len(spans):]
    x = _active_rows(pl.program_id(0), x_refs, spans)
    y = x * lax.rsqrt(jnp.mean(x * x, axis=-1, keepdims=True) + EPS)
    o_ref[...] = (y * g_ref[...]).astype(o_ref.dtype)


def rmsnorm(xs, g, out_dtype, tm=512):
    d = xs[0].shape[1]
    n = sum(x.shape[0] for x in xs)
    specs, spans = _row_sources(xs, tm, d, lambda i: 0)
    return pl.pallas_call(
        functools.partial(_rmsnorm_kernel, spans=spans),
        grid=(n // tm,),
        in_specs=specs + [pl.BlockSpec((1, d), lambda i: (0, 0))],
        out_specs=pl.BlockSpec((tm, d), lambda i: (i, 0)),
        out_shape=jax.ShapeDtypeStruct((n, d), out_dtype),
        compiler_params=_params("arbitrary"),
        name="rmsnorm",
    )(*xs, g.reshape(1, d))


def _matmul_kernel(*refs, x_spans, res_spans, k_chunk):
    refs = list(refs)
    x_refs = [[refs.pop(0) for _ in spans] for spans in x_spans]
    w_refs = [refs.pop(0) for _ in x_spans]
    res_refs = [refs.pop(0) for _ in res_spans]
    o_ref, wb_refs = refs[0], refs[1:]
    i = pl.program_id(1)

    @pl.when(pl.program_id(1) == 0)
    def _():
        for w_ref, wb_ref in zip(w_refs, wb_refs):
            rows = math.gcd(k_chunk, w_ref.shape[0])

            def cast(c, carry, w_ref=w_ref, wb_ref=wb_ref, rows=rows):
                sl = pl.ds(pl.multiple_of(c * rows, rows), rows)
                wb_ref[sl, :] = w_ref[sl, :].astype(BF16)
                return carry
            lax.fori_loop(0, w_ref.shape[0] // rows, cast, 0)

    acc = None
    for pieces, spans, wb_ref in zip(x_refs, x_spans, wb_refs):
        part = jnp.dot(_active_rows(i, pieces, spans), wb_ref[...], preferred_element_type=F32)
        acc = part if acc is None else acc + part
    if res_spans:
        acc = _active_rows(i, res_refs, res_spans) + acc
    o_ref[...] = acc.astype(o_ref.dtype)


def matmul(xs, ws, residuals=(), out_dtype=F32, tm=512, tn=1024):
    n = sum(x.shape[0] for x in xs[0])
    m = ws[0][0].shape[1]
    tn = min(tn, m)
    in_specs, x_spans, w_specs = [], [], []
    for pieces, (w, row) in zip(xs, ws):
        k = pieces[0].shape[1]
        assert row % k == 0
        specs, spans = _row_sources(pieces, tm, k, lambda j, i: 0)
        in_specs += specs
        x_spans.append(spans)
        w_specs.append(pl.BlockSpec((k, tn), lambda j, i, rb=row // k: (rb, j)))
    res_specs, res_spans = _row_sources(list(residuals), tm, tn, lambda j, i: j)
    kern = functools.partial(_matmul_kernel, x_spans=x_spans, res_spans=res_spans, k_chunk=256)
    return pl.pallas_call(
        kern,
        grid=(m // tn, n // tm),
        in_specs=in_specs + w_specs + res_specs,
        out_specs=pl.BlockSpec((tm, tn), lambda j, i: (i, j)),
        out_shape=jax.ShapeDtypeStruct((n, m), out_dtype),
        scratch_shapes=[pltpu.VMEM((pieces[0].shape[1], tn), BF16) for pieces in xs],
        compiler_params=_params("arbitrary", "arbitrary"),
        name="matmul",
    )(*[p for pieces in xs for p in pieces], *[w for w, _ in ws], *residuals)


def _mixer_a_kernel(u_ref, v_ref, lng_ref, lnb_ref, ws_ref, bs_ref, o_ref):
    tm = u_ref.shape[0]
    gd = u_ref.shape[1] // A_GROUPS
    for c in range(tm // CHUNK):
        rows = slice(c * CHUNK, (c + 1) * CHUNK)
        u = jax.nn.gelu(u_ref[rows, :])
        v = jax.nn.gelu(v_ref[rows, :])
        mu = jnp.mean(v, axis=-1, keepdims=True)
        vc = v - mu
        var = jnp.mean(vc * vc, axis=-1, keepdims=True)
        v = vc * lax.rsqrt(var + EPS) * lng_ref[...] + lnb_ref[...]
        for g in range(A_GROUPS):
            cols = slice(g * gd, (g + 1) * gd)
            f = jnp.dot(ws_ref[g].astype(BF16), v[:, cols].astype(BF16), preferred_element_type=F32)
            f = f + bs_ref[g]
            o_ref[rows, cols] = (u[:, cols] * f).astype(o_ref.dtype)


def mixer_a(z, ln_g, ln_b, w_s, b_s, tm=512):
    n = z.shape[0]
    aw = ln_g.shape[0]
    gd = aw // A_GROUPS
    bs_b = jnp.broadcast_to(b_s[:, :, None], (A_GROUPS, CHUNK, gd))
    return pl.pallas_call(
        _mixer_a_kernel,
        grid=(n // tm,),
        in_specs=[
            pl.BlockSpec((tm, aw), lambda i: (i, 0)),
            pl.BlockSpec((tm, aw), lambda i: (i, 1)),
            pl.BlockSpec((1, aw), lambda i: (0, 0)),
            pl.BlockSpec((1, aw), lambda i: (0, 0)),
            pl.BlockSpec((A_GROUPS, CHUNK, CHUNK), lambda i: (0, 0, 0)),
            pl.BlockSpec((A_GROUPS, CHUNK, gd), lambda i: (0, 0, 0)),
        ],
        out_specs=pl.BlockSpec((tm, aw), lambda i: (i, 0)),
        out_shape=jax.ShapeDtypeStruct((n, aw), BF16),
        compiler_params=_params("parallel"),
        name="mixer_a",
    )(z, z, ln_g.reshape(1, aw), ln_b.reshape(1, aw), w_s, bs_b)


def _skew(w, rows):
    n = w.shape[-1]
    flat = jnp.tile(w, (1,) * (w.ndim - 1) + (rows,))
    return flat[..., :rows * (n - 1)].reshape(w.shape[:-1] + (rows, n - 1))


def _pair_rmsnorm(x, gain):
    sq = x * x
    lane = lax.broadcasted_iota(jnp.int32, x.shape, 1)
    lo = lane < B_DH
    s_lo = jnp.sum(jnp.where(lo, sq, 0.0), axis=-1, keepdims=True)
    s_hi = jnp.sum(jnp.where(lo, 0.0, sq), axis=-1, keepdims=True)
    ms = jnp.where(lo, s_lo, s_hi) * (1.0 / B_DH)
    return x * lax.rsqrt(ms + EPS) * gain


def _natten_kernel(q_ref, k_ref, v_ref, gq_ref, gk_ref, bias_ref, o_ref, kn_ref, vb_ref, s_ref, m_ref, e_ref,
                   *, rows_per_step, rows):
    seq = k_ref.shape[0]
    t = pl.program_id(2)
    prep = 256

    @pl.when(t == 0)
    def _():
        def body(c, carry):
            sl = pl.ds(pl.multiple_of(c * prep, prep), prep)
            kn_ref[sl, :] = _pair_rmsnorm(k_ref[sl, :], gk_ref[...]).astype(BF16)
            vb_ref[sl, :] = v_ref[sl, :].astype(BF16)
            return carry
        lax.fori_loop(0, seq // prep, body, 0)

    q = (_pair_rmsnorm(q_ref[...], gq_ref[...]) * (B_DH ** -0.5)).astype(BF16)
    lo = lax.broadcasted_iota(jnp.int32, (GRID_W, LANES), 1) < B_DH
    zero = jnp.zeros((GRID_W, LANES), BF16)
    win = NA_KH * GRID_W
    ones = jnp.ones((win, LANES), BF16)

    def window(rr):
        r = t * rows_per_step + rr
        rs = jnp.clip(r - NA_KH // 2, 0, rows - NA_KH)
        return r - rs, pl.ds(pl.multiple_of(rs * GRID_W, GRID_W), win)

    for rr in range(rows_per_step):
        case, ksl = window(rr)
        qr = q[rr * GRID_W:(rr + 1) * GRID_W, :]
        q2 = jnp.concatenate([jnp.where(lo, qr, zero), jnp.where(lo, zero, qr)], axis=0)
        s = lax.dot_general(q2, kn_ref[ksl, :], (((1,), (1,)), ((), ())), preferred_element_type=F32)
        s_ref[rr] = s + bias_ref[0, case]
    for rr in range(rows_per_step):
        m = jnp.max(s_ref[rr], axis=-1, keepdims=True)
        m_ref[rr] = jnp.broadcast_to(m, (2 * GRID_W, LANES))
    for rr in range(rows_per_step):
        mb = m_ref[rr]
        for c in range(win // LANES):
            cols = slice(c * LANES, (c + 1) * LANES)
            e_ref[rr, :, cols] = jnp.exp(s_ref[rr, :, cols] - mb).astype(BF16)
    for rr in range(rows_per_step):
        _, ksl = window(rr)
        e = e_ref[rr]
        o = jnp.dot(e, vb_ref[ksl, :], preferred_element_type=F32)
        l = jnp.dot(e, ones, preferred_element_type=F32)
        o = o / l
        o_ref[rr * GRID_W:(rr + 1) * GRID_W, :] = jnp.where(lo, o[:GRID_W], o[GRID_W:]).astype(o_ref.dtype)


def natten_bias_table(rpb):
    heads = rpb.shape[0]
    col = jnp.arange(GRID_W)
    col_start = jnp.clip(col - NA_KW // 2, 0, GRID_W - NA_KW)
    col_valid = (col[None, :] >= col_start[:, None]) & (col[None, :] < col_start[:, None] + NA_KW)
    edge = GRID_W - NA_KW
    w = jnp.concatenate([jnp.repeat(rpb[..., :1], edge, axis=-1), rpb.astype(F32),
                         jnp.repeat(rpb[..., -1:], edge, axis=-1), jnp.zeros(rpb.shape[:-1] + (1,), F32)], axis=-1)
    bias_c = _skew(w, GRID_W)[..., GRID_W - 1:]
    tab = jnp.stack([bias_c[:, NA_KH - 1 - c:2 * NA_KH - 1 - c] for c in range(NA_KH)], axis=1)
    tab = jnp.where(col_valid[:, None, :], tab.transpose(0, 1, 3, 2, 4), NEG)
    tab = tab.reshape(heads // 2, 2, NA_KH, GRID_W, NA_KH * GRID_W).transpose(0, 2, 1, 3, 4)
    return tab.reshape(heads // 2, NA_KH, 2 * GRID_W, NA_KH * GRID_W)


def natten(z, col0, row0, batch, seq, gq, gk, bias_tab, rows_per_step=8):
    hp = bias_tab.shape[0]
    rows = seq // GRID_W
    assert rows >= NA_KH and rows % rows_per_step == 0 and row0 % seq == 0
    tq = rows_per_step * GRID_W
    steps = rows // rows_per_step
    qb0 = row0 // tq
    sb0 = row0 // seq
    win = NA_KH * GRID_W
    kern = functools.partial(_natten_kernel, rows_per_step=rows_per_step, rows=rows)
    gq2 = jnp.tile(gq, 2).reshape(1, 2 * B_DH)
    gk2 = jnp.tile(gk, 2).reshape(1, 2 * B_DH)
    return pl.pallas_call(
        kern,
        grid=(batch, hp, steps),
        in_specs=[
            pl.BlockSpec((tq, LANES), lambda b, h, t: (qb0 + b * steps + t, col0 + h)),
            pl.BlockSpec((seq, LANES), lambda b, h, t: (sb0 + b, col0 + hp + h)),
            pl.BlockSpec((seq, LANES), lambda b, h, t: (sb0 + b, col0 + 2 * hp + h)),
            pl.BlockSpec((1, LANES), lambda b, h, t: (0, 0)),
            pl.BlockSpec((1, LANES), lambda b, h, t: (0, 0)),
            pl.BlockSpec((1, NA_KH, 2 * GRID_W, win), lambda b, h, t: (h, 0, 0, 0)),
        ],
        out_specs=pl.BlockSpec((tq, LANES), lambda b, h, t: (b * steps + t, h)),
        out_shape=jax.ShapeDtypeStruct((batch * seq, 2 * hp * B_DH), BF16),
        scratch_shapes=[
            pltpu.VMEM((seq, LANES), BF16),
            pltpu.VMEM((seq, LANES), BF16),
            pltpu.VMEM((rows_per_step, 2 * GRID_W, win), F32),
            pltpu.VMEM((rows_per_step, 2 * GRID_W, LANES), F32),
            pltpu.VMEM((rows_per_step, 2 * GRID_W, win), BF16),
        ],
        compiler_params=_params("parallel", "parallel", "arbitrary"),
        name="natten",
    )(z, z, z, gq2, gk2, bias_tab)


def t5_bucket(rel):
    nb = T5_BUCKETS // 2
    max_exact = nb // 2
    ret = jnp.where(rel > 0, nb, 0)
    n = jnp.abs(rel)
    large = max_exact + (jnp.log(jnp.maximum(n, 1).astype(F32) / max_exact)
                         / math.log(T5_MAX_DIST / max_exact) * (nb - max_exact)).astype(jnp.int32)
    large = jnp.minimum(large, nb - 1)
    return ret + jnp.where(n < max_exact, n, large)


def dilated_bias_table(t5_table, dil):
    span = C_TK - 1
    rel = jnp.arange(-span, span + 1)
    vec = t5_table.astype(F32)[t5_bucket(rel * dil)]
    vec = jnp.where((jnp.abs(rel) <= C_RAD)[:, None], vec, NEG).T
    w = jnp.concatenate([vec, jnp.zeros((vec.shape[0], 1), F32)], axis=1)
    r = _skew(w, C_TQ)
    return jnp.stack([r[:, :, span - off:span - off + C_TK] for off in (0, C_RAD, 2 * C_RAD)], axis=1)


def _row_rmsnorm(x, gain):
    return x * lax.rsqrt(jnp.mean(x * x, axis=-1, keepdims=True) + EPS) * gain


def _dilated_kernel(q_ref, k_ref, v_ref, gq_ref, gk_ref, b0_ref, b1_ref, b2_ref, o_ref,
                    kn_ref, qn_ref, oacc_ref, lse_ref, s_ref, m_ref, e_ref, *, seq, unroll):
    t = pl.program_id(2)
    prep = 256

    @pl.when(t == 0)
    def _():
        def body(c, carry):
            sl = pl.ds(pl.multiple_of(c * prep, prep), prep)
            kn_ref[sl, :] = _row_rmsnorm(k_ref[sl, :], gk_ref[...])
            return carry
        lax.fori_loop(0, seq // prep, body, 0)

    qn_ref[...] = _row_rmsnorm(q_ref[...], gq_ref[...]) * (C_DH ** -0.5)
    n_tiles = C_SUPER // C_TQ

    for p, ((window, dil), b_ref) in enumerate(zip(C_PATTERNS, (b0_ref, b1_ref, b2_ref))):
        n_sub = seq // dil
        tk = min(C_TK, n_sub)
        span = C_TQ * dil
        ones = jnp.ones((tk, LANES), BF16)

        def tile(idx, dil=dil, n_sub=n_sub, tk=tk, span=span):
            u = idx // dil
            s = idx % dil
            qpos = u * span + s
            j0 = t * (C_SUPER // dil) + u * C_TQ
            ws = jnp.clip(j0 - C_RAD, 0, n_sub - tk)
            kpos = ws * dil + s
            if dil == 1:
                return (pl.ds(pl.multiple_of(qpos, C_TQ), C_TQ), pl.ds(pl.multiple_of(kpos, C_RAD), tk),
                        (j0 - ws) // C_RAD)
            return pl.ds(qpos, C_TQ, stride=dil), pl.ds(kpos, tk, stride=dil), (j0 - ws) // C_RAD

        def scores(idx, carry, tile=tile, tk=tk, b_ref=b_ref):
            qsl, ksl, case = tile(idx)
            q = qn_ref[qsl, :].astype(BF16)
            k = kn_ref[ksl, :].astype(BF16)
            sc = lax.dot_general(q, k, (((1,), (1,)), ((), ())), preferred_element_type=F32)
            s_ref[idx, :, :tk] = sc + b_ref[0, case, :, :tk]
            return carry

        def row_max(idx, carry, tk=tk):
            m = jnp.max(s_ref[idx, :, :tk], axis=-1, keepdims=True)
            m_ref[idx] = jnp.broadcast_to(m, (C_TQ, LANES))
            return carry

        def exps(idx, carry, tk=tk):
            mb = m_ref[idx]
            for c in range(tk // LANES):
                cols = slice(c * LANES, (c + 1) * LANES)
                e_ref[idx, :, cols] = jnp.exp(s_ref[idx, :, cols] - mb).astype(BF16)
            return carry

        def outputs(idx, carry, tile=tile, tk=tk, ones=ones, p=p):
            qsl, ksl, _ = tile(idx)
            e = e_ref[idx, :, :tk]
            o = jnp.dot(e, v_ref[ksl, :].astype(BF16), preferred_element_type=F32)
            l = jnp.dot(e, ones, preferred_element_type=F32)
            oacc_ref[p, qsl, :] = o / l
            lse_ref[p, qsl, :] = m_ref[idx] + jnp.log(l)
            return carry

        for phase in (scores, row_max, exps, outputs):
            lax.fori_loop(0, n_tiles, phase, 0, unroll=unroll)

    l0, l1, l2 = lse_ref[0], lse_ref[1], lse_ref[2]
    mx = jnp.maximum(jnp.maximum(l0, l1), l2)
    w0, w1, w2 = jnp.exp(l0 - mx), jnp.exp(l1 - mx), jnp.exp(l2 - mx)
    num = w0 * oacc_ref[0] + w1 * oacc_ref[1] + w2 * oacc_ref[2]
    o_ref[...] = (num / (w0 + w1 + w2)).astype(o_ref.dtype)


def dilated_attention(qkv, row0, batch, seq, gq, gk, tabs, unroll=8):
    heads = tabs[0].shape[0]
    assert seq % C_SUPER == 0 and row0 % seq == 0
    steps = seq // C_SUPER
    qb0 = row0 // C_SUPER
    sb0 = row0 // seq
    n_tiles = C_SUPER // C_TQ
    kern = functools.partial(_dilated_kernel, seq=seq, unroll=unroll)
    tab_specs = [pl.BlockSpec((1,) + tuple(tb.shape[1:]), lambda b, h, t: (h, 0, 0, 0)) for tb in tabs]
    return pl.pallas_call(
        kern,
        grid=(batch, heads, steps),
        in_specs=[
            pl.BlockSpec((C_SUPER, LANES), lambda b, h, t: (qb0 + b * steps + t, h)),
            pl.BlockSpec((seq, LANES), lambda b, h, t: (sb0 + b, heads + h)),
            pl.BlockSpec((seq, LANES), lambda b, h, t: (sb0 + b, 2 * heads + h)),
            pl.BlockSpec((1, LANES), lambda b, h, t: (0, 0)),
            pl.BlockSpec((1, LANES), lambda b, h, t: (0, 0)),
        ] + tab_specs,
        out_specs=pl.BlockSpec((C_SUPER, LANES), lambda b, h, t: (b * steps + t, h)),
        out_shape=jax.ShapeDtypeStruct((batch * seq, heads * C_DH), BF16),
        scratch_shapes=[
            pltpu.VMEM((seq, LANES), F32),
            pltpu.VMEM((C_SUPER, LANES), F32),
            pltpu.VMEM((len(C_PATTERNS), C_SUPER, LANES), F32),
            pltpu.VMEM((len(C_PATTERNS), C_SUPER, LANES), F32),
            pltpu.VMEM((n_tiles, C_TQ, C_TK), F32),
            pltpu.VMEM((n_tiles, C_TQ, LANES), F32),
            pltpu.VMEM((n_tiles, C_TQ, C_TK), BF16),
        ],
        compiler_params=_params("parallel", "parallel", "arbitrary"),
        name="dilated",
    )(qkv, qkv, qkv, gq.reshape(1, C_DH), gk.reshape(1, C_DH), *tabs)


def _norm_router_kernel(x_ref, g_ref, wr_ref, br_ref, h_ref, logit_ref):
    x = x_ref[...]
    h = x * lax.rsqrt(jnp.mean(x * x, axis=-1, keepdims=True) + EPS) * g_ref[...]
    h_ref[...] = h
    logit_ref[...] = jnp.dot(h, wr_ref[...], preferred_element_type=F32,
                             precision=lax.Precision.HIGHEST) + br_ref[...]


def norm_router(x, g, wr, br, tm=256):
    n, d = x.shape
    return pl.pallas_call(
        _norm_router_kernel,
        grid=(n // tm,),
        in_specs=[
            pl.BlockSpec((tm, d), lambda i: (i, 0)),
            pl.BlockSpec((1, d), lambda i: (0, 0)),
            pl.BlockSpec((d, ROUTER_LANES), lambda i: (0, 0)),
            pl.BlockSpec((1, ROUTER_LANES), lambda i: (0, 0)),
        ],
        out_specs=[pl.BlockSpec((tm, d), lambda i: (i, 0)), pl.BlockSpec((tm, ROUTER_LANES), lambda i: (i, 0))],
        out_shape=[jax.ShapeDtypeStruct((n, d), F32), jax.ShapeDtypeStruct((n, ROUTER_LANES), F32)],
        compiler_params=_params("parallel"),
        name="norm_router",
    )(x, g.reshape(1, d), wr, br)


def _row_gather_copy(src_hbm, dst_ref, sem, src_row, dst_row):
    return pltpu.make_async_copy(src_hbm.at[pl.ds(src_row, 1), :], dst_ref.at[pl.ds(dst_row, 1), :], sem)


def _gather_rows(idx_ref, base, src_hbm, dst_ref, sem, unroll=8):
    rows = dst_ref.shape[0]

    def issue(r, carry):
        _row_gather_copy(src_hbm, dst_ref, sem, idx_ref[base + r], r).start()
        return carry
    lax.fori_loop(0, rows, issue, 0, unroll=unroll)

    def drain(r, carry):
        _row_gather_copy(src_hbm, dst_ref, sem, 0, r).wait()
        return carry
    lax.fori_loop(0, rows, drain, 0, unroll=unroll)


def _gather_kernel(idx_ref, src_hbm, o_ref, sem):
    _gather_rows(idx_ref, pl.program_id(0) * o_ref.shape[0], src_hbm, o_ref, sem.at[0])


def gather_rows(src, idx, tm=MOE_TM):
    p = idx.shape[0]
    d = src.shape[1]
    return pl.pallas_call(
        _gather_kernel,
        grid_spec=pltpu.PrefetchScalarGridSpec(
            num_scalar_prefetch=1,
            grid=(p // tm,),
            in_specs=[pl.BlockSpec(memory_space=pl.ANY)],
            out_specs=pl.BlockSpec((tm, d), lambda i, idx: (i, 0)),
            scratch_shapes=[pltpu.SemaphoreType.DMA((1,))],
        ),
        out_shape=jax.ShapeDtypeStruct((p, d), src.dtype),
        compiler_params=_params("arbitrary"),
        name="moe_gather",
    )(idx, src)


def _expert_weight_copies(e, w_hbm, stage, sem):
    return [pltpu.make_async_copy(w.at[e], s, sem.at[k]) for k, (w, s) in enumerate(zip(w_hbm, stage))]


def _ffn_kernel(be_ref, first_ref, next_ref, nused_ref, x_ref, gate_ref, wg_hbm, wu_hbm, wd_hbm, o_ref,
                sg_ref, su_ref, sd_ref, wg_ref, wu_ref, wd_ref, sem, *, k_chunk):
    i = pl.program_id(0)
    w_hbm = (wg_hbm, wu_hbm, wd_hbm)
    stage = (sg_ref, su_ref, sd_ref)
    resident = (wg_ref, wu_ref, wd_ref)

    @pl.when(i == 0)
    def _():
        for cp in _expert_weight_copies(be_ref[0], w_hbm, stage, sem):
            cp.start()

    @pl.when(first_ref[i] == 1)
    def _():
        for cp in _expert_weight_copies(be_ref[i], w_hbm, stage, sem):
            cp.wait()
        for s_ref, w_ref in zip(stage, resident):
            rows = math.gcd(k_chunk, s_ref.shape[0])

            def cast(c, carry, s_ref=s_ref, w_ref=w_ref, rows=rows):
                sl = pl.ds(pl.multiple_of(c * rows, rows), rows)
                w_ref[sl, :] = s_ref[sl, :].astype(BF16)
                return carry
            lax.fori_loop(0, s_ref.shape[0] // rows, cast, 0)

        @pl.when(next_ref[i] >= 0)
        def _():
            for cp in _expert_weight_copies(next_ref[i], w_hbm, stage, sem):
                cp.start()

    live = i < nused_ref[0]

    @pl.when(live)
    def _():
        x = x_ref[...].astype(BF16)
        g = jnp.dot(x, wg_ref[...], preferred_element_type=F32)
        u = jnp.dot(x, wu_ref[...], preferred_element_type=F32)
        h = (jax.nn.silu(g) * u).astype(BF16)
        y = jnp.dot(h, wd_ref[...], preferred_element_type=F32)
        o_ref[...] = y * gate_ref[...]

    @pl.when(jnp.logical_not(live))
    def _():
        o_ref[...] = jnp.zeros_like(o_ref)


def expert_ffn(xb, gate_slot, block_e, first, next_e, n_used, wg, wu, wd, tm=MOE_TM):
    p, d = xb.shape
    de = wg.shape[2]
    any_spec = pl.BlockSpec(memory_space=pl.ANY)
    return pl.pallas_call(
        functools.partial(_ffn_kernel, k_chunk=256),
        grid_spec=pltpu.PrefetchScalarGridSpec(
            num_scalar_prefetch=4,
            grid=(p // tm,),
            in_specs=[
                pl.BlockSpec((tm, d), lambda i, *_: (i, 0)),
                pl.BlockSpec((tm, 1), lambda i, *_: (i, 0)),
                any_spec, any_spec, any_spec,
            ],
            out_specs=pl.BlockSpec((tm, d), lambda i, *_: (i, 0)),
            scratch_shapes=[
                pltpu.VMEM((d, de), F32), pltpu.VMEM((d, de), F32), pltpu.VMEM((de, d), F32),
                pltpu.VMEM((d, de), BF16), pltpu.VMEM((d, de), BF16), pltpu.VMEM((de, d), BF16),
                pltpu.SemaphoreType.DMA((3,)),
            ],
        ),
        out_shape=jax.ShapeDtypeStruct((p, d), F32),
        compiler_params=_params("arbitrary"),
        name="moe_ffn",
    )(block_e, first, next_e, n_used, xb, gate_slot, wg, wu, wd)


def _combine_kernel(d0_ref, d1_ref, x_ref, yb_hbm, o_ref, a_ref, b_ref, sem, *, row0):
    base = row0 + pl.program_id(0) * x_ref.shape[0]
    _gather_rows(d0_ref, base, yb_hbm, a_ref, sem.at[0])
    _gather_rows(d1_ref, base, yb_hbm, b_ref, sem.at[1])
    o_ref[...] = x_ref[...] + (a_ref[...] + b_ref[...])


def moe_combine(x, yb, d0, d1, row0, rows, tm=256):
    d = x.shape[1]
    assert row0 % tm == 0
    return pl.pallas_call(
        functools.partial(_combine_kernel, row0=row0),
        grid_spec=pltpu.PrefetchScalarGridSpec(
            num_scalar_prefetch=2,
            grid=(rows // tm,),
            in_specs=[pl.BlockSpec((tm, d), lambda i, a, b: (row0 // tm + i, 0)), pl.BlockSpec(memory_space=pl.ANY)],
            out_specs=pl.BlockSpec((tm, d), lambda i, a, b: (i, 0)),
            scratch_shapes=[pltpu.VMEM((tm, d), F32), pltpu.VMEM((tm, d), F32), pltpu.SemaphoreType.DMA((2,))],
        ),
        out_shape=jax.ShapeDtypeStruct((rows, d), F32),
        compiler_params=_params("arbitrary"),
        name="moe_combine",
    )(d0, d1, x, yb)


def moe_routing(logits, tm=MOE_TM):
    n = logits.shape[0]
    m = n * MOE_TOPK
    lg = logits[:, :MOE_GROUPS]
    le = logits[:, MOE_GROUPS:MOE_GROUPS + MOE_EXPERTS].reshape(n, MOE_GROUPS, MOE_PER_GROUP)
    g_sel = jnp.argmax(lg, axis=-1)
    p_g = jnp.take_along_axis(jax.nn.softmax(lg, axis=-1), g_sel[:, None], axis=-1)
    le_sel = jnp.take_along_axis(le, g_sel[:, None, None], axis=1)[:, 0]
    top_v, top_i = lax.top_k(le_sel, MOE_TOPK)
    gates = p_g * jax.nn.softmax(top_v, axis=-1)
    expert = (g_sel[:, None] * MOE_PER_GROUP + top_i).astype(jnp.int32)

    flat_e = expert.reshape(m)
    onehot = (flat_e[:, None] == jnp.arange(MOE_EXPERTS, dtype=jnp.int32)[None, :]).astype(jnp.int32)
    csum = jnp.cumsum(onehot, axis=0)
    rank = jnp.sum(csum * onehot, axis=1) - 1
    counts = csum[-1]
    padded = (counts + tm - 1) // tm * tm
    pad_end = jnp.cumsum(padded)
    pad_start = pad_end - padded
    dest = (pad_start[flat_e] + rank).astype(jnp.int32)
    n_blocks = (m + MOE_EXPERTS * (tm - 1) + tm - 1) // tm
    p = n_blocks * tm
    flat_tok = jnp.arange(m, dtype=jnp.int32) // MOE_TOPK
    slot_tok = jnp.zeros((p,), jnp.int32).at[dest].set(flat_tok)
    gate_slot = jnp.zeros((p,), F32).at[dest].set(gates.reshape(m))

    blk = jnp.arange(n_blocks, dtype=jnp.int32)
    n_used = (pad_end[-1] // tm).astype(jnp.int32)
    block_e = jnp.minimum(jnp.searchsorted(pad_end, blk * tm, side='right'), MOE_EXPERTS - 1).astype(jnp.int32)
    live = blk < n_used
    first = (live & ((blk == 0) | (block_e != jnp.roll(block_e, 1)))).astype(jnp.int32)
    has_rows = counts > 0
    later = jnp.arange(MOE_EXPERTS)[None, :] > jnp.arange(MOE_EXPERTS)[:, None]
    nxt_of_e = jnp.min(jnp.where(later & has_rows[None, :], jnp.arange(MOE_EXPERTS)[None, :], MOE_EXPERTS), axis=1)
    nxt_of_e = jnp.where(nxt_of_e == MOE_EXPERTS, -1, nxt_of_e).astype(jnp.int32)
    next_e = nxt_of_e[block_e]
    dest2 = dest.reshape(n, MOE_TOPK)
    return slot_tok, gate_slot.reshape(p, 1), block_e, first, next_e, n_used.reshape(1), dest2[:, 0], dest2[:, 1]


def hier_moe_residual(x, g, wr_g, br_g, wr_e, br_e, w_gate, w_up, w_down, out_segments=None):
    n, d = x.shape
    pad = ROUTER_LANES - MOE_GROUPS - MOE_EXPERTS
    wr = jnp.concatenate([wr_g, wr_e, jnp.zeros((d, pad), F32)], axis=1)
    br = jnp.concatenate([br_g, br_e, jnp.zeros((pad,), F32)]).reshape(1, ROUTER_LANES)
    h, logits = norm_router(x, g, wr, br)
    slot_tok, gate_slot, block_e, first, next_e, n_used, d0, d1 = moe_routing(logits)
    xb = gather_rows(h, slot_tok)
    yb = expert_ffn(xb, gate_slot, block_e, first, next_e, n_used, w_gate, w_up, w_down)
    if out_segments is None:
        return moe_combine(x, yb, d0, d1, 0, n)
    return [moe_combine(x, yb, d0, d1, row0, rows) for row0, rows in out_segments]


def kernel(x_prompt, x_sample, norm_mix, norm_ffn, ev_w_in, ev_w_out, a_ln_g, a_ln_b, a_w_s, a_b_s, b_q_gain, b_k_gain, b_rpb, od_w_in, od_w_out, c_q_gain, c_k_gain, t5_table, moe_wr_g, moe_br_g, moe_wr_e, moe_br_e, moe_w_gate, moe_w_up, moe_w_down):
    d = x_prompt.shape[-1]
    segs = [(x_prompt.shape[0], x_prompt.shape[1]), (x_sample.shape[0], x_sample.shape[1])]
    xs = [x_prompt.reshape(-1, d), x_sample.reshape(-1, d)]
    depth = norm_mix.shape[0]
    for l in range(depth):
        i = l // 2
        h = rmsnorm(xs, norm_mix[l], BF16)
        if l % 2 == 0:
            aw = a_ln_g.shape[1]
            z = matmul([[h]], [(ev_w_in[i], 0)])
            a_out = mixer_a(z, a_ln_g[i], a_ln_b[i], a_w_s[i], a_b_s[i])
            tab = natten_bias_table(b_rpb[i])
            b_out, row0 = [], 0
            for batch, seq in segs:
                b_out.append(natten(z, 2 * aw // LANES, row0, batch, seq, b_q_gain[i], b_k_gain[i], tab))
                row0 += batch * seq
            x = matmul([[a_out], b_out], [(ev_w_out[i], 0), (ev_w_out[i], aw)], residuals=xs)
        else:
            qkv = matmul([[h]], [(od_w_in[i], 0)])
            tabs = [dilated_bias_table(t5_table, dil) for _, dil in C_PATTERNS]
            c_out, row0 = [], 0
            for batch, seq in segs:
                c_out.append(dilated_attention(qkv, row0, batch, seq, c_q_gain[i], c_k_gain[i], tabs))
                row0 += batch * seq
            x = matmul([c_out], [(od_w_out[i], 0)], residuals=xs)
        out_segments = None
        if l == depth - 1:
            out_segments = [(0, segs[0][0] * segs[0][1]), (segs[0][0] * segs[0][1], segs[1][0] * segs[1][1])]
        res = hier_moe_residual(x, norm_ffn[l], moe_wr_g[l], moe_br_g[l], moe_wr_e[l], moe_br_e[l],
                                moe_w_gate[l], moe_w_up[l], moe_w_down[l], out_segments)
        xs = res if out_segments is not None else [res]
    return (xs[0].reshape(x_prompt.shape), xs[1].reshape(x_sample.shape))
```

```python
import functools
import math

import jax
import jax.numpy as jnp
from jax import lax
from jax.experimental import pallas as pl
from jax.experimental.pallas import tpu as pltpu

F32 = jnp.float32
BF16 = jnp.bfloat16
EPS = 1e-6
NEG = -1e30

V7X_VMEM_LIMIT_BYTES = 56 * 1024 * 1024
LANES = 128

GRID_W = 64
CHUNK = 128
A_GROUPS = 8
NA_KH = 8
NA_KW = 16
B_DH = 64
C_DH = 128
C_PATTERNS = ((128, 1), (512, 4), (2048, 16))
C_RAD = 64
C_TQ = 128
C_TK = C_TQ + 2 * C_RAD
C_SUPER = C_TQ * 16
T5_BUCKETS = 32
T5_MAX_DIST = 1024
MOE_GROUPS = 4
MOE_PER_GROUP = 8
MOE_EXPERTS = MOE_GROUPS * MOE_PER_GROUP
MOE_TOPK = 2
MOE_TM = 256
ROUTER_LANES = 128


def _params(*sem):
    return pltpu.CompilerParams(dimension_semantics=sem, vmem_limit_bytes=V7X_VMEM_LIMIT_BYTES)


def _row_sources(arrays, tm, width, col_of):
    specs, spans, off = [], [], 0
    for a in arrays:
        nblk = a.shape[0] // tm
        specs.append(pl.BlockSpec(
            (tm, width), lambda *g, off=off, nblk=nblk: (jnp.clip(g[-1] - off, 0, nblk - 1), col_of(*g))))
        spans.append((off, nblk))
        off += nblk
    return specs, spans


def _active_rows(i, refs, spans):
    val = refs[-1][...]
    for ref, (off, nblk) in reversed(list(zip(refs[:-1], spans[:-1]))):
        val = jnp.where(i < off + nblk, ref[...], val)
    return val


def _rmsnorm_kernel(*refs, spans):
    x_refs, (g_ref, o_ref) = refs[:len(spans)], refs[len(spans):]
    x = _active_rows(pl.program_id(0), x_refs, spans)
    y = x * lax.rsqrt(jnp.mean(x * x, axis=-1, keepdims=True) + EPS)
    o_ref[...] = (y * g_ref[...]).astype(o_ref.dtype)


def rmsnorm(xs, g, out_dtype, tm=512):
    d = xs[0].shape[1]
    n = sum(x.shape[0] for x in xs)
    specs, spans = _row_sources(xs, tm, d, lambda i: 0)
    return pl.pallas_call(
        functools.partial(_rmsnorm_kernel, spans=spans),
        grid=(n // tm,),
        in_specs=specs + [pl.BlockSpec((1, d), lambda i: (0, 0))],
        out_specs=pl.BlockSpec((tm, d), lambda i: (i, 0)),
        out_shape=jax.ShapeDtypeStruct((n, d), out_dtype),
        compiler_params=_params("arbitrary"),
        name="rmsnorm",
    )(*xs, g.reshape(1, d))


def _matmul_kernel(*refs, x_spans, res_spans, k_chunk):
    refs = list(refs)
    x_refs = [[refs.pop(0) for _ in spans] for spans in x_spans]
    w_refs = [refs.pop(0) for _ in x_spans]
    res_refs = [refs.pop(0) for _ in res_spans]
    o_ref, wb_refs = refs[0], refs[1:]
    i = pl.program_id(1)

    @pl.when(pl.program_id(1) == 0)
    def _():
        for w_ref, wb_ref in zip(w_refs, wb_refs):
            rows = math.gcd(k_chunk, w_ref.shape[0])

            def cast(c, carry, w_ref=w_ref, wb_ref=wb_ref, rows=rows):
                sl = pl.ds(pl.multiple_of(c * rows, rows), rows)
                wb_ref[sl, :] = w_ref[sl, :].astype(BF16)
                return carry
            lax.fori_loop(0, w_ref.shape[0] // rows, cast, 0)

    acc = None
    for pieces, spans, wb_ref in zip(x_refs, x_spans, wb_refs):
        part = jnp.dot(_active_rows(i, pieces, spans), wb_ref[...], preferred_element_type=F32)
        acc = part if acc is None else acc + part
    if res_spans:
        acc = _active_rows(i, res_refs, res_spans) + acc
    o_ref[...] = acc.astype(o_ref.dtype)


def matmul(xs, ws, residuals=(), out_dtype=F32, tm=512, tn=1024):
    n = sum(x.shape[0] for x in xs[0])
    m = ws[0][0].shape[1]
    tn = min(tn, m)
    in_specs, x_spans, w_specs = [], [], []
    for pieces, (w, row) in zip(xs, ws):
        k = pieces[0].shape[1]
        assert row % k == 0
        specs, spans = _row_sources(pieces, tm, k, lambda j, i: 0)
        in_specs += specs
        x_spans.append(spans)
        w_specs.append(pl.BlockSpec((k, tn), lambda j, i, rb=row // k: (rb, j)))
    res_specs, res_spans = _row_sources(list(residuals), tm, tn, lambda j, i: j)
    kern = functools.partial(_matmul_kernel, x_spans=x_spans, res_spans=res_spans, k_chunk=256)
    return pl.pallas_call(
        kern,
        grid=(m // tn, n // tm),
        in_specs=in_specs + w_specs + res_specs,
        out_specs=pl.BlockSpec((tm, tn), lambda j, i: (i, j)),
        out_shape=jax.ShapeDtypeStruct((n, m), out_dtype),
        scratch_shapes=[pltpu.VMEM((pieces[0].shape[1], tn), BF16) for pieces in xs],
        compiler_params=_params("arbitrary", "arbitrary"),
        name="matmul",
    )(*[p for pieces in xs for p in pieces], *[w for w, _ in ws], *residuals)


def _mixer_a_kernel(u_ref, v_ref, lng_ref, lnb_ref, ws_ref, bs_ref, o_ref):
    tm = u_ref.shape[0]
    gd = u_ref.shape[1] // A_GROUPS
    for c in range(tm // CHUNK):
        rows = slice(c * CHUNK, (c + 1) * CHUNK)
        u = jax.nn.gelu(u_ref[rows, :])
        v = jax.nn.gelu(v_ref[rows, :])
        mu = jnp.mean(v, axis=-1, keepdims=True)
        vc = v - mu
        var = jnp.mean(vc * vc, axis=-1, keepdims=True)
        v = vc * lax.rsqrt(var + EPS) * lng_ref[...] + lnb_ref[...]
        for g in range(A_GROUPS):
            cols = slice(g * gd, (g + 1) * gd)
            f = jnp.dot(ws_ref[g].astype(BF16), v[:, cols].astype(BF16), preferred_element_type=F32)
            f = f + bs_ref[g]
            o_ref[rows, cols] = (u[:, cols] * f).astype(o_ref.dtype)


def mixer_a(z, ln_g, ln_b, w_s, b_s, tm=512):
    n = z.shape[0]
    aw = ln_g.shape[0]
    gd = aw // A_GROUPS
    bs_b = jnp.broadcast_to(b_s[:, :, None], (A_GROUPS, CHUNK, gd))
    return pl.pallas_call(
        _mixer_a_kernel,
        grid=(n // tm,),
        in_specs=[
            pl.BlockSpec((tm, aw), lambda i: (i, 0)),
            pl.BlockSpec((tm, aw), lambda i: (i, 1)),
            pl.BlockSpec((1, aw), lambda i: (0, 0)),
            pl.BlockSpec((1, aw), lambda i: (0, 0)),
            pl.BlockSpec((A_GROUPS, CHUNK, CHUNK), lambda i: (0, 0, 0)),
            pl.BlockSpec((A_GROUPS, CHUNK, gd), lambda i: (0, 0, 0)),
        ],
        out_specs=pl.BlockSpec((tm, aw), lambda i: (i, 0)),
        out_shape=jax.ShapeDtypeStruct((n, aw), BF16),
        compiler_params=_params("parallel"),
        name="mixer_a",
    )(z, z, ln_g.reshape(1, aw), ln_b.reshape(1, aw), w_s, bs_b)


def _skew(w, rows):
    n = w.shape[-1]
    flat = jnp.tile(w, (1,) * (w.ndim - 1) + (rows,))
    return flat[..., :rows * (n - 1)].reshape(w.shape[:-1] + (rows, n - 1))


def _pair_rmsnorm(x, gain):
    sq = x * x
    lane = lax.broadcasted_iota(jnp.int32, x.shape, 1)
    lo = lane < B_DH
    s_lo = jnp.sum(jnp.where(lo, sq, 0.0), axis=-1, keepdims=True)
    s_hi = jnp.sum(jnp.where(lo, 0.0, sq), axis=-1, keepdims=True)
    ms = jnp.where(lo, s_lo, s_hi) * (1.0 / B_DH)
    return x * lax.rsqrt(ms + EPS) * gain


def _natten_kernel(q_ref, k_ref, v_ref, gq_ref, gk_ref, bias_ref, o_ref, kn_ref, vb_ref, s_ref, m_ref, e_ref,
                   *, rows_per_step, rows):
    seq = k_ref.shape[0]
    t = pl.program_id(2)
    prep = 256

    @pl.when(t == 0)
    def _():
        def body(c, carry):
            sl = pl.ds(pl.multiple_of(c * prep, prep), prep)
            kn_ref[sl, :] = _pair_rmsnorm(k_ref[sl, :], gk_ref[...]).astype(BF16)
            vb_ref[sl, :] = v_ref[sl, :].astype(BF16)
            return carry
        lax.fori_loop(0, seq // prep, body, 0)

    q = (_pair_rmsnorm(q_ref[...], gq_ref[...]) * (B_DH ** -0.5)).astype(BF16)
    lo = lax.broadcasted_iota(jnp.int32, (GRID_W, LANES), 1) < B_DH
    zero = jnp.zeros((GRID_W, LANES), BF16)
    win = NA_KH * GRID_W
    ones = jnp.ones((win, LANES), BF16)

    def window(rr):
        r = t * rows_per_step + rr
        rs = jnp.clip(r - NA_KH // 2, 0, rows - NA_KH)
        return r - rs, pl.ds(pl.multiple_of(rs * GRID_W, GRID_W), win)

    for rr in range(rows_per_step):
        case, ksl = window(rr)
        qr = q[rr * GRID_W:(rr + 1) * GRID_W, :]
        q2 = jnp.concatenate([jnp.where(lo, qr, zero), jnp.where(lo, zero, qr)], axis=0)
        s = lax.dot_general(q2, kn_ref[ksl, :], (((1,), (1,)), ((), ())), preferred_element_type=F32)
        s_ref[rr] = s + bias_ref[0, case]
    for rr in range(rows_per_step):
        m = jnp.max(s_ref[rr], axis=-1, keepdims=True)
        m_ref[rr] = jnp.broadcast_to(m, (2 * GRID_W, LANES))
    for rr in range(rows_per_step):
        mb = m_ref[rr]
        for c in range(win // LANES):
            cols = slice(c * LANES, (c + 1) * LANES)
            e_ref[rr, :, cols] = jnp.exp(s_ref[rr, :, cols] - mb).astype(BF16)
    for rr in range(rows_per_step):
        _, ksl = window(rr)
        e = e_ref[rr]
        o = jnp.dot(e, vb_ref[ksl, :], preferred_element_type=F32)
        l = jnp.dot(e, ones, preferred_element_type=F32)
        o = o / l
        o_ref[rr * GRID_W:(rr + 1) * GRID_W, :] = jnp.where(lo, o[:GRID_W], o[GRID_W:]).astype(o_ref.dtype)


def natten_bias_table(rpb):
    heads = rpb.shape[0]
    col = jnp.arange(GRID_W)
    col_start = jnp.clip(col - NA_KW // 2, 0, GRID_W - NA_KW)
    col_valid = (col[None, :] >= col_start[:, None]) & (col[None, :] < col_start[:, None] + NA_KW)
    edge = GRID_W - NA_KW
    w = jnp.concatenate([jnp.repeat(rpb[..., :1], edge, axis=-1), rpb.astype(F32),
                         jnp.repeat(rpb[..., -1:], edge, axis=-1), jnp.zeros(rpb.shape[:-1] + (1,), F32)], axis=-1)
    bias_c = _skew(w, GRID_W)[..., GRID_W - 1:]
    tab = jnp.stack([bias_c[:, NA_KH - 1 - c:2 * NA_KH - 1 - c] for c in range(NA_KH)], axis=1)
    tab = jnp.where(col_valid[:, None, :], tab.transpose(0, 1, 3, 2, 4), NEG)
    tab = tab.reshape(heads // 2, 2, NA_KH, GRID_W, NA_KH * GRID_W).transpose(0, 2, 1, 3, 4)
    return tab.reshape(heads // 2, NA_KH, 2 * GRID_W, NA_KH * GRID_W)


def natten(z, col0, row0, batch, seq, gq, gk, bias_tab, rows_per_step=8):
    hp = bias_tab.shape[0]
    rows = seq // GRID_W
    assert rows >= NA_KH and rows % rows_per_step == 0 and row0 % seq == 0
    tq = rows_per_step * GRID_W
    steps = rows // rows_per_step
    qb0 = row0 // tq
    sb0 = row0 // seq
    win = NA_KH * GRID_W
    kern = functools.partial(_natten_kernel, rows_per_step=rows_per_step, rows=rows)
    gq2 = jnp.tile(gq, 2).reshape(1, 2 * B_DH)
    gk2 = jnp.tile(gk, 2).reshape(1, 2 * B_DH)
    return pl.pallas_call(
        kern,
        grid=(batch, hp, steps),
        in_specs=[
            pl.BlockSpec((tq, LANES), lambda b, h, t: (qb0 + b * steps + t, col0 + h)),
            pl.BlockSpec((seq, LANES), lambda b, h, t: (sb0 + b, col0 + hp + h)),
            pl.BlockSpec((seq, LANES), lambda b, h, t: (sb0 + b, col0 + 2 * hp + h)),
            pl.BlockSpec((1, LANES), lambda b, h, t: (0, 0)),
            pl.BlockSpec((1, LANES), lambda b, h, t: (0, 0)),
            pl.BlockSpec((1, NA_KH, 2 * GRID_W, win), lambda b, h, t: (h, 0, 0, 0)),
        ],
        out_specs=pl.BlockSpec((tq, LANES), lambda b, h, t: (b * steps + t, h)),
        out_shape=jax.ShapeDtypeStruct((batch * seq, 2 * hp * B_DH), BF16),
        scratch_shapes=[
            pltpu.VMEM((seq, LANES), BF16),
            pltpu.VMEM((seq, LANES), BF16),
            pltpu.VMEM((rows_per_step, 2 * GRID_W, win), F32),
            pltpu.VMEM((rows_per_step, 2 * GRID_W, LANES), F32),
            pltpu.VMEM((rows_per_step, 2 * GRID_W, win), BF16),
        ],
        compiler_params=_params("parallel", "parallel", "arbitrary"),
        name="natten",
    )(z, z, z, gq2, gk2, bias_tab)


def t5_bucket(rel):
    nb = T5_BUCKETS // 2
    max_exact = nb // 2
    ret = jnp.where(rel > 0, nb, 0)
    n = jnp.abs(rel)
    large = max_exact + (jnp.log(jnp.maximum(n, 1).astype(F32) / max_exact)
                         / math.log(T5_MAX_DIST / max_exact) * (nb - max_exact)).astype(jnp.int32)
    large = jnp.minimum(large, nb - 1)
    return ret + jnp.where(n < max_exact, n, large)


def dilated_bias_table(t5_table, dil):
    span = C_TK - 1
    rel = jnp.arange(-span, span + 1)
    vec = t5_table.astype(F32)[t5_bucket(rel * dil)]
    vec = jnp.where((jnp.abs(rel) <= C_RAD)[:, None], vec, NEG).T
    w = jnp.concatenate([vec, jnp.zeros((vec.shape[0], 1), F32)], axis=1)
    r = _skew(w, C_TQ)
    return jnp.stack([r[:, :, span - off:span - off + C_TK] for off in (0, C_RAD, 2 * C_RAD)], axis=1)


def _row_rmsnorm(x, gain):
    return x * lax.rsqrt(jnp.mean(x * x, axis=-1, keepdims=True) + EPS) * gain


def _dilated_kernel(q_ref, k_ref, v_ref, gq_ref, gk_ref, b0_ref, b1_ref, b2_ref, o_ref,
                    kn_ref, qn_ref, oacc_ref, lse_ref, s_ref, m_ref, e_ref, *, seq, unroll):
    t = pl.program_id(2)
    prep = 256

    @pl.when(t == 0)
    def _():
        def body(c, carry):
            sl = pl.ds(pl.multiple_of(c * prep, prep), prep)
            kn_ref[sl, :] = _row_rmsnorm(k_ref[sl, :], gk_ref[...])
            return carry
        lax.fori_loop(0, seq // prep, body, 0)

    qn_ref[...] = _row_rmsnorm(q_ref[...], gq_ref[...]) * (C_DH ** -0.5)
    n_tiles = C_SUPER // C_TQ

    for p, ((window, dil), b_ref) in enumerate(zip(C_PATTERNS, (b0_ref, b1_ref, b2_ref))):
        n_sub = seq // dil
        tk = min(C_TK, n_sub)
        span = C_TQ * dil
        ones = jnp.ones((tk, LANES), BF16)

        def tile(idx, dil=dil, n_sub=n_sub, tk=tk, span=span):
            u = idx // dil
            s = idx % dil
            qpos = u * span + s
            j0 = t * (C_SUPER // dil) + u * C_TQ
            ws = jnp.clip(j0 - C_RAD, 0, n_sub - tk)
            kpos = ws * dil + s
            if dil == 1:
                return (pl.ds(pl.multiple_of(qpos, C_TQ), C_TQ), pl.ds(pl.multiple_of(kpos, C_RAD), tk),
                        (j0 - ws) // C_RAD)
            return pl.ds(qpos, C_TQ, stride=dil), pl.ds(kpos, tk, stride=dil), (j0 - ws) // C_RAD

        def scores(idx, carry, tile=tile, tk=tk, b_ref=b_ref):
            qsl, ksl, case = tile(idx)
            q = qn_ref[qsl, :].astype(BF16)
            k = kn_ref[ksl, :].astype(BF16)
            sc = lax.dot_general(q, k, (((1,), (1,)), ((), ())), preferred_element_type=F32)
            s_ref[idx, :, :tk] = sc + b_ref[0, case, :, :tk]
            return carry

        def row_max(idx, carry, tk=tk):
            m = jnp.max(s_ref[idx, :, :tk], axis=-1, keepdims=True)
            m_ref[idx] = jnp.broadcast_to(m, (C_TQ, LANES))
            return carry

        def exps(idx, carry, tk=tk):
            mb = m_ref[idx]
            for c in range(tk // LANES):
                cols = slice(c * LANES, (c + 1) * LANES)
                e_ref[idx, :, cols] = jnp.exp(s_ref[idx, :, cols] - mb).astype(BF16)
            return carry

        def outputs(idx, carry, tile=tile, tk=tk, ones=ones, p=p):
            qsl, ksl, _ = tile(idx)
            e = e_ref[idx, :, :tk]
            o = jnp.dot(e, v_ref[ksl, :].astype(BF16), preferred_element_type=F32)
            l = jnp.dot(e, ones, preferred_element_type=F32)
            oacc_ref[p, qsl, :] = o / l
            lse_ref[p, qsl, :] = m_ref[idx] + jnp.log(l)
            return carry

        for phase in (scores, row_max, exps, outputs):
            lax.fori_loop(0, n_tiles, phase, 0, unroll=unroll)

    l0, l1, l2 = lse_ref[0], lse_ref[1], lse_ref[2]
    mx = jnp.maximum(jnp.maximum(l0, l1), l2)
    w0, w1, w2 = jnp.exp(l0 - mx), jnp.exp(l1 - mx), jnp.exp(l2 - mx)
    num = w0 * oacc_ref[0] + w1 * oacc_ref[1] + w2 * oacc_ref[2]
    o_ref[...] = (num / (w0 + w1 + w2)).astype(o_ref.dtype)


def dilated_attention(qkv, row0, batch, seq, gq, gk, tabs, unroll=8):
    heads = tabs[0].shape[0]
    assert seq % C_SUPER == 0 and row0 % seq == 0
    steps = seq // C_SUPER
    qb0 = row0 // C_SUPER
    sb0 = row0 // seq
    n_tiles = C_SUPER // C_TQ
    kern = functools.partial(_dilated_kernel, seq=seq, unroll=unroll)
    tab_specs = [pl.BlockSpec((1,) + tuple(tb.shape[1:]), lambda b, h, t: (h, 0, 0, 0)) for tb in tabs]
    return pl.pallas_call(
        kern,
        grid=(batch, heads, steps),
        in_specs=[
            pl.BlockSpec((C_SUPER, LANES), lambda b, h, t: (qb0 + b * steps + t, h)),
            pl.BlockSpec((seq, LANES), lambda b, h, t: (sb0 + b, heads + h)),
            pl.BlockSpec((seq, LANES), lambda b, h, t: (sb0 + b, 2 * heads + h)),
            pl.BlockSpec((1, LANES), lambda b, h, t: (0, 0)),
            pl.BlockSpec((1, LANES), lambda b, h, t: (0, 0)),
        ] + tab_specs,
        out_specs=pl.BlockSpec((C_SUPER, LANES), lambda b, h, t: (b * steps + t, h)),
        out_shape=jax.ShapeDtypeStruct((batch * seq, heads * C_DH), BF16),
        scratch_shapes=[
            pltpu.VMEM((seq, LANES), F32),
            pltpu.VMEM((C_SUPER, LANES), F32),
            pltpu.VMEM((len(C_PATTERNS), C_SUPER, LANES), F32),
            pltpu.VMEM((len(C_PATTERNS), C_SUPER, LANES), F32),
            pltpu.VMEM((n_tiles, C_TQ, C_TK), F32),
            pltpu.VMEM((n_tiles, C_TQ, LANES), F32),
            pltpu.VMEM((n_tiles, C_TQ, C_TK), BF16),
        ],
        compiler_params=_params("parallel", "parallel", "arbitrary"),
        name="dilated",
    )(qkv, qkv, qkv, gq.reshape(1, C_DH), gk.reshape(1, C_DH), *tabs)


def _pack_bf16_pairs(lo, hi):
    lo_bits = lax.bitcast_convert_type(lo.astype(BF16).astype(F32), jnp.uint32)
    hi_bits = lax.bitcast_convert_type(hi.astype(BF16).astype(F32), jnp.uint32)
    return (hi_bits & jnp.uint32(0xFFFF0000)) | (lo_bits >> 16)


def _unpack_bf16_pairs(packed):
    lo = lax.bitcast_convert_type(packed << 16, F32)
    hi = lax.bitcast_convert_type(packed & jnp.uint32(0xFFFF0000), F32)
    return lo, hi


META_E0, META_E1, META_R0, META_R1, META_G0, META_G1 = range(6)


def _route_kernel(x_ref, g_ref, wr_ref, br_ref, hp_ref, meta_ref, tot_ref):
    x = x_ref[...]
    h = x * lax.rsqrt(jnp.mean(x * x, axis=-1, keepdims=True) + EPS) * g_ref[...]
    half = h.shape[1] // 2
    hp_ref[...] = _pack_bf16_pairs(h[:, :half], h[:, half:])
    logits = jnp.dot(h, wr_ref[...], preferred_element_type=F32, precision=lax.Precision.HIGHEST) + br_ref[...]
    tm = x.shape[0]
    lane = lax.broadcasted_iota(jnp.int32, (tm, ROUTER_LANES), 1)
    lane_f = lane.astype(F32)

    def first_max(v):
        mx = jnp.max(v, axis=-1, keepdims=True)
        return mx, jnp.min(jnp.where(v == mx, lane_f, float(ROUTER_LANES)), axis=-1, keepdims=True)

    lg = jnp.where(lane < MOE_GROUPS, logits, -jnp.inf)
    gmax, gsel = first_max(lg)
    p_g = 1.0 / jnp.sum(jnp.exp(lg - gmax), axis=-1, keepdims=True)
    expert_lane = (lane >= MOE_GROUPS) & (lane < MOE_GROUPS + MOE_EXPERTS)
    in_group = expert_lane & (((lane - MOE_GROUPS) // MOE_PER_GROUP) == gsel.astype(jnp.int32))
    le = jnp.where(in_group, logits, -jnp.inf)
    v0, i0 = first_max(le)
    pick0 = lane_f == i0
    v1, i1 = first_max(jnp.where(pick0, -jnp.inf, le))
    pick1 = lane_f == i1
    t = jnp.exp(v1 - v0)
    g0 = p_g / (1.0 + t)
    g1 = p_g * t / (1.0 + t)

    onehot = jnp.where(pick0 | pick1, 1.0, 0.0).astype(BF16)
    row = lax.broadcasted_iota(jnp.int32, (tm, tm), 0)
    col = lax.broadcasted_iota(jnp.int32, (tm, tm), 1)
    earlier = jnp.where(col < row, 1.0, 0.0).astype(BF16)
    before = jnp.dot(earlier, onehot, preferred_element_type=F32)
    r0 = jnp.sum(jnp.where(pick0, before, 0.0), axis=-1, keepdims=True)
    r1 = jnp.sum(jnp.where(pick1, before, 0.0), axis=-1, keepdims=True)

    meta = jnp.zeros((tm, ROUTER_LANES), F32)
    for pos, val in ((META_E0, i0 - MOE_GROUPS), (META_E1, i1 - MOE_GROUPS), (META_R0, r0), (META_R1, r1),
                     (META_G0, g0), (META_G1, g1)):
        meta = jnp.where(lane == pos, val, meta)
    meta_ref[...] = meta
    tot_ref[0] = jnp.sum(onehot.astype(F32), axis=0, keepdims=True)


def moe_route(x, g, wr, br, tm):
    n, d = x.shape
    return pl.pallas_call(
        _route_kernel,
        grid=(n // tm,),
        in_specs=[
            pl.BlockSpec((tm, d), lambda i: (i, 0)),
            pl.BlockSpec((1, d), lambda i: (0, 0)),
            pl.BlockSpec((d, ROUTER_LANES), lambda i: (0, 0)),
            pl.BlockSpec((1, ROUTER_LANES), lambda i: (0, 0)),
        ],
        out_specs=[
            pl.BlockSpec((tm, d // 2), lambda i: (i, 0)),
            pl.BlockSpec((tm, ROUTER_LANES), lambda i: (i, 0)),
            pl.BlockSpec((1, 1, ROUTER_LANES), lambda i: (i, 0, 0)),
        ],
        out_shape=[
            jax.ShapeDtypeStruct((n, d // 2), jnp.uint32),
            jax.ShapeDtypeStruct((n, ROUTER_LANES), F32),
            jax.ShapeDtypeStruct((n // tm, 1, ROUTER_LANES), F32),
        ],
        compiler_params=_params("parallel"),
        name="moe_route",
    )(x, g.reshape(1, d), wr, br)


def _dispatch_kernel(d0_ref, d1_ref, pad_first_ref, pad_cnt_ref, nused_ref, h_ref, xb_hbm, sem):
    i = pl.program_id(0)
    tm = h_ref.shape[0]
    base = i * tm

    def row_copy(src_row, dst_row):
        return pltpu.make_async_copy(h_ref.at[pl.ds(src_row, 1), :], xb_hbm.at[pl.ds(dst_row, 1), :], sem.at[0])

    def issue(r, carry):
        row_copy(r, d0_ref[base + r]).start()
        row_copy(r, d1_ref[base + r]).start()
        return carry
    lax.fori_loop(0, tm, issue, 0, unroll=8)

    def drain(r, carry):
        row_copy(0, 0).wait()
        row_copy(0, 0).wait()
        return carry
    lax.fori_loop(0, tm, drain, 0, unroll=8)

    @pl.when(i == pl.num_programs(0) - 1)
    def _():
        def per_expert(e, carry):
            def fill(j, c):
                row_copy(0, pad_first_ref[e] + j).start()
                return c
            lax.fori_loop(0, pad_cnt_ref[e], fill, 0)

            def fill_wait(j, c):
                row_copy(0, 0).wait()
                return c
            lax.fori_loop(0, pad_cnt_ref[e], fill_wait, 0)
            return carry
        lax.fori_loop(0, MOE_EXPERTS, per_expert, 0)

        def block_copy(b):
            return pltpu.make_async_copy(h_ref, xb_hbm.at[pl.ds(pl.multiple_of(b * tm, tm), tm), :], sem.at[0])

        def fill_block(b, c):
            block_copy(b).start()
            return c
        lax.fori_loop(nused_ref[0], xb_hbm.shape[0] // tm, fill_block, 0)

        def fill_block_wait(b, c):
            block_copy(b).wait()
            return c
        lax.fori_loop(nused_ref[0], xb_hbm.shape[0] // tm, fill_block_wait, 0)


def moe_dispatch(hp, d0, d1, pad_first, pad_cnt, n_used, p, tm=MOE_TM):
    n, dw = hp.shape
    return pl.pallas_call(
        _dispatch_kernel,
        grid_spec=pltpu.PrefetchScalarGridSpec(
            num_scalar_prefetch=5,
            grid=(n // tm,),
            in_specs=[pl.BlockSpec((tm, dw), lambda i, *_: (i, 0))],
            out_specs=pl.BlockSpec(memory_space=pl.ANY),
            scratch_shapes=[pltpu.SemaphoreType.DMA((1,))],
        ),
        out_shape=jax.ShapeDtypeStruct((p, dw), hp.dtype),
        compiler_params=_params("arbitrary"),
        name="moe_dispatch",
    )(d0, d1, pad_first, pad_cnt, n_used, hp)


def _row_gather_copy(src_hbm, dst_ref, sem, src_row, dst_row):
    return pltpu.make_async_copy(src_hbm.at[pl.ds(src_row, 1), :], dst_ref.at[pl.ds(dst_row, 1), :], sem)


def _gather_rows(idx_ref, base, src_hbm, dst_ref, sem, unroll=8):
    rows = dst_ref.shape[0]

    def issue(r, carry):
        _row_gather_copy(src_hbm, dst_ref, sem, idx_ref[base + r], r).start()
        return carry
    lax.fori_loop(0, rows, issue, 0, unroll=unroll)

    def drain(r, carry):
        _row_gather_copy(src_hbm, dst_ref, sem, 0, r).wait()
        return carry
    lax.fori_loop(0, rows, drain, 0, unroll=unroll)


def _expert_weight_copies(layer, e, w_hbm, stage, sem):
    return [pltpu.make_async_copy(w.at[layer, e], s, sem.at[k]) for k, (w, s) in enumerate(zip(w_hbm, stage))]


def _ffn_kernel(be_ref, first_ref, next_ref, nused_ref, x_ref, wg_hbm, wu_hbm, wd_hbm, o_ref,
                sg_ref, su_ref, sd_ref, wg_ref, wu_ref, wd_ref, sem, *, layer, k_chunk):
    i = pl.program_id(0)
    w_hbm = (wg_hbm, wu_hbm, wd_hbm)
    stage = (sg_ref, su_ref, sd_ref)
    resident = (wg_ref, wu_ref, wd_ref)

    @pl.when(i == 0)
    def _():
        for cp in _expert_weight_copies(layer, be_ref[0], w_hbm, stage, sem):
            cp.start()

    @pl.when(first_ref[i] == 1)
    def _():
        for cp in _expert_weight_copies(layer, be_ref[i], w_hbm, stage, sem):
            cp.wait()
        for s_ref, w_ref in zip(stage, resident):
            rows = math.gcd(k_chunk, s_ref.shape[0])

            def cast(c, carry, s_ref=s_ref, w_ref=w_ref, rows=rows):
                sl = pl.ds(pl.multiple_of(c * rows, rows), rows)
                w_ref[sl, :] = s_ref[sl, :].astype(BF16)
                return carry
            lax.fori_loop(0, s_ref.shape[0] // rows, cast, 0)

        @pl.when(next_ref[i] >= 0)
        def _():
            for cp in _expert_weight_copies(layer, next_ref[i], w_hbm, stage, sem):
                cp.start()

    live = i < nused_ref[0]

    @pl.when(live)
    def _():
        lo, hi = _unpack_bf16_pairs(x_ref[...])
        lo, hi = lo.astype(BF16), hi.astype(BF16)
        half = lo.shape[1]
        g = (jnp.dot(lo, wg_ref[:half, :], preferred_element_type=F32)
             + jnp.dot(hi, wg_ref[half:, :], preferred_element_type=F32))
        u = (jnp.dot(lo, wu_ref[:half, :], preferred_element_type=F32)
             + jnp.dot(hi, wu_ref[half:, :], preferred_element_type=F32))
        h = (jax.nn.silu(g) * u).astype(BF16)
        y = jnp.dot(h, wd_ref[...], preferred_element_type=F32)
        o_ref[...] = _pack_bf16_pairs(y[:, :half], y[:, half:])

    @pl.when(jnp.logical_not(live))
    def _():
        o_ref[...] = jnp.zeros_like(o_ref)


def expert_ffn(xb, block_e, first, next_e, n_used, wg, wu, wd, layer, tm=MOE_TM):
    p, dw = xb.shape
    d, de = wg.shape[2], wg.shape[3]
    any_spec = pl.BlockSpec(memory_space=pl.ANY)
    return pl.pallas_call(
        functools.partial(_ffn_kernel, layer=layer, k_chunk=256),
        grid_spec=pltpu.PrefetchScalarGridSpec(
            num_scalar_prefetch=4,
            grid=(p // tm,),
            in_specs=[
                pl.BlockSpec((tm, dw), lambda i, be, fi, ne, nu: (jnp.minimum(i, nu[0] - 1), 0)),
                any_spec, any_spec, any_spec,
            ],
            out_specs=pl.BlockSpec((tm, dw), lambda i, *_: (i, 0)),
            scratch_shapes=[
                pltpu.VMEM((d, de), F32), pltpu.VMEM((d, de), F32), pltpu.VMEM((de, d), F32),
                pltpu.VMEM((d, de), BF16), pltpu.VMEM((d, de), BF16), pltpu.VMEM((de, d), BF16),
                pltpu.SemaphoreType.DMA((3,)),
            ],
        ),
        out_shape=jax.ShapeDtypeStruct((p, dw), jnp.uint32),
        compiler_params=_params("arbitrary"),
        name="moe_ffn",
    )(block_e, first, next_e, n_used, xb, wg, wu, wd)


def _combine_kernel(d0_ref, d1_ref, x_ref, meta_ref, yb_hbm, o_ref, a_ref, b_ref, sem, *, row0):
    base = row0 + pl.program_id(0) * x_ref.shape[0]
    _gather_rows(d0_ref, base, yb_hbm, a_ref, sem.at[0])
    _gather_rows(d1_ref, base, yb_hbm, b_ref, sem.at[1])
    a_lo, a_hi = _unpack_bf16_pairs(a_ref[...])
    b_lo, b_hi = _unpack_bf16_pairs(b_ref[...])
    g0 = meta_ref[:, META_G0:META_G0 + 1]
    g1 = meta_ref[:, META_G1:META_G1 + 1]
    half = a_lo.shape[1]
    o_ref[:, :half] = x_ref[:, :half] + (g0 * a_lo + g1 * b_lo)
    o_ref[:, half:] = x_ref[:, half:] + (g0 * a_hi + g1 * b_hi)


def moe_combine(x, meta, yb, d0, d1, row0, rows, tm=256):
    d = x.shape[1]
    assert row0 % tm == 0
    blk0 = row0 // tm
    return pl.pallas_call(
        functools.partial(_combine_kernel, row0=row0),
        grid_spec=pltpu.PrefetchScalarGridSpec(
            num_scalar_prefetch=2,
            grid=(rows // tm,),
            in_specs=[
                pl.BlockSpec((tm, d), lambda i, a, b: (blk0 + i, 0)),
                pl.BlockSpec((tm, ROUTER_LANES), lambda i, a, b: (blk0 + i, 0)),
                pl.BlockSpec(memory_space=pl.ANY),
            ],
            out_specs=pl.BlockSpec((tm, d), lambda i, a, b: (i, 0)),
            scratch_shapes=[pltpu.VMEM((tm, d // 2), jnp.uint32), pltpu.VMEM((tm, d // 2), jnp.uint32),
                            pltpu.SemaphoreType.DMA((2,))],
        ),
        out_shape=jax.ShapeDtypeStruct((rows, d), F32),
        compiler_params=_params("arbitrary"),
        name="moe_combine",
    )(d0, d1, x, meta, yb)


def moe_layout(meta, tot, tile, tm=MOE_TM):
    n = meta.shape[0]
    m = n * MOE_TOPK
    ids = jnp.arange(MOE_EXPERTS, dtype=jnp.int32)
    cnt = tot[:, 0, MOE_GROUPS:MOE_GROUPS + MOE_EXPERTS].astype(jnp.int32)
    counts = jnp.sum(cnt, axis=0)
    padded = (counts + tm - 1) // tm * tm
    pad_end = jnp.cumsum(padded)
    pad_start = pad_end - padded
    base = pad_start[None, :] + jnp.cumsum(cnt, axis=0) - cnt
    base_tok = jnp.repeat(base, tile, axis=0)

    def slots(e_col, r_col):
        e = meta[:, e_col].astype(jnp.int32)
        return jnp.sum(jnp.where(e[:, None] == ids[None, :], base_tok, 0), axis=1) + meta[:, r_col].astype(jnp.int32)

    d0 = slots(META_E0, META_R0)
    d1 = slots(META_E1, META_R1)
    n_blocks = (m + MOE_EXPERTS * (tm - 1) + tm - 1) // tm

    blk = jnp.arange(n_blocks, dtype=jnp.int32)
    n_used = (pad_end[-1] // tm).astype(jnp.int32)
    block_e = jnp.sum((pad_end[None, :] <= (blk * tm)[:, None]).astype(jnp.int32), axis=1)
    block_e = jnp.minimum(block_e, MOE_EXPERTS - 1)
    live = blk < n_used
    first = (live & ((blk == 0) | (block_e != jnp.roll(block_e, 1)))).astype(jnp.int32)
    has_rows = counts > 0
    later = jnp.arange(MOE_EXPERTS)[None, :] > jnp.arange(MOE_EXPERTS)[:, None]
    nxt_of_e = jnp.min(jnp.where(later & has_rows[None, :], jnp.arange(MOE_EXPERTS)[None, :], MOE_EXPERTS), axis=1)
    nxt_of_e = jnp.where(nxt_of_e == MOE_EXPERTS, -1, nxt_of_e).astype(jnp.int32)
    next_e = jnp.sum(jnp.where(block_e[:, None] == ids[None, :], nxt_of_e[None, :], 0), axis=1)
    pad_first = (pad_start + counts).astype(jnp.int32)
    pad_cnt = (padded - counts).astype(jnp.int32)
    return d0, d1, pad_first, pad_cnt, n_blocks * tm, block_e, first, next_e, n_used.reshape(1)


def hier_moe_residual(x, g, wr_g, br_g, wr_e, br_e, w_gate, w_up, w_down, layer, out_segments=None, tile=256):
    n, d = x.shape
    pad = ROUTER_LANES - MOE_GROUPS - MOE_EXPERTS
    wr = jnp.concatenate([wr_g, wr_e, jnp.zeros((d, pad), F32)], axis=1)
    br = jnp.concatenate([br_g, br_e, jnp.zeros((pad,), F32)]).reshape(1, ROUTER_LANES)
    hp, meta, tot = moe_route(x, g, wr, br, tile)
    d0, d1, pad_first, pad_cnt, p, block_e, first, next_e, n_used = moe_layout(meta, tot, tile)
    xb = moe_dispatch(hp, d0, d1, pad_first, pad_cnt, n_used, p)
    yb = expert_ffn(xb, block_e, first, next_e, n_used, w_gate, w_up, w_down, layer)
    if out_segments is None:
        return moe_combine(x, meta, yb, d0, d1, 0, n)
    return [moe_combine(x, meta, yb, d0, d1, row0, rows) for row0, rows in out_segments]


def kernel(x_prompt, x_sample, norm_mix, norm_ffn, ev_w_in, ev_w_out, a_ln_g, a_ln_b, a_w_s, a_b_s, b_q_gain, b_k_gain, b_rpb, od_w_in, od_w_out, c_q_gain, c_k_gain, t5_table, moe_wr_g, moe_br_g, moe_wr_e, moe_br_e, moe_w_gate, moe_w_up, moe_w_down):
    d = x_prompt.shape[-1]
    segs = [(x_prompt.shape[0], x_prompt.shape[1]), (x_sample.shape[0], x_sample.shape[1])]
    xs = [x_prompt.reshape(-1, d), x_sample.reshape(-1, d)]
    depth = norm_mix.shape[0]
    for l in range(depth):
        i = l // 2
        h = rmsnorm(xs, norm_mix[l], BF16)
        if l % 2 == 0:
            aw = a_ln_g.shape[1]
            z = matmul([[h]], [(ev_w_in[i], 0)])
            a_out = mixer_a(z, a_ln_g[i], a_ln_b[i], a_w_s[i], a_b_s[i])
            tab = natten_bias_table(b_rpb[i])
            b_out, row0 = [], 0
            for batch, seq in segs:
                b_out.append(natten(z, 2 * aw // LANES, row0, batch, seq, b_q_gain[i], b_k_gain[i], tab))
                row0 += batch * seq
            x = matmul([[a_out], b_out], [(ev_w_out[i], 0), (ev_w_out[i], aw)], residuals=xs)
        else:
            qkv = matmul([[h]], [(od_w_in[i], 0)])
            tabs = [dilated_bias_table(t5_table, dil) for _, dil in C_PATTERNS]
            c_out, row0 = [], 0
            for batch, seq in segs:
                c_out.append(dilated_attention(qkv, row0, batch, seq, c_q_gain[i], c_k_gain[i], tabs))
                row0 += batch * seq
            x = matmul([c_out], [(od_w_out[i], 0)], residuals=xs)
        out_segments = None
        if l == depth - 1:
            out_segments = [(0, segs[0][0] * segs[0][1]), (segs[0][0] * segs[0][1], segs[1][0] * segs[1][1])]
        res = hier_moe_residual(x, norm_ffn[l], moe_wr_g[l], moe_br_g[l], moe_wr_e[l], moe_br_e[l],
                                moe_w_gate, moe_w_up, moe_w_down, l, out_segments)
        xs = res if out_segments is not None else [res]
    return (xs[0].reshape(x_prompt.shape), xs[1].reshape(x_sample.shape))
```

```python
import functools
import math

import jax
import jax.numpy as jnp
from jax import lax
from jax.experimental import pallas as pl
from jax.experimental.pallas import tpu as pltpu

F32 = jnp.float32
BF16 = jnp.bfloat16
EPS = 1e-6
NEG = -1e30

V7X_VMEM_LIMIT_BYTES = 56 * 1024 * 1024
LANES = 128

GRID_W = 64
CHUNK = 128
A_GROUPS = 8
NA_KH = 8
NA_KW = 16
B_DH = 64
C_DH = 128
C_PATTERNS = ((128, 1), (512, 4), (2048, 16))
C_RAD = 64
C_TQ = 128
C_TK = C_TQ + 2 * C_RAD
C_SUPER = C_TQ * 16
T5_BUCKETS = 32
T5_MAX_DIST = 1024
MOE_GROUPS = 4
MOE_PER_GROUP = 8
MOE_EXPERTS = MOE_GROUPS * MOE_PER_GROUP
MOE_TOPK = 2
MOE_TM = 256
ROUTER_LANES = 128


def _params(*sem):
    return pltpu.CompilerParams(dimension_semantics=sem, vmem_limit_bytes=V7X_VMEM_LIMIT_BYTES)


def _row_sources(arrays, tm, width, col_of):
    specs, spans, off = [], [], 0
    for a in arrays:
        assert a.shape[0] % tm == 0 and a.shape[0] >= tm
        nblk = a.shape[0] // tm
        specs.append(pl.BlockSpec(
            (tm, width), lambda *g, off=off, nblk=nblk: (jnp.clip(g[-1] - off, 0, nblk - 1), col_of(*g))))
        spans.append((off, nblk))
        off += nblk
    return specs, spans


def _active_rows(i, refs, spans):
    val = refs[-1][...]
    for ref, (off, nblk) in reversed(list(zip(refs[:-1], spans[:-1]))):
        val = jnp.where(i < off + nblk, ref[...], val)
    return val


def _rmsnorm_kernel(*refs, spans):
    x_refs, (g_ref, o_ref) = refs[:len(spans)], refs[len(spans):]
    x = _active_rows(pl.program_id(0), x_refs, spans)
    y = x * lax.rsqrt(jnp.mean(x * x, axis=-1, keepdims=True) + EPS)
    o_ref[...] = (y * g_ref[...]).astype(o_ref.dtype)


def rmsnorm(xs, g, out_dtype, tm=1024):
    d = xs[0].shape[1]
    n = sum(x.shape[0] for x in xs)
    specs, spans = _row_sources(xs, tm, d, lambda i: 0)
    return pl.pallas_call(
        functools.partial(_rmsnorm_kernel, spans=spans),
        grid=(n // tm,),
        in_specs=specs + [pl.BlockSpec((1, d), lambda i: (0, 0))],
        out_specs=pl.BlockSpec((tm, d), lambda i: (i, 0)),
        out_shape=jax.ShapeDtypeStruct((n, d), out_dtype),
        compiler_params=_params("arbitrary"),
        name="rmsnorm",
    )(*xs, g.reshape(1, d))


def _matmul_kernel(*refs, x_spans, res_spans, k_chunk):
    refs = list(refs)
    x_refs = [[refs.pop(0) for _ in spans] for spans in x_spans]
    w_refs = [refs.pop(0) for _ in x_spans]
    res_refs = [refs.pop(0) for _ in res_spans]
    o_ref, wb_refs = refs[0], refs[1:]
    i = pl.program_id(1)

    @pl.when(pl.program_id(1) == 0)
    def _():
        for w_ref, wb_ref in zip(w_refs, wb_refs):
            rows = math.gcd(k_chunk, w_ref.shape[0])

            def cast(c, carry, w_ref=w_ref, wb_ref=wb_ref, rows=rows):
                sl = pl.ds(pl.multiple_of(c * rows, rows), rows)
                wb_ref[sl, :] = w_ref[sl, :].astype(BF16)
                return carry
            lax.fori_loop(0, w_ref.shape[0] // rows, cast, 0)

    acc = None
    for pieces, spans, wb_ref in zip(x_refs, x_spans, wb_refs):
        part = jnp.dot(_active_rows(i, pieces, spans), wb_ref[...], preferred_element_type=F32)
        acc = part if acc is None else acc + part
    if res_spans:
        acc = _active_rows(i, res_refs, res_spans) + acc
    o_ref[...] = acc.astype(o_ref.dtype)


def _matmul_row_tile(xs, residuals, tn, candidates=(1024, 512, 256)):
    ks = [pieces[0].shape[1] for pieces in xs]
    weights = sum(2 * k * tn * 4 + k * tn * 2 for k in ks)
    for tm in candidates:
        rows_ok = all(p.shape[0] % tm == 0 for pieces in xs for p in pieces) and \
            all(r.shape[0] % tm == 0 for r in residuals)
        blocks = sum(2 * len(pieces) * tm * k * 2 for pieces, k in zip(xs, ks))
        blocks += (2 + 2 * len(residuals) + 1) * tm * tn * 4
        if rows_ok and weights + blocks <= 0.8 * V7X_VMEM_LIMIT_BYTES:
            return tm
    raise ValueError("no row tile fits")


def matmul(xs, ws, residuals=(), out_dtype=F32, tn=1024):
    n = sum(x.shape[0] for x in xs[0])
    m = ws[0][0].shape[1]
    tn = min(tn, m)
    tm = _matmul_row_tile(xs, residuals, tn)
    in_specs, x_spans, w_specs = [], [], []
    for pieces, (w, row) in zip(xs, ws):
        k = pieces[0].shape[1]
        assert row % k == 0
        specs, spans = _row_sources(pieces, tm, k, lambda j, i: 0)
        in_specs += specs
        x_spans.append(spans)
        w_specs.append(pl.BlockSpec((k, tn), lambda j, i, rb=row // k: (rb, j)))
    res_specs, res_spans = _row_sources(list(residuals), tm, tn, lambda j, i: j)
    kern = functools.partial(_matmul_kernel, x_spans=x_spans, res_spans=res_spans, k_chunk=256)
    return pl.pallas_call(
        kern,
        grid=(m // tn, n // tm),
        in_specs=in_specs + w_specs + res_specs,
        out_specs=pl.BlockSpec((tm, tn), lambda j, i: (i, j)),
        out_shape=jax.ShapeDtypeStruct((n, m), out_dtype),
        scratch_shapes=[pltpu.VMEM((pieces[0].shape[1], tn), BF16) for pieces in xs],
        compiler_params=_params("arbitrary", "arbitrary"),
        name="matmul",
    )(*[p for pieces in xs for p in pieces], *[w for w, _ in ws], *residuals)


def _mixer_a_kernel(u_ref, v_ref, lng_ref, lnb_ref, ws_ref, bs_ref, o_ref):
    tm = u_ref.shape[0]
    gd = u_ref.shape[1] // A_GROUPS
    for c in range(tm // CHUNK):
        rows = slice(c * CHUNK, (c + 1) * CHUNK)
        u = jax.nn.gelu(u_ref[rows, :])
        v = jax.nn.gelu(v_ref[rows, :])
        mu = jnp.mean(v, axis=-1, keepdims=True)
        vc = v - mu
        var = jnp.mean(vc * vc, axis=-1, keepdims=True)
        v = vc * lax.rsqrt(var + EPS) * lng_ref[...] + lnb_ref[...]
        for g in range(A_GROUPS):
            cols = slice(g * gd, (g + 1) * gd)
            f = jnp.dot(ws_ref[g].astype(BF16), v[:, cols].astype(BF16), preferred_element_type=F32)
            f = f + bs_ref[g]
            o_ref[rows, cols] = (u[:, cols] * f).astype(o_ref.dtype)


def mixer_a(z, ln_g, ln_b, w_s, b_s, tm=512):
    n = z.shape[0]
    aw = ln_g.shape[0]
    gd = aw // A_GROUPS
    bs_b = jnp.broadcast_to(b_s[:, :, None], (A_GROUPS, CHUNK, gd))
    return pl.pallas_call(
        _mixer_a_kernel,
        grid=(n // tm,),
        in_specs=[
            pl.BlockSpec((tm, aw), lambda i: (i, 0)),
            pl.BlockSpec((tm, aw), lambda i: (i, 1)),
            pl.BlockSpec((1, aw), lambda i: (0, 0)),
            pl.BlockSpec((1, aw), lambda i: (0, 0)),
            pl.BlockSpec((A_GROUPS, CHUNK, CHUNK), lambda i: (0, 0, 0)),
            pl.BlockSpec((A_GROUPS, CHUNK, gd), lambda i: (0, 0, 0)),
        ],
        out_specs=pl.BlockSpec((tm, aw), lambda i: (i, 0)),
        out_shape=jax.ShapeDtypeStruct((n, aw), BF16),
        compiler_params=_params("parallel"),
        name="mixer_a",
    )(z, z, ln_g.reshape(1, aw), ln_b.reshape(1, aw), w_s, bs_b)


def _skew(w, rows):
    n = w.shape[-1]
    flat = jnp.tile(w, (1,) * (w.ndim - 1) + (rows,))
    return flat[..., :rows * (n - 1)].reshape(w.shape[:-1] + (rows, n - 1))


def _pair_rmsnorm(x, gain):
    sq = x * x
    lane = lax.broadcasted_iota(jnp.int32, x.shape, 1)
    lo = lane < B_DH
    s_lo = jnp.sum(jnp.where(lo, sq, 0.0), axis=-1, keepdims=True)
    s_hi = jnp.sum(jnp.where(lo, 0.0, sq), axis=-1, keepdims=True)
    ms = jnp.where(lo, s_lo, s_hi) * (1.0 / B_DH)
    return x * lax.rsqrt(ms + EPS) * gain


def _natten_kernel(q_ref, k_ref, v_ref, gq_ref, gk_ref, bias_ref, o_ref, kn_ref, vb_ref, s_ref, m_ref, e_ref,
                   *, rows_per_step, rows):
    seq = k_ref.shape[0]
    t = pl.program_id(2)
    prep = 256

    @pl.when(t == 0)
    def _():
        def body(c, carry):
            sl = pl.ds(pl.multiple_of(c * prep, prep), prep)
            kn_ref[sl, :] = _pair_rmsnorm(k_ref[sl, :], gk_ref[...]).astype(BF16)
            vb_ref[sl, :LANES] = v_ref[sl, :].astype(BF16)
            vb_ref[sl, LANES:] = jnp.ones((prep, LANES), BF16)
            return carry
        lax.fori_loop(0, seq // prep, body, 0)

    q = (_pair_rmsnorm(q_ref[...], gq_ref[...]) * (B_DH ** -0.5)).astype(BF16)
    lo = lax.broadcasted_iota(jnp.int32, (GRID_W, LANES), 1) < B_DH
    zero = jnp.zeros((GRID_W, LANES), BF16)
    win = NA_KH * GRID_W

    def window(rr):
        r = t * rows_per_step + rr
        rs = jnp.clip(r - NA_KH // 2, 0, rows - NA_KH)
        return r - rs, pl.ds(pl.multiple_of(rs * GRID_W, GRID_W), win)

    for rr in range(rows_per_step):
        case, ksl = window(rr)
        qr = q[rr * GRID_W:(rr + 1) * GRID_W, :]
        q2 = jnp.concatenate([jnp.where(lo, qr, zero), jnp.where(lo, zero, qr)], axis=0)
        s = lax.dot_general(q2, kn_ref[ksl, :], (((1,), (1,)), ((), ())), preferred_element_type=F32)
        s_ref[rr] = s + bias_ref[0, case]
    for rr in range(rows_per_step):
        m = jnp.max(s_ref[rr], axis=-1, keepdims=True)
        m_ref[rr] = jnp.broadcast_to(m, (2 * GRID_W, LANES))
    for rr in range(rows_per_step):
        mb = m_ref[rr]
        for c in range(win // LANES):
            cols = slice(c * LANES, (c + 1) * LANES)
            e_ref[rr, :, cols] = jnp.exp(s_ref[rr, :, cols] - mb).astype(BF16)
    for rr in range(rows_per_step):
        _, ksl = window(rr)
        ol = jnp.dot(e_ref[rr], vb_ref[ksl, :], preferred_element_type=F32)
        o = ol[:, :LANES] / ol[:, LANES:]
        o_ref[rr * GRID_W:(rr + 1) * GRID_W, :] = jnp.where(lo, o[:GRID_W], o[GRID_W:]).astype(o_ref.dtype)


def natten_bias_table(rpb):
    heads = rpb.shape[0]
    col = jnp.arange(GRID_W)
    col_start = jnp.clip(col - NA_KW // 2, 0, GRID_W - NA_KW)
    col_valid = (col[None, :] >= col_start[:, None]) & (col[None, :] < col_start[:, None] + NA_KW)
    edge = GRID_W - NA_KW
    w = jnp.concatenate([jnp.repeat(rpb[..., :1], edge, axis=-1), rpb.astype(F32),
                         jnp.repeat(rpb[..., -1:], edge, axis=-1), jnp.zeros(rpb.shape[:-1] + (1,), F32)], axis=-1)
    bias_c = _skew(w, GRID_W)[..., GRID_W - 1:]
    tab = jnp.stack([bias_c[:, NA_KH - 1 - c:2 * NA_KH - 1 - c] for c in range(NA_KH)], axis=1)
    tab = jnp.where(col_valid[:, None, :], tab.transpose(0, 1, 3, 2, 4), NEG)
    tab = tab.reshape(heads // 2, 2, NA_KH, GRID_W, NA_KH * GRID_W).transpose(0, 2, 1, 3, 4)
    return tab.reshape(heads // 2, NA_KH, 2 * GRID_W, NA_KH * GRID_W)


def natten(z, col0, row0, batch, seq, gq, gk, bias_tab, rows_per_step=16):
    hp = bias_tab.shape[0]
    rows = seq // GRID_W
    assert rows >= NA_KH and rows % rows_per_step == 0 and row0 % seq == 0
    tq = rows_per_step * GRID_W
    steps = rows // rows_per_step
    qb0 = row0 // tq
    sb0 = row0 // seq
    win = NA_KH * GRID_W
    kern = functools.partial(_natten_kernel, rows_per_step=rows_per_step, rows=rows)
    gq2 = jnp.tile(gq, 2).reshape(1, 2 * B_DH)
    gk2 = jnp.tile(gk, 2).reshape(1, 2 * B_DH)
    return pl.pallas_call(
        kern,
        grid=(batch, hp, steps),
        in_specs=[
            pl.BlockSpec((tq, LANES), lambda b, h, t: (qb0 + b * steps + t, col0 + h)),
            pl.BlockSpec((seq, LANES), lambda b, h, t: (sb0 + b, col0 + hp + h)),
            pl.BlockSpec((seq, LANES), lambda b, h, t: (sb0 + b, col0 + 2 * hp + h)),
            pl.BlockSpec((1, LANES), lambda b, h, t: (0, 0)),
            pl.BlockSpec((1, LANES), lambda b, h, t: (0, 0)),
            pl.BlockSpec((1, NA_KH, 2 * GRID_W, win), lambda b, h, t: (h, 0, 0, 0)),
        ],
        out_specs=pl.BlockSpec((tq, LANES), lambda b, h, t: (b * steps + t, h)),
        out_shape=jax.ShapeDtypeStruct((batch * seq, 2 * hp * B_DH), BF16),
        scratch_shapes=[
            pltpu.VMEM((seq, LANES), BF16),
            pltpu.VMEM((seq, 2 * LANES), BF16),
            pltpu.VMEM((rows_per_step, 2 * GRID_W, win), F32),
            pltpu.VMEM((rows_per_step, 2 * GRID_W, LANES), F32),
            pltpu.VMEM((rows_per_step, 2 * GRID_W, win), BF16),
        ],
        compiler_params=_params("parallel", "parallel", "arbitrary"),
        name="natten",
    )(z, z, z, gq2, gk2, bias_tab)


def t5_bucket(rel):
    nb = T5_BUCKETS // 2
    max_exact = nb // 2
    ret = jnp.where(rel > 0, nb, 0)
    n = jnp.abs(rel)
    large = max_exact + (jnp.log(jnp.maximum(n, 1).astype(F32) / max_exact)
                         / math.log(T5_MAX_DIST / max_exact) * (nb - max_exact)).astype(jnp.int32)
    large = jnp.minimum(large, nb - 1)
    return ret + jnp.where(n < max_exact, n, large)


def dilated_bias_table(t5_table, dil):
    span = C_TK - 1
    rel = jnp.arange(-span, span + 1)
    vec = t5_table.astype(F32)[t5_bucket(rel * dil)]
    vec = jnp.where((jnp.abs(rel) <= C_RAD)[:, None], vec, NEG).T
    w = jnp.concatenate([vec, jnp.zeros((vec.shape[0], 1), F32)], axis=1)
    r = _skew(w, C_TQ)
    return jnp.stack([r[:, :, span - off:span - off + C_TK] for off in (0, C_RAD, 2 * C_RAD)], axis=1)


def _row_rmsnorm(x, gain):
    return x * lax.rsqrt(jnp.mean(x * x, axis=-1, keepdims=True) + EPS) * gain


def _dilated_kernel(q_ref, k_ref, v_ref, gq_ref, gk_ref, b0_ref, b1_ref, b2_ref, o_ref,
                    kn_ref, qn_ref, oacc_ref, lse_ref, s_ref, m_ref, e_ref, *, seq, unroll):
    t = pl.program_id(2)
    prep = 256

    @pl.when(t == 0)
    def _():
        def body(c, carry):
            sl = pl.ds(pl.multiple_of(c * prep, prep), prep)
            kn_ref[sl, :] = _row_rmsnorm(k_ref[sl, :], gk_ref[...])
            return carry
        lax.fori_loop(0, seq // prep, body, 0)

    qn_ref[...] = _row_rmsnorm(q_ref[...], gq_ref[...]) * (C_DH ** -0.5)
    n_tiles = C_SUPER // C_TQ

    for p, ((window, dil), b_ref) in enumerate(zip(C_PATTERNS, (b0_ref, b1_ref, b2_ref))):
        n_sub = seq // dil
        tk = min(C_TK, n_sub)
        span = C_TQ * dil
        ones = jnp.ones((tk, LANES), BF16)

        def tile(idx, dil=dil, n_sub=n_sub, tk=tk, span=span):
            u = idx // dil
            s = idx % dil
            qpos = u * span + s
            j0 = t * (C_SUPER // dil) + u * C_TQ
            ws = jnp.clip(j0 - C_RAD, 0, n_sub - tk)
            kpos = ws * dil + s
            if dil == 1:
                return (pl.ds(pl.multiple_of(qpos, C_TQ), C_TQ), pl.ds(pl.multiple_of(kpos, C_RAD), tk),
                        (j0 - ws) // C_RAD)
            return pl.ds(qpos, C_TQ, stride=dil), pl.ds(kpos, tk, stride=dil), (j0 - ws) // C_RAD

        def scores(idx, carry, tile=tile, tk=tk, b_ref=b_ref):
            qsl, ksl, case = tile(idx)
            q = qn_ref[qsl, :].astype(BF16)
            k = kn_ref[ksl, :].astype(BF16)
            sc = lax.dot_general(q, k, (((1,), (1,)), ((), ())), preferred_element_type=F32)
            s_ref[idx, :, :tk] = sc + b_ref[0, case, :, :tk]
            return carry

        def row_max(idx, carry, tk=tk):
            m = jnp.max(s_ref[idx, :, :tk], axis=-1, keepdims=True)
            m_ref[idx] = jnp.broadcast_to(m, (C_TQ, LANES))
            return carry

        def exps(idx, carry, tk=tk):
            mb = m_ref[idx]
            for c in range(tk // LANES):
                cols = slice(c * LANES, (c + 1) * LANES)
                e_ref[idx, :, cols] = jnp.exp(s_ref[idx, :, cols] - mb).astype(BF16)
            return carry

        def outputs(idx, carry, tile=tile, tk=tk, ones=ones, p=p):
            qsl, ksl, _ = tile(idx)
            v1 = jnp.concatenate([v_ref[ksl, :].astype(BF16), ones], axis=1)
            ol = jnp.dot(e_ref[idx, :, :tk], v1, preferred_element_type=F32)
            l = ol[:, LANES:]
            oacc_ref[p, qsl, :] = ol[:, :LANES] / l
            lse_ref[p, qsl, :] = m_ref[idx] + jnp.log(l)
            return carry

        for phase in (scores, row_max, exps, outputs):
            lax.fori_loop(0, n_tiles, phase, 0, unroll=unroll)

    l0, l1, l2 = lse_ref[0], lse_ref[1], lse_ref[2]
    mx = jnp.maximum(jnp.maximum(l0, l1), l2)
    w0, w1, w2 = jnp.exp(l0 - mx), jnp.exp(l1 - mx), jnp.exp(l2 - mx)
    num = w0 * oacc_ref[0] + w1 * oacc_ref[1] + w2 * oacc_ref[2]
    o_ref[...] = (num / (w0 + w1 + w2)).astype(o_ref.dtype)


def dilated_attention(qkv, row0, batch, seq, gq, gk, tabs, unroll=8):
    heads = tabs[0].shape[0]
    assert seq % C_SUPER == 0 and row0 % seq == 0
    steps = seq // C_SUPER
    qb0 = row0 // C_SUPER
    sb0 = row0 // seq
    n_tiles = C_SUPER // C_TQ
    kern = functools.partial(_dilated_kernel, seq=seq, unroll=unroll)
    tab_specs = [pl.BlockSpec((1,) + tuple(tb.shape[1:]), lambda b, h, t: (h, 0, 0, 0)) for tb in tabs]
    return pl.pallas_call(
        kern,
        grid=(batch, heads, steps),
        in_specs=[
            pl.BlockSpec((C_SUPER, LANES), lambda b, h, t: (qb0 + b * steps + t, h)),
            pl.BlockSpec((seq, LANES), lambda b, h, t: (sb0 + b, heads + h)),
            pl.BlockSpec((seq, LANES), lambda b, h, t: (sb0 + b, 2 * heads + h)),
            pl.BlockSpec((1, LANES), lambda b, h, t: (0, 0)),
            pl.BlockSpec((1, LANES), lambda b, h, t: (0, 0)),
        ] + tab_specs,
        out_specs=pl.BlockSpec((C_SUPER, LANES), lambda b, h, t: (b * steps + t, h)),
        out_shape=jax.ShapeDtypeStruct((batch * seq, heads * C_DH), BF16),
        scratch_shapes=[
            pltpu.VMEM((seq, LANES), F32),
            pltpu.VMEM((C_SUPER, LANES), F32),
            pltpu.VMEM((len(C_PATTERNS), C_SUPER, LANES), F32),
            pltpu.VMEM((len(C_PATTERNS), C_SUPER, LANES), F32),
            pltpu.VMEM((n_tiles, C_TQ, C_TK), F32),
            pltpu.VMEM((n_tiles, C_TQ, LANES), F32),
            pltpu.VMEM((n_tiles, C_TQ, C_TK), BF16),
        ],
        compiler_params=_params("parallel", "parallel", "arbitrary"),
        name="dilated",
    )(qkv, qkv, qkv, gq.reshape(1, C_DH), gk.reshape(1, C_DH), *tabs)


def _pack_bf16_pairs(lo, hi):
    lo_bits = lax.bitcast_convert_type(lo.astype(BF16).astype(F32), jnp.uint32)
    hi_bits = lax.bitcast_convert_type(hi.astype(BF16).astype(F32), jnp.uint32)
    return (hi_bits & jnp.uint32(0xFFFF0000)) | (lo_bits >> 16)


def _unpack_bf16_pairs(packed):
    lo = lax.bitcast_convert_type(packed << 16, F32)
    hi = lax.bitcast_convert_type(packed & jnp.uint32(0xFFFF0000), F32)
    return lo, hi


META_E0, META_E1, META_R0, META_R1, META_G0, META_G1 = range(6)


def _route_kernel(x_ref, g_ref, wr_ref, br_ref, hp_ref, meta_ref, tot_ref):
    x = x_ref[...]
    h = x * lax.rsqrt(jnp.mean(x * x, axis=-1, keepdims=True) + EPS) * g_ref[...]
    half = h.shape[1] // 2
    hp_ref[...] = _pack_bf16_pairs(h[:, :half], h[:, half:])
    logits = jnp.dot(h, wr_ref[...], preferred_element_type=F32, precision=lax.Precision.HIGHEST) + br_ref[...]
    tm = x.shape[0]
    lane = lax.broadcasted_iota(jnp.int32, (tm, ROUTER_LANES), 1)
    lane_f = lane.astype(F32)

    def first_max(v):
        mx = jnp.max(v, axis=-1, keepdims=True)
        return mx, jnp.min(jnp.where(v == mx, lane_f, float(ROUTER_LANES)), axis=-1, keepdims=True)

    lg = jnp.where(lane < MOE_GROUPS, logits, -jnp.inf)
    gmax, gsel = first_max(lg)
    p_g = 1.0 / jnp.sum(jnp.exp(lg - gmax), axis=-1, keepdims=True)
    expert_lane = (lane >= MOE_GROUPS) & (lane < MOE_GROUPS + MOE_EXPERTS)
    in_group = expert_lane & (((lane - MOE_GROUPS) // MOE_PER_GROUP) == gsel.astype(jnp.int32))
    le = jnp.where(in_group, logits, -jnp.inf)
    v0, i0 = first_max(le)
    pick0 = lane_f == i0
    v1, i1 = first_max(jnp.where(pick0, -jnp.inf, le))
    pick1 = lane_f == i1
    t = jnp.exp(v1 - v0)
    g0 = p_g / (1.0 + t)
    g1 = p_g * t / (1.0 + t)

    onehot = jnp.where(pick0 | pick1, 1.0, 0.0).astype(BF16)
    row = lax.broadcasted_iota(jnp.int32, (tm, tm), 0)
    col = lax.broadcasted_iota(jnp.int32, (tm, tm), 1)
    earlier = jnp.where(col < row, 1.0, 0.0).astype(BF16)
    before = jnp.dot(earlier, onehot, preferred_element_type=F32)
    r0 = jnp.sum(jnp.where(pick0, before, 0.0), axis=-1, keepdims=True)
    r1 = jnp.sum(jnp.where(pick1, before, 0.0), axis=-1, keepdims=True)

    meta = jnp.zeros((tm, ROUTER_LANES), F32)
    for pos, val in ((META_E0, i0 - MOE_GROUPS), (META_E1, i1 - MOE_GROUPS), (META_R0, r0), (META_R1, r1),
                     (META_G0, g0), (META_G1, g1)):
        meta = jnp.where(lane == pos, val, meta)
    meta_ref[...] = meta
    tot_ref[0] = jnp.sum(onehot.astype(F32), axis=0, keepdims=True)


def moe_route(x, g, wr, br, tm):
    n, d = x.shape
    return pl.pallas_call(
        _route_kernel,
        grid=(n // tm,),
        in_specs=[
            pl.BlockSpec((tm, d), lambda i: (i, 0)),
            pl.BlockSpec((1, d), lambda i: (0, 0)),
            pl.BlockSpec((d, ROUTER_LANES), lambda i: (0, 0)),
            pl.BlockSpec((1, ROUTER_LANES), lambda i: (0, 0)),
        ],
        out_specs=[
            pl.BlockSpec((tm, d // 2), lambda i: (i, 0)),
            pl.BlockSpec((tm, ROUTER_LANES), lambda i: (i, 0)),
            pl.BlockSpec((1, 1, ROUTER_LANES), lambda i: (i, 0, 0)),
        ],
        out_shape=[
            jax.ShapeDtypeStruct((n, d // 2), jnp.uint32),
            jax.ShapeDtypeStruct((n, ROUTER_LANES), F32),
            jax.ShapeDtypeStruct((n // tm, 1, ROUTER_LANES), F32),
        ],
        compiler_params=_params("parallel"),
        name="moe_route",
    )(x, g.reshape(1, d), wr, br)


def _dispatch_kernel(d0_ref, d1_ref, pad_first_ref, pad_cnt_ref, nused_ref, h_ref, xb_hbm, sem):
    i = pl.program_id(0)
    tm = h_ref.shape[0]
    base = i * tm

    def row_copy(src_row, dst_row):
        return pltpu.make_async_copy(h_ref.at[pl.ds(src_row, 1), :], xb_hbm.at[pl.ds(dst_row, 1), :], sem.at[0])

    for r in range(tm):
        row_copy(r, d0_ref[base + r]).start()
        row_copy(r, d1_ref[base + r]).start()

    def drain(r, carry):
        row_copy(0, 0).wait()
        row_copy(0, 0).wait()
        return carry
    lax.fori_loop(0, tm, drain, 0, unroll=8)

    @pl.when(i == pl.num_programs(0) - 1)
    def _():
        def per_expert(e, carry):
            def fill(j, c):
                row_copy(0, pad_first_ref[e] + j).start()
                return c
            lax.fori_loop(0, pad_cnt_ref[e], fill, 0)

            def fill_wait(j, c):
                row_copy(0, 0).wait()
                return c
            lax.fori_loop(0, pad_cnt_ref[e], fill_wait, 0)
            return carry
        lax.fori_loop(0, MOE_EXPERTS, per_expert, 0)

        def block_copy(b):
            return pltpu.make_async_copy(h_ref, xb_hbm.at[pl.ds(pl.multiple_of(b * tm, tm), tm), :], sem.at[0])

        def fill_block(b, c):
            block_copy(b).start()
            return c
        lax.fori_loop(nused_ref[0], xb_hbm.shape[0] // tm, fill_block, 0)

        def fill_block_wait(b, c):
            block_copy(b).wait()
            return c
        lax.fori_loop(nused_ref[0], xb_hbm.shape[0] // tm, fill_block_wait, 0)


def moe_dispatch(hp, d0, d1, pad_first, pad_cnt, n_used, p, tm=MOE_TM):
    n, dw = hp.shape
    return pl.pallas_call(
        _dispatch_kernel,
        grid_spec=pltpu.PrefetchScalarGridSpec(
            num_scalar_prefetch=5,
            grid=(n // tm,),
            in_specs=[pl.BlockSpec((tm, dw), lambda i, *_: (i, 0))],
            out_specs=pl.BlockSpec(memory_space=pl.ANY),
            scratch_shapes=[pltpu.SemaphoreType.DMA((1,))],
        ),
        out_shape=jax.ShapeDtypeStruct((p, dw), hp.dtype),
        compiler_params=_params("arbitrary"),
        name="moe_dispatch",
    )(d0, d1, pad_first, pad_cnt, n_used, hp)


def _row_gather_copy(src_hbm, dst_ref, sem, src_row, dst_row):
    return pltpu.make_async_copy(src_hbm.at[pl.ds(src_row, 1), :], dst_ref.at[pl.ds(dst_row, 1), :], sem)


def _gather_rows(idx_ref, base, src_hbm, dst_ref, sem, unroll=8):
    rows = dst_ref.shape[0]
    for r in range(rows):
        _row_gather_copy(src_hbm, dst_ref, sem, idx_ref[base + r], r).start()

    def drain(r, carry):
        _row_gather_copy(src_hbm, dst_ref, sem, 0, r).wait()
        return carry
    lax.fori_loop(0, rows, drain, 0, unroll=unroll)


def _expert_weight_copies(layer, e, w_hbm, stage, sem):
    return [pltpu.make_async_copy(w.at[layer, e], s, sem.at[k]) for k, (w, s) in enumerate(zip(w_hbm, stage))]


def _ffn_kernel(be_ref, first_ref, next_ref, nused_ref, x_ref, wg_hbm, wu_hbm, wd_hbm, o_ref,
                sg_ref, su_ref, sd_ref, wg_ref, wu_ref, wd_ref, sem, *, layer, k_chunk):
    i = pl.program_id(0)
    w_hbm = (wg_hbm, wu_hbm, wd_hbm)
    stage = (sg_ref, su_ref, sd_ref)
    resident = (wg_ref, wu_ref, wd_ref)

    @pl.when(i == 0)
    def _():
        for cp in _expert_weight_copies(layer, be_ref[0], w_hbm, stage, sem):
            cp.start()

    @pl.when(first_ref[i] == 1)
    def _():
        for cp in _expert_weight_copies(layer, be_ref[i], w_hbm, stage, sem):
            cp.wait()
        for s_ref, w_ref in zip(stage, resident):
            rows = math.gcd(k_chunk, s_ref.shape[0])

            def cast(c, carry, s_ref=s_ref, w_ref=w_ref, rows=rows):
                sl = pl.ds(pl.multiple_of(c * rows, rows), rows)
                w_ref[sl, :] = s_ref[sl, :].astype(BF16)
                return carry
            lax.fori_loop(0, s_ref.shape[0] // rows, cast, 0)

        @pl.when(next_ref[i] >= 0)
        def _():
            for cp in _expert_weight_copies(layer, next_ref[i], w_hbm, stage, sem):
                cp.start()

    live = i < nused_ref[0]

    @pl.when(live)
    def _():
        lo, hi = _unpack_bf16_pairs(x_ref[...])
        lo, hi = lo.astype(BF16), hi.astype(BF16)
        half = lo.shape[1]
        g = (jnp.dot(lo, wg_ref[:half, :], preferred_element_type=F32)
             + jnp.dot(hi, wg_ref[half:, :], preferred_element_type=F32))
        u = (jnp.dot(lo, wu_ref[:half, :], preferred_element_type=F32)
             + jnp.dot(hi, wu_ref[half:, :], preferred_element_type=F32))
        h = (jax.nn.silu(g) * u).astype(BF16)
        y = jnp.dot(h, wd_ref[...], preferred_element_type=F32)
        o_ref[...] = _pack_bf16_pairs(y[:, :half], y[:, half:])

    @pl.when(jnp.logical_not(live))
    def _():
        o_ref[...] = jnp.zeros_like(o_ref)


def expert_ffn(xb, block_e, first, next_e, n_used, wg, wu, wd, layer, tm=MOE_TM):
    p, dw = xb.shape
    d, de = wg.shape[2], wg.shape[3]
    any_spec = pl.BlockSpec(memory_space=pl.ANY)
    return pl.pallas_call(
        functools.partial(_ffn_kernel, layer=layer, k_chunk=256),
        grid_spec=pltpu.PrefetchScalarGridSpec(
            num_scalar_prefetch=4,
            grid=(p // tm,),
            in_specs=[
                pl.BlockSpec((tm, dw), lambda i, be, fi, ne, nu: (jnp.minimum(i, nu[0] - 1), 0)),
                any_spec, any_spec, any_spec,
            ],
            out_specs=pl.BlockSpec((tm, dw), lambda i, *_: (i, 0)),
            scratch_shapes=[
                pltpu.VMEM((d, de), F32), pltpu.VMEM((d, de), F32), pltpu.VMEM((de, d), F32),
                pltpu.VMEM((d, de), BF16), pltpu.VMEM((d, de), BF16), pltpu.VMEM((de, d), BF16),
                pltpu.SemaphoreType.DMA((3,)),
            ],
        ),
        out_shape=jax.ShapeDtypeStruct((p, dw), jnp.uint32),
        compiler_params=_params("arbitrary"),
        name="moe_ffn",
    )(block_e, first, next_e, n_used, xb, wg, wu, wd)


def _combine_kernel(d0_ref, d1_ref, x_ref, meta_ref, yb_hbm, o_ref, a_ref, b_ref, sem, *, row0):
    base = row0 + pl.program_id(0) * x_ref.shape[0]
    _gather_rows(d0_ref, base, yb_hbm, a_ref, sem.at[0])
    _gather_rows(d1_ref, base, yb_hbm, b_ref, sem.at[1])
    a_lo, a_hi = _unpack_bf16_pairs(a_ref[...])
    b_lo, b_hi = _unpack_bf16_pairs(b_ref[...])
    g0 = meta_ref[:, META_G0:META_G0 + 1]
    g1 = meta_ref[:, META_G1:META_G1 + 1]
    half = a_lo.shape[1]
    o_ref[:, :half] = x_ref[:, :half] + (g0 * a_lo + g1 * b_lo)
    o_ref[:, half:] = x_ref[:, half:] + (g0 * a_hi + g1 * b_hi)


def moe_combine(x, meta, yb, d0, d1, row0, rows, tm=256):
    d = x.shape[1]
    assert row0 % tm == 0
    blk0 = row0 // tm
    return pl.pallas_call(
        functools.partial(_combine_kernel, row0=row0),
        grid_spec=pltpu.PrefetchScalarGridSpec(
            num_scalar_prefetch=2,
            grid=(rows // tm,),
            in_specs=[
                pl.BlockSpec((tm, d), lambda i, a, b: (blk0 + i, 0)),
                pl.BlockSpec((tm, ROUTER_LANES), lambda i, a, b: (blk0 + i, 0)),
                pl.BlockSpec(memory_space=pl.ANY),
            ],
            out_specs=pl.BlockSpec((tm, d), lambda i, a, b: (i, 0)),
            scratch_shapes=[pltpu.VMEM((tm, d // 2), jnp.uint32), pltpu.VMEM((tm, d // 2), jnp.uint32),
                            pltpu.SemaphoreType.DMA((2,))],
        ),
        out_shape=jax.ShapeDtypeStruct((rows, d), F32),
        compiler_params=_params("arbitrary"),
        name="moe_combine",
    )(d0, d1, x, meta, yb)


def moe_layout(meta, tot, tile, tm=MOE_TM):
    n = meta.shape[0]
    m = n * MOE_TOPK
    ids = jnp.arange(MOE_EXPERTS, dtype=jnp.int32)
    cnt = tot[:, 0, MOE_GROUPS:MOE_GROUPS + MOE_EXPERTS].astype(jnp.int32)
    counts = jnp.sum(cnt, axis=0)
    padded = (counts + tm - 1) // tm * tm
    pad_end = jnp.cumsum(padded)
    pad_start = pad_end - padded
    base = pad_start[None, :] + jnp.cumsum(cnt, axis=0) - cnt
    base_tok = jnp.repeat(base, tile, axis=0)

    def slots(e_col, r_col):
        e = meta[:, e_col].astype(jnp.int32)
        return jnp.sum(jnp.where(e[:, None] == ids[None, :], base_tok, 0), axis=1) + meta[:, r_col].astype(jnp.int32)

    d0 = slots(META_E0, META_R0)
    d1 = slots(META_E1, META_R1)
    n_blocks = (m + MOE_EXPERTS * (tm - 1) + tm - 1) // tm

    blk = jnp.arange(n_blocks, dtype=jnp.int32)
    n_used = (pad_end[-1] // tm).astype(jnp.int32)
    block_e = jnp.sum((pad_end[None, :] <= (blk * tm)[:, None]).astype(jnp.int32), axis=1)
    block_e = jnp.minimum(block_e, MOE_EXPERTS - 1)
    live = blk < n_used
    first = (live & ((blk == 0) | (block_e != jnp.roll(block_e, 1)))).astype(jnp.int32)
    has_rows = counts > 0
    later = jnp.arange(MOE_EXPERTS)[None, :] > jnp.arange(MOE_EXPERTS)[:, None]
    nxt_of_e = jnp.min(jnp.where(later & has_rows[None, :], jnp.arange(MOE_EXPERTS)[None, :], MOE_EXPERTS), axis=1)
    nxt_of_e = jnp.where(nxt_of_e == MOE_EXPERTS, -1, nxt_of_e).astype(jnp.int32)
    next_e = jnp.sum(jnp.where(block_e[:, None] == ids[None, :], nxt_of_e[None, :], 0), axis=1)
    pad_first = (pad_start + counts).astype(jnp.int32)
    pad_cnt = (padded - counts).astype(jnp.int32)
    return d0, d1, pad_first, pad_cnt, n_blocks * tm, block_e, first, next_e, n_used.reshape(1)


def hier_moe_residual(x, g, wr_g, br_g, wr_e, br_e, w_gate, w_up, w_down, layer, out_segments=None, tile=256):
    n, d = x.shape
    pad = ROUTER_LANES - MOE_GROUPS - MOE_EXPERTS
    wr = jnp.concatenate([wr_g, wr_e, jnp.zeros((d, pad), F32)], axis=1)
    br = jnp.concatenate([br_g, br_e, jnp.zeros((pad,), F32)]).reshape(1, ROUTER_LANES)
    hp, meta, tot = moe_route(x, g, wr, br, tile)
    d0, d1, pad_first, pad_cnt, p, block_e, first, next_e, n_used = moe_layout(meta, tot, tile)
    xb = moe_dispatch(hp, d0, d1, pad_first, pad_cnt, n_used, p)
    yb = expert_ffn(xb, block_e, first, next_e, n_used, w_gate, w_up, w_down, layer)
    if out_segments is None:
        return moe_combine(x, meta, yb, d0, d1, 0, n)
    return [moe_combine(x, meta, yb, d0, d1, row0, rows) for row0, rows in out_segments]


def kernel(x_prompt, x_sample, norm_mix, norm_ffn, ev_w_in, ev_w_out, a_ln_g, a_ln_b, a_w_s, a_b_s, b_q_gain, b_k_gain, b_rpb, od_w_in, od_w_out, c_q_gain, c_k_gain, t5_table, moe_wr_g, moe_br_g, moe_wr_e, moe_br_e, moe_w_gate, moe_w_up, moe_w_down):
    d = x_prompt.shape[-1]
    segs = [(x_prompt.shape[0], x_prompt.shape[1]), (x_sample.shape[0], x_sample.shape[1])]
    xs = [x_prompt.reshape(-1, d), x_sample.reshape(-1, d)]
    depth = norm_mix.shape[0]
    for l in range(depth):
        i = l // 2
        h = rmsnorm(xs, norm_mix[l], BF16)
        if l % 2 == 0:
            aw = a_ln_g.shape[1]
            z = matmul([[h]], [(ev_w_in[i], 0)])
            a_out = mixer_a(z, a_ln_g[i], a_ln_b[i], a_w_s[i], a_b_s[i])
            tab = natten_bias_table(b_rpb[i])
            b_out, row0 = [], 0
            for batch, seq in segs:
                b_out.append(natten(z, 2 * aw // LANES, row0, batch, seq, b_q_gain[i], b_k_gain[i], tab))
                row0 += batch * seq
            x = matmul([[a_out], b_out], [(ev_w_out[i], 0), (ev_w_out[i], aw)], residuals=xs)
        else:
            qkv = matmul([[h]], [(od_w_in[i], 0)])
            tabs = [dilated_bias_table(t5_table, dil) for _, dil in C_PATTERNS]
            c_out, row0 = [], 0
            for batch, seq in segs:
                c_out.append(dilated_attention(qkv, row0, batch, seq, c_q_gain[i], c_k_gain[i], tabs))
                row0 += batch * seq
            x = matmul([c_out], [(od_w_out[i], 0)], residuals=xs)
        out_segments = None
        if l == depth - 1:
            out_segments = [(0, segs[0][0] * segs[0][1]), (segs[0][0] * segs[0][1], segs[1][0] * segs[1][1])]
        res = hier_moe_residual(x, norm_ffn[l], moe_wr_g[l], moe_br_g[l], moe_wr_e[l], moe_br_e[l],
                                moe_w_gate, moe_w_up, moe_w_down, l, out_segments)
        xs = res if out_segments is not None else [res]
    return (xs[0].reshape(x_prompt.shape), xs[1].reshape(x_sample.shape))
```

```python
import functools
import math

import jax
import jax.numpy as jnp
from jax import lax
from jax.experimental import pallas as pl
from jax.experimental.pallas import tpu as pltpu

F32 = jnp.float32
BF16 = jnp.bfloat16
EPS = 1e-6
NEG = -1e30

V7X_VMEM_LIMIT_BYTES = 56 * 1024 * 1024
LANES = 128

GRID_W = 64
CHUNK = 128
A_GROUPS = 8
NA_KH = 8
NA_KW = 16
B_DH = 64
C_DH = 128
C_PATTERNS = ((128, 1), (512, 4), (2048, 16))
C_RAD = 64
C_TQ = 128
C_TK = C_TQ + 2 * C_RAD
C_SUPER = C_TQ * 16
T5_BUCKETS = 32
T5_MAX_DIST = 1024
MOE_GROUPS = 4
MOE_PER_GROUP = 8
MOE_EXPERTS = MOE_GROUPS * MOE_PER_GROUP
MOE_TOPK = 2
MOE_TM = 256
ROUTER_LANES = 128


def _params(*sem):
    return pltpu.CompilerParams(dimension_semantics=sem, vmem_limit_bytes=V7X_VMEM_LIMIT_BYTES)


def _row_sources(arrays, tm, width, col_of):
    specs, spans, off = [], [], 0
    for a in arrays:
        assert a.shape[0] % tm == 0 and a.shape[0] >= tm
        nblk = a.shape[0] // tm
        specs.append(pl.BlockSpec(
            (tm, width), lambda *g, off=off, nblk=nblk: (jnp.clip(g[-1] - off, 0, nblk - 1), col_of(*g))))
        spans.append((off, nblk))
        off += nblk
    return specs, spans


def _active_rows(i, refs, spans):
    val = refs[-1][...]
    for ref, (off, nblk) in reversed(list(zip(refs[:-1], spans[:-1]))):
        val = jnp.where(i < off + nblk, ref[...], val)
    return val


def _rmsnorm_kernel(*refs, spans):
    x_refs, (g_ref, o_ref) = refs[:len(spans)], refs[len(spans):]
    x = _active_rows(pl.program_id(0), x_refs, spans)
    y = x * lax.rsqrt(jnp.mean(x * x, axis=-1, keepdims=True) + EPS)
    o_ref[...] = (y * g_ref[...]).astype(o_ref.dtype)


def rmsnorm(xs, g, out_dtype, tm=1024):
    d = xs[0].shape[1]
    n = sum(x.shape[0] for x in xs)
    specs, spans = _row_sources(xs, tm, d, lambda i: 0)
    return pl.pallas_call(
        functools.partial(_rmsnorm_kernel, spans=spans),
        grid=(n // tm,),
        in_specs=specs + [pl.BlockSpec((1, d), lambda i: (0, 0))],
        out_specs=pl.BlockSpec((tm, d), lambda i: (i, 0)),
        out_shape=jax.ShapeDtypeStruct((n, d), out_dtype),
        compiler_params=_params("arbitrary"),
        name="rmsnorm",
    )(*xs, g.reshape(1, d))


def _matmul_kernel(*refs, x_spans, res_spans, k_chunk):
    refs = list(refs)
    x_refs = [[refs.pop(0) for _ in spans] for spans in x_spans]
    w_refs = [refs.pop(0) for _ in x_spans]
    res_refs = [refs.pop(0) for _ in res_spans]
    o_ref, wb_refs = refs[0], refs[1:]
    i = pl.program_id(1)

    @pl.when(pl.program_id(1) == 0)
    def _():
        for w_ref, wb_ref in zip(w_refs, wb_refs):
            rows = math.gcd(k_chunk, w_ref.shape[0])

            def cast(c, carry, w_ref=w_ref, wb_ref=wb_ref, rows=rows):
                sl = pl.ds(pl.multiple_of(c * rows, rows), rows)
                wb_ref[sl, :] = w_ref[sl, :].astype(BF16)
                return carry
            lax.fori_loop(0, w_ref.shape[0] // rows, cast, 0)

    acc = None
    for pieces, spans, wb_ref in zip(x_refs, x_spans, wb_refs):
        part = jnp.dot(_active_rows(i, pieces, spans), wb_ref[...], preferred_element_type=F32)
        acc = part if acc is None else acc + part
    if res_spans:
        acc = _active_rows(i, res_refs, res_spans) + acc
    o_ref[...] = acc.astype(o_ref.dtype)


def _matmul_row_tile(xs, residuals, tn, candidates=(1024, 512, 256)):
    ks = [pieces[0].shape[1] for pieces in xs]
    weights = sum(2 * k * tn * 4 + k * tn * 2 for k in ks)
    for tm in candidates:
        rows_ok = all(p.shape[0] % tm == 0 for pieces in xs for p in pieces) and \
            all(r.shape[0] % tm == 0 for r in residuals)
        blocks = sum(2 * len(pieces) * tm * k * 2 for pieces, k in zip(xs, ks))
        blocks += (2 + 2 * len(residuals) + 1) * tm * tn * 4
        if rows_ok and weights + blocks <= 0.8 * V7X_VMEM_LIMIT_BYTES:
            return tm
    raise ValueError("no row tile fits")


def matmul(xs, ws, residuals=(), out_dtype=F32, tn=1024):
    n = sum(x.shape[0] for x in xs[0])
    m = ws[0][0].shape[1]
    tn = min(tn, m)
    tm = _matmul_row_tile(xs, residuals, tn)
    in_specs, x_spans, w_specs = [], [], []
    for pieces, (w, row) in zip(xs, ws):
        k = pieces[0].shape[1]
        assert row % k == 0
        specs, spans = _row_sources(pieces, tm, k, lambda j, i: 0)
        in_specs += specs
        x_spans.append(spans)
        w_specs.append(pl.BlockSpec((k, tn), lambda j, i, rb=row // k: (rb, j)))
    res_specs, res_spans = _row_sources(list(residuals), tm, tn, lambda j, i: j)
    kern = functools.partial(_matmul_kernel, x_spans=x_spans, res_spans=res_spans, k_chunk=256)
    return pl.pallas_call(
        kern,
        grid=(m // tn, n // tm),
        in_specs=in_specs + w_specs + res_specs,
        out_specs=pl.BlockSpec((tm, tn), lambda j, i: (i, j)),
        out_shape=jax.ShapeDtypeStruct((n, m), out_dtype),
        scratch_shapes=[pltpu.VMEM((pieces[0].shape[1], tn), BF16) for pieces in xs],
        compiler_params=_params("arbitrary", "arbitrary"),
        name="matmul",
    )(*[p for pieces in xs for p in pieces], *[w for w, _ in ws], *residuals)


def _mixer_a_kernel(u_ref, v_ref, lng_ref, lnb_ref, ws_ref, bs_ref, o_ref):
    tm = u_ref.shape[0]
    gd = u_ref.shape[1] // A_GROUPS
    for c in range(tm // CHUNK):
        rows = slice(c * CHUNK, (c + 1) * CHUNK)
        u = jax.nn.gelu(u_ref[rows, :])
        v = jax.nn.gelu(v_ref[rows, :])
        mu = jnp.mean(v, axis=-1, keepdims=True)
        vc = v - mu
        var = jnp.mean(vc * vc, axis=-1, keepdims=True)
        v = vc * lax.rsqrt(var + EPS) * lng_ref[...] + lnb_ref[...]
        for g in range(A_GROUPS):
            cols = slice(g * gd, (g + 1) * gd)
            f = jnp.dot(ws_ref[g].astype(BF16), v[:, cols].astype(BF16), preferred_element_type=F32)
            f = f + bs_ref[g]
            o_ref[rows, cols] = (u[:, cols] * f).astype(o_ref.dtype)


def mixer_a(z, ln_g, ln_b, w_s, b_s, tm=512):
    n = z.shape[0]
    aw = ln_g.shape[0]
    gd = aw // A_GROUPS
    bs_b = jnp.broadcast_to(b_s[:, :, None], (A_GROUPS, CHUNK, gd))
    return pl.pallas_call(
        _mixer_a_kernel,
        grid=(n // tm,),
        in_specs=[
            pl.BlockSpec((tm, aw), lambda i: (i, 0)),
            pl.BlockSpec((tm, aw), lambda i: (i, 1)),
            pl.BlockSpec((1, aw), lambda i: (0, 0)),
            pl.BlockSpec((1, aw), lambda i: (0, 0)),
            pl.BlockSpec((A_GROUPS, CHUNK, CHUNK), lambda i: (0, 0, 0)),
            pl.BlockSpec((A_GROUPS, CHUNK, gd), lambda i: (0, 0, 0)),
        ],
        out_specs=pl.BlockSpec((tm, aw), lambda i: (i, 0)),
        out_shape=jax.ShapeDtypeStruct((n, aw), BF16),
        compiler_params=_params("parallel"),
        name="mixer_a",
    )(z, z, ln_g.reshape(1, aw), ln_b.reshape(1, aw), w_s, bs_b)


def _skew(w, rows):
    n = w.shape[-1]
    flat = jnp.tile(w, (1,) * (w.ndim - 1) + (rows,))
    return flat[..., :rows * (n - 1)].reshape(w.shape[:-1] + (rows, n - 1))


def _pair_rmsnorm(x, gain):
    sq = x * x
    lane = lax.broadcasted_iota(jnp.int32, x.shape, 1)
    lo = lane < B_DH
    s_lo = jnp.sum(jnp.where(lo, sq, 0.0), axis=-1, keepdims=True)
    s_hi = jnp.sum(jnp.where(lo, 0.0, sq), axis=-1, keepdims=True)
    ms = jnp.where(lo, s_lo, s_hi) * (1.0 / B_DH)
    return x * lax.rsqrt(ms + EPS) * gain


def _natten_kernel(q_ref, k_ref, v_ref, gq_ref, gk_ref, bias_ref, o_ref, kn_ref, vb_ref, s_ref, m_ref, e_ref,
                   *, rows_per_step, rows):
    seq = k_ref.shape[0]
    t = pl.program_id(2)
    prep = 256

    @pl.when(t == 0)
    def _():
        def body(c, carry):
            sl = pl.ds(pl.multiple_of(c * prep, prep), prep)
            kn_ref[sl, :] = _pair_rmsnorm(k_ref[sl, :], gk_ref[...]).astype(BF16)
            vb_ref[sl, :LANES] = v_ref[sl, :].astype(BF16)
            vb_ref[sl, LANES:] = jnp.ones((prep, LANES), BF16)
            return carry
        lax.fori_loop(0, seq // prep, body, 0)

    q = (_pair_rmsnorm(q_ref[...], gq_ref[...]) * (B_DH ** -0.5)).astype(BF16)
    lo = lax.broadcasted_iota(jnp.int32, (GRID_W, LANES), 1) < B_DH
    zero = jnp.zeros((GRID_W, LANES), BF16)
    win = NA_KH * GRID_W

    def window(rr):
        r = t * rows_per_step + rr
        rs = jnp.clip(r - NA_KH // 2, 0, rows - NA_KH)
        return r - rs, pl.ds(pl.multiple_of(rs * GRID_W, GRID_W), win)

    for rr in range(rows_per_step):
        case, ksl = window(rr)
        qr = q[rr * GRID_W:(rr + 1) * GRID_W, :]
        q2 = jnp.concatenate([jnp.where(lo, qr, zero), jnp.where(lo, zero, qr)], axis=0)
        s = lax.dot_general(q2, kn_ref[ksl, :], (((1,), (1,)), ((), ())), preferred_element_type=F32)
        s_ref[rr] = s + bias_ref[0, case]
    for rr in range(rows_per_step):
        m = jnp.max(s_ref[rr], axis=-1, keepdims=True)
        m_ref[rr] = jnp.broadcast_to(m, (2 * GRID_W, LANES))
    for rr in range(rows_per_step):
        mb = m_ref[rr]
        for c in range(win // LANES):
            cols = slice(c * LANES, (c + 1) * LANES)
            e_ref[rr, :, cols] = jnp.exp(s_ref[rr, :, cols] - mb).astype(BF16)
    for rr in range(rows_per_step):
        _, ksl = window(rr)
        ol = jnp.dot(e_ref[rr], vb_ref[ksl, :], preferred_element_type=F32)
        o = ol[:, :LANES] / ol[:, LANES:]
        o_ref[rr * GRID_W:(rr + 1) * GRID_W, :] = jnp.where(lo, o[:GRID_W], o[GRID_W:]).astype(o_ref.dtype)


def natten_bias_table(rpb):
    heads = rpb.shape[0]
    col = jnp.arange(GRID_W)
    col_start = jnp.clip(col - NA_KW // 2, 0, GRID_W - NA_KW)
    col_valid = (col[None, :] >= col_start[:, None]) & (col[None, :] < col_start[:, None] + NA_KW)
    edge = GRID_W - NA_KW
    w = jnp.concatenate([jnp.repeat(rpb[..., :1], edge, axis=-1), rpb.astype(F32),
                         jnp.repeat(rpb[..., -1:], edge, axis=-1), jnp.zeros(rpb.shape[:-1] + (1,), F32)], axis=-1)
    bias_c = _skew(w, GRID_W)[..., GRID_W - 1:]
    tab = jnp.stack([bias_c[:, NA_KH - 1 - c:2 * NA_KH - 1 - c] for c in range(NA_KH)], axis=1)
    tab = jnp.where(col_valid[:, None, :], tab.transpose(0, 1, 3, 2, 4), NEG)
    tab = tab.reshape(heads // 2, 2, NA_KH, GRID_W, NA_KH * GRID_W).transpose(0, 2, 1, 3, 4)
    return tab.reshape(heads // 2, NA_KH, 2 * GRID_W, NA_KH * GRID_W)


def natten(z, col0, row0, batch, seq, gq, gk, bias_tab, rows_per_step=16):
    hp = bias_tab.shape[0]
    rows = seq // GRID_W
    assert rows >= NA_KH and rows % rows_per_step == 0 and row0 % seq == 0
    tq = rows_per_step * GRID_W
    steps = rows // rows_per_step
    qb0 = row0 // tq
    sb0 = row0 // seq
    win = NA_KH * GRID_W
    kern = functools.partial(_natten_kernel, rows_per_step=rows_per_step, rows=rows)
    gq2 = jnp.tile(gq, 2).reshape(1, 2 * B_DH)
    gk2 = jnp.tile(gk, 2).reshape(1, 2 * B_DH)
    return pl.pallas_call(
        kern,
        grid=(batch, hp, steps),
        in_specs=[
            pl.BlockSpec((tq, LANES), lambda b, h, t: (qb0 + b * steps + t, col0 + h)),
            pl.BlockSpec((seq, LANES), lambda b, h, t: (sb0 + b, col0 + hp + h)),
            pl.BlockSpec((seq, LANES), lambda b, h, t: (sb0 + b, col0 + 2 * hp + h)),
            pl.BlockSpec((1, LANES), lambda b, h, t: (0, 0)),
            pl.BlockSpec((1, LANES), lambda b, h, t: (0, 0)),
            pl.BlockSpec((1, NA_KH, 2 * GRID_W, win), lambda b, h, t: (h, 0, 0, 0)),
        ],
        out_specs=pl.BlockSpec((tq, LANES), lambda b, h, t: (b * steps + t, h)),
        out_shape=jax.ShapeDtypeStruct((batch * seq, 2 * hp * B_DH), BF16),
        scratch_shapes=[
            pltpu.VMEM((seq, LANES), BF16),
            pltpu.VMEM((seq, 2 * LANES), BF16),
            pltpu.VMEM((rows_per_step, 2 * GRID_W, win), F32),
            pltpu.VMEM((rows_per_step, 2 * GRID_W, LANES), F32),
            pltpu.VMEM((rows_per_step, 2 * GRID_W, win), BF16),
        ],
        compiler_params=_params("parallel", "parallel", "arbitrary"),
        name="natten",
    )(z, z, z, gq2, gk2, bias_tab)


def t5_bucket(rel):
    nb = T5_BUCKETS // 2
    max_exact = nb // 2
    ret = jnp.where(rel > 0, nb, 0)
    n = jnp.abs(rel)
    large = max_exact + (jnp.log(jnp.maximum(n, 1).astype(F32) / max_exact)
                         / math.log(T5_MAX_DIST / max_exact) * (nb - max_exact)).astype(jnp.int32)
    large = jnp.minimum(large, nb - 1)
    return ret + jnp.where(n < max_exact, n, large)


def dilated_bias_table(t5_table, dil):
    span = C_TK - 1
    rel = jnp.arange(-span, span + 1)
    vec = t5_table.astype(F32)[t5_bucket(rel * dil)]
    vec = jnp.where((jnp.abs(rel) <= C_RAD)[:, None], vec, NEG).T
    w = jnp.concatenate([vec, jnp.zeros((vec.shape[0], 1), F32)], axis=1)
    r = _skew(w, C_TQ)
    return jnp.stack([r[:, :, span - off:span - off + C_TK] for off in (0, C_RAD, 2 * C_RAD)], axis=1)


def _row_rmsnorm(x, gain):
    return x * lax.rsqrt(jnp.mean(x * x, axis=-1, keepdims=True) + EPS) * gain


def _dilated_kernel(q_ref, k_ref, v_ref, gq_ref, gk_ref, b0_ref, b1_ref, b2_ref, o_ref,
                    kn_ref, qn_ref, oacc_ref, lse_ref, s_ref, m_ref, e_ref, *kv_by_residue, seq, unroll):
    t = pl.program_id(2)
    prep = 256
    kv_sub = {dil: (kv_by_residue[2 * i], kv_by_residue[2 * i + 1])
              for i, dil in enumerate(d for _, d in C_PATTERNS if d > 1)}

    @pl.when(t == 0)
    def _():
        def body(c, carry):
            sl = pl.ds(pl.multiple_of(c * prep, prep), prep)
            kn_ref[sl, :] = _row_rmsnorm(k_ref[sl, :], gk_ref[...])
            return carry
        lax.fori_loop(0, seq // prep, body, 0)

        for dil, (ks_ref, vs_ref) in kv_sub.items():
            rows = min(prep, seq // dil)
            for s in range(dil):
                def split(c, carry, dil=dil, s=s, rows=rows, ks_ref=ks_ref, vs_ref=vs_ref):
                    src = pl.ds(c * rows * dil + s, rows, stride=dil)
                    dst = pl.ds(pl.multiple_of(c * rows, rows), rows)
                    ks_ref[s, dst, :] = kn_ref[src, :].astype(BF16)
                    vs_ref[s, dst, :] = v_ref[src, :].astype(BF16)
                    return carry
                lax.fori_loop(0, seq // (dil * rows), split, 0)

    qn_ref[...] = _row_rmsnorm(q_ref[...], gq_ref[...]) * (C_DH ** -0.5)
    n_tiles = C_SUPER // C_TQ

    for p, ((window, dil), b_ref) in enumerate(zip(C_PATTERNS, (b0_ref, b1_ref, b2_ref))):
        n_sub = seq // dil
        tk = min(C_TK, n_sub)
        span = C_TQ * dil
        ones = jnp.ones((tk, LANES), BF16)

        def tile(idx, dil=dil, n_sub=n_sub, tk=tk, span=span):
            u = idx // dil
            s = idx % dil
            qpos = u * span + s
            j0 = t * (C_SUPER // dil) + u * C_TQ
            ws = jnp.clip(j0 - C_RAD, 0, n_sub - tk)
            ksl = pl.ds(pl.multiple_of(ws, C_RAD), tk)
            if dil == 1:
                return pl.ds(pl.multiple_of(qpos, C_TQ), C_TQ), s, ksl, (j0 - ws) // C_RAD
            return pl.ds(qpos, C_TQ, stride=dil), s, ksl, (j0 - ws) // C_RAD

        def keys(s, ksl, dil=dil):
            return kn_ref[ksl, :].astype(BF16) if dil == 1 else kv_sub[dil][0][s, ksl, :]

        def values(s, ksl, dil=dil):
            return v_ref[ksl, :].astype(BF16) if dil == 1 else kv_sub[dil][1][s, ksl, :]

        def scores(idx, carry, tile=tile, keys=keys, tk=tk, b_ref=b_ref):
            qsl, s, ksl, case = tile(idx)
            q = qn_ref[qsl, :].astype(BF16)
            sc = lax.dot_general(q, keys(s, ksl), (((1,), (1,)), ((), ())), preferred_element_type=F32)
            s_ref[idx, :, :tk] = sc + b_ref[0, case, :, :tk]
            return carry

        def row_max(idx, carry, tk=tk):
            m = jnp.max(s_ref[idx, :, :tk], axis=-1, keepdims=True)
            m_ref[idx] = jnp.broadcast_to(m, (C_TQ, LANES))
            return carry

        def exps(idx, carry, tk=tk):
            mb = m_ref[idx]
            for c in range(tk // LANES):
                cols = slice(c * LANES, (c + 1) * LANES)
                e_ref[idx, :, cols] = jnp.exp(s_ref[idx, :, cols] - mb).astype(BF16)
            return carry

        def outputs(idx, carry, tile=tile, values=values, tk=tk, ones=ones, p=p):
            qsl, s, ksl, _ = tile(idx)
            v1 = jnp.concatenate([values(s, ksl), ones], axis=1)
            ol = jnp.dot(e_ref[idx, :, :tk], v1, preferred_element_type=F32)
            l = ol[:, LANES:]
            oacc_ref[p, qsl, :] = ol[:, :LANES] / l
            lse_ref[p, qsl, :] = m_ref[idx] + jnp.log(l)
            return carry

        for phase in (scores, row_max, exps, outputs):
            lax.fori_loop(0, n_tiles, phase, 0, unroll=unroll)

    l0, l1, l2 = lse_ref[0], lse_ref[1], lse_ref[2]
    mx = jnp.maximum(jnp.maximum(l0, l1), l2)
    w0, w1, w2 = jnp.exp(l0 - mx), jnp.exp(l1 - mx), jnp.exp(l2 - mx)
    num = w0 * oacc_ref[0] + w1 * oacc_ref[1] + w2 * oacc_ref[2]
    o_ref[...] = (num / (w0 + w1 + w2)).astype(o_ref.dtype)


def dilated_attention(qkv, row0, batch, seq, gq, gk, tabs, unroll=8):
    heads = tabs[0].shape[0]
    assert seq % C_SUPER == 0 and row0 % seq == 0
    steps = seq // C_SUPER
    qb0 = row0 // C_SUPER
    sb0 = row0 // seq
    n_tiles = C_SUPER // C_TQ
    kern = functools.partial(_dilated_kernel, seq=seq, unroll=unroll)
    tab_specs = [pl.BlockSpec((1,) + tuple(tb.shape[1:]), lambda b, h, t: (h, 0, 0, 0)) for tb in tabs]
    return pl.pallas_call(
        kern,
        grid=(batch, heads, steps),
        in_specs=[
            pl.BlockSpec((C_SUPER, LANES), lambda b, h, t: (qb0 + b * steps + t, h)),
            pl.BlockSpec((seq, LANES), lambda b, h, t: (sb0 + b, heads + h)),
            pl.BlockSpec((seq, LANES), lambda b, h, t: (sb0 + b, 2 * heads + h)),
            pl.BlockSpec((1, LANES), lambda b, h, t: (0, 0)),
            pl.BlockSpec((1, LANES), lambda b, h, t: (0, 0)),
        ] + tab_specs,
        out_specs=pl.BlockSpec((C_SUPER, LANES), lambda b, h, t: (b * steps + t, h)),
        out_shape=jax.ShapeDtypeStruct((batch * seq, heads * C_DH), BF16),
        scratch_shapes=[
            pltpu.VMEM((seq, LANES), F32),
            pltpu.VMEM((C_SUPER, LANES), F32),
            pltpu.VMEM((len(C_PATTERNS), C_SUPER, LANES), F32),
            pltpu.VMEM((len(C_PATTERNS), C_SUPER, LANES), F32),
            pltpu.VMEM((n_tiles, C_TQ, C_TK), F32),
            pltpu.VMEM((n_tiles, C_TQ, LANES), F32),
            pltpu.VMEM((n_tiles, C_TQ, C_TK), BF16),
        ] + [pltpu.VMEM((dil, seq // dil, LANES), BF16) for _, dil in C_PATTERNS if dil > 1 for _ in "kv"],
        compiler_params=_params("parallel", "parallel", "arbitrary"),
        name="dilated",
    )(qkv, qkv, qkv, gq.reshape(1, C_DH), gk.reshape(1, C_DH), *tabs)


def _pack_bf16_pairs(lo, hi):
    lo_bits = lax.bitcast_convert_type(lo.astype(BF16).astype(F32), jnp.uint32)
    hi_bits = lax.bitcast_convert_type(hi.astype(BF16).astype(F32), jnp.uint32)
    return (hi_bits & jnp.uint32(0xFFFF0000)) | (lo_bits >> 16)


def _unpack_bf16_pairs(packed):
    lo = lax.bitcast_convert_type(packed << 16, F32)
    hi = lax.bitcast_convert_type(packed & jnp.uint32(0xFFFF0000), F32)
    return lo, hi


META_E0, META_E1, META_R0, META_R1, META_G0, META_G1 = range(6)


def _route_kernel(x_ref, g_ref, wr_ref, br_ref, hp_ref, meta_ref, tot_ref):
    x = x_ref[...]
    h = x * lax.rsqrt(jnp.mean(x * x, axis=-1, keepdims=True) + EPS) * g_ref[...]
    half = h.shape[1] // 2
    hp_ref[...] = _pack_bf16_pairs(h[:, :half], h[:, half:])
    h_hi = h.astype(BF16)
    h_lo = (h - h_hi.astype(F32)).astype(BF16)
    w_hi = wr_ref[...].astype(BF16)
    w_lo = (wr_ref[...] - w_hi.astype(F32)).astype(BF16)
    logits = (jnp.dot(h_hi, w_hi, preferred_element_type=F32) + jnp.dot(h_hi, w_lo, preferred_element_type=F32)
              + jnp.dot(h_lo, w_hi, preferred_element_type=F32)) + br_ref[...]
    tm = x.shape[0]
    lane = lax.broadcasted_iota(jnp.int32, (tm, ROUTER_LANES), 1)
    lane_f = lane.astype(F32)

    def first_max(v):
        mx = jnp.max(v, axis=-1, keepdims=True)
        return mx, jnp.min(jnp.where(v == mx, lane_f, float(ROUTER_LANES)), axis=-1, keepdims=True)

    lg = jnp.where(lane < MOE_GROUPS, logits, -jnp.inf)
    gmax, gsel = first_max(lg)
    p_g = 1.0 / jnp.sum(jnp.exp(lg - gmax), axis=-1, keepdims=True)
    expert_lane = (lane >= MOE_GROUPS) & (lane < MOE_GROUPS + MOE_EXPERTS)
    in_group = expert_lane & (((lane - MOE_GROUPS) // MOE_PER_GROUP) == gsel.astype(jnp.int32))
    le = jnp.where(in_group, logits, -jnp.inf)
    v0, i0 = first_max(le)
    pick0 = lane_f == i0
    v1, i1 = first_max(jnp.where(pick0, -jnp.inf, le))
    pick1 = lane_f == i1
    t = jnp.exp(v1 - v0)
    g0 = p_g / (1.0 + t)
    g1 = p_g * t / (1.0 + t)

    onehot = jnp.where(pick0 | pick1, 1.0, 0.0).astype(BF16)
    row = lax.broadcasted_iota(jnp.int32, (tm, tm), 0)
    col = lax.broadcasted_iota(jnp.int32, (tm, tm), 1)
    earlier = jnp.where(col < row, 1.0, 0.0).astype(BF16)
    before = jnp.dot(earlier, onehot, preferred_element_type=F32)
    r0 = jnp.sum(jnp.where(pick0, before, 0.0), axis=-1, keepdims=True)
    r1 = jnp.sum(jnp.where(pick1, before, 0.0), axis=-1, keepdims=True)

    meta = jnp.zeros((tm, ROUTER_LANES), F32)
    for pos, val in ((META_E0, i0 - MOE_GROUPS), (META_E1, i1 - MOE_GROUPS), (META_R0, r0), (META_R1, r1),
                     (META_G0, g0), (META_G1, g1)):
        meta = jnp.where(lane == pos, val, meta)
    meta_ref[...] = meta
    tot_ref[0] = jnp.sum(onehot.astype(F32), axis=0, keepdims=True)


def moe_route(x, g, wr, br, tm):
    n, d = x.shape
    return pl.pallas_call(
        _route_kernel,
        grid=(n // tm,),
        in_specs=[
            pl.BlockSpec((tm, d), lambda i: (i, 0)),
            pl.BlockSpec((1, d), lambda i: (0, 0)),
            pl.BlockSpec((d, ROUTER_LANES), lambda i: (0, 0)),
            pl.BlockSpec((1, ROUTER_LANES), lambda i: (0, 0)),
        ],
        out_specs=[
            pl.BlockSpec((tm, d // 2), lambda i: (i, 0)),
            pl.BlockSpec((tm, ROUTER_LANES), lambda i: (i, 0)),
            pl.BlockSpec((1, 1, ROUTER_LANES), lambda i: (i, 0, 0)),
        ],
        out_shape=[
            jax.ShapeDtypeStruct((n, d // 2), jnp.uint32),
            jax.ShapeDtypeStruct((n, ROUTER_LANES), F32),
            jax.ShapeDtypeStruct((n // tm, 1, ROUTER_LANES), F32),
        ],
        compiler_params=_params("parallel"),
        name="moe_route",
    )(x, g.reshape(1, d), wr, br)


def _dispatch_kernel(d0_ref, d1_ref, pad_first_ref, pad_cnt_ref, nused_ref, h_ref, xb_hbm, sem):
    i = pl.program_id(0)
    tm = h_ref.shape[0]
    base = i * tm

    def row_copy(src_row, dst_row):
        return pltpu.make_async_copy(h_ref.at[pl.ds(src_row, 1), :], xb_hbm.at[pl.ds(dst_row, 1), :], sem.at[0])

    for r in range(tm):
        row_copy(r, d0_ref[base + r]).start()
        row_copy(r, d1_ref[base + r]).start()

    def drain(r, carry):
        row_copy(0, 0).wait()
        row_copy(0, 0).wait()
        return carry
    lax.fori_loop(0, tm, drain, 0, unroll=8)

    @pl.when(i == pl.num_programs(0) - 1)
    def _():
        def per_expert(e, carry):
            def fill(j, c):
                row_copy(0, pad_first_ref[e] + j).start()
                return c
            lax.fori_loop(0, pad_cnt_ref[e], fill, 0)

            def fill_wait(j, c):
                row_copy(0, 0).wait()
                return c
            lax.fori_loop(0, pad_cnt_ref[e], fill_wait, 0)
            return carry
        lax.fori_loop(0, MOE_EXPERTS, per_expert, 0)

        def block_copy(b):
            return pltpu.make_async_copy(h_ref, xb_hbm.at[pl.ds(pl.multiple_of(b * tm, tm), tm), :], sem.at[0])

        def fill_block(b, c):
            block_copy(b).start()
            return c
        lax.fori_loop(nused_ref[0], xb_hbm.shape[0] // tm, fill_block, 0)

        def fill_block_wait(b, c):
            block_copy(b).wait()
            return c
        lax.fori_loop(nused_ref[0], xb_hbm.shape[0] // tm, fill_block_wait, 0)


def moe_dispatch(hp, d0, d1, pad_first, pad_cnt, n_used, p, tm=MOE_TM):
    n, dw = hp.shape
    return pl.pallas_call(
        _dispatch_kernel,
        grid_spec=pltpu.PrefetchScalarGridSpec(
            num_scalar_prefetch=5,
            grid=(n // tm,),
            in_specs=[pl.BlockSpec((tm, dw), lambda i, *_: (i, 0))],
            out_specs=pl.BlockSpec(memory_space=pl.ANY),
            scratch_shapes=[pltpu.SemaphoreType.DMA((1,))],
        ),
        out_shape=jax.ShapeDtypeStruct((p, dw), hp.dtype),
        compiler_params=_params("arbitrary"),
        name="moe_dispatch",
    )(d0, d1, pad_first, pad_cnt, n_used, hp)


def _row_gather_copy(src_hbm, dst_ref, sem, src_row, dst_row):
    return pltpu.make_async_copy(src_hbm.at[pl.ds(src_row, 1), :], dst_ref.at[pl.ds(dst_row, 1), :], sem)


def _gather_rows(idx_ref, base, src_hbm, dst_ref, sem, unroll=8):
    rows = dst_ref.shape[0]
    for r in range(rows):
        _row_gather_copy(src_hbm, dst_ref, sem, idx_ref[base + r], r).start()

    def drain(r, carry):
        _row_gather_copy(src_hbm, dst_ref, sem, 0, r).wait()
        return carry
    lax.fori_loop(0, rows, drain, 0, unroll=unroll)


def _expert_weight_copies(layer, e, w_hbm, stage, sem):
    return [pltpu.make_async_copy(w.at[layer, e], s, sem.at[k]) for k, (w, s) in enumerate(zip(w_hbm, stage))]


def _ffn_kernel(be_ref, first_ref, next_ref, nused_ref, x_ref, wg_hbm, wu_hbm, wd_hbm, o_ref,
                sg_ref, su_ref, sd_ref, wg_ref, wu_ref, wd_ref, sem, *, layer, k_chunk):
    i = pl.program_id(0)
    w_hbm = (wg_hbm, wu_hbm, wd_hbm)
    stage = (sg_ref, su_ref, sd_ref)
    resident = (wg_ref, wu_ref, wd_ref)

    @pl.when(i == 0)
    def _():
        for cp in _expert_weight_copies(layer, be_ref[0], w_hbm, stage, sem):
            cp.start()

    @pl.when(first_ref[i] == 1)
    def _():
        for cp in _expert_weight_copies(layer, be_ref[i], w_hbm, stage, sem):
            cp.wait()
        for s_ref, w_ref in zip(stage, resident):
            rows = math.gcd(k_chunk, s_ref.shape[0])

            def cast(c, carry, s_ref=s_ref, w_ref=w_ref, rows=rows):
                sl = pl.ds(pl.multiple_of(c * rows, rows), rows)
                w_ref[sl, :] = s_ref[sl, :].astype(BF16)
                return carry
            lax.fori_loop(0, s_ref.shape[0] // rows, cast, 0)

        @pl.when(next_ref[i] >= 0)
        def _():
            for cp in _expert_weight_copies(layer, next_ref[i], w_hbm, stage, sem):
                cp.start()

    live = i < nused_ref[0]

    @pl.when(live)
    def _():
        lo, hi = _unpack_bf16_pairs(x_ref[...])
        lo, hi = lo.astype(BF16), hi.astype(BF16)
        half = lo.shape[1]
        g = (jnp.dot(lo, wg_ref[:half, :], preferred_element_type=F32)
             + jnp.dot(hi, wg_ref[half:, :], preferred_element_type=F32))
        u = (jnp.dot(lo, wu_ref[:half, :], preferred_element_type=F32)
             + jnp.dot(hi, wu_ref[half:, :], preferred_element_type=F32))
        h = (jax.nn.silu(g) * u).astype(BF16)
        y = jnp.dot(h, wd_ref[...], preferred_element_type=F32)
        o_ref[...] = _pack_bf16_pairs(y[:, :half], y[:, half:])

    @pl.when(jnp.logical_not(live))
    def _():
        o_ref[...] = jnp.zeros_like(o_ref)


def expert_ffn(xb, block_e, first, next_e, n_used, wg, wu, wd, layer, tm=MOE_TM):
    p, dw = xb.shape
    d, de = wg.shape[2], wg.shape[3]
    any_spec = pl.BlockSpec(memory_space=pl.ANY)
    return pl.pallas_call(
        functools.partial(_ffn_kernel, layer=layer, k_chunk=256),
        grid_spec=pltpu.PrefetchScalarGridSpec(
            num_scalar_prefetch=4,
            grid=(p // tm,),
            in_specs=[
                pl.BlockSpec((tm, dw), lambda i, be, fi, ne, nu: (jnp.minimum(i, nu[0] - 1), 0)),
                any_spec, any_spec, any_spec,
            ],
            out_specs=pl.BlockSpec((tm, dw), lambda i, *_: (i, 0)),
            scratch_shapes=[
                pltpu.VMEM((d, de), F32), pltpu.VMEM((d, de), F32), pltpu.VMEM((de, d), F32),
                pltpu.VMEM((d, de), BF16), pltpu.VMEM((d, de), BF16), pltpu.VMEM((de, d), BF16),
                pltpu.SemaphoreType.DMA((3,)),
            ],
        ),
        out_shape=jax.ShapeDtypeStruct((p, dw), jnp.uint32),
        compiler_params=_params("arbitrary"),
        name="moe_ffn",
    )(block_e, first, next_e, n_used, xb, wg, wu, wd)


def _combine_kernel(d0_ref, d1_ref, x_ref, meta_ref, yb_hbm, *rest, row0, with_norm):
    if with_norm:
        gain_ref, o_ref, h_ref, a_ref, b_ref, sem = rest
    else:
        o_ref, a_ref, b_ref, sem = rest
    base = row0 + pl.program_id(0) * x_ref.shape[0]
    _gather_rows(d0_ref, base, yb_hbm, a_ref, sem.at[0])
    _gather_rows(d1_ref, base, yb_hbm, b_ref, sem.at[1])
    a_lo, a_hi = _unpack_bf16_pairs(a_ref[...])
    b_lo, b_hi = _unpack_bf16_pairs(b_ref[...])
    g0 = meta_ref[:, META_G0:META_G0 + 1]
    g1 = meta_ref[:, META_G1:META_G1 + 1]
    half = a_lo.shape[1]
    lo = x_ref[:, :half] + (g0 * a_lo + g1 * b_lo)
    hi = x_ref[:, half:] + (g0 * a_hi + g1 * b_hi)
    o_ref[:, :half] = lo
    o_ref[:, half:] = hi
    if with_norm:
        ms = (jnp.sum(lo * lo, axis=-1, keepdims=True) + jnp.sum(hi * hi, axis=-1, keepdims=True)) / (2 * half)
        inv = lax.rsqrt(ms + EPS)
        h_ref[:, :half] = (lo * inv * gain_ref[:, :half]).astype(h_ref.dtype)
        h_ref[:, half:] = (hi * inv * gain_ref[:, half:]).astype(h_ref.dtype)


def moe_combine(x, meta, yb, d0, d1, row0, rows, next_gain=None, tm=256):
    d = x.shape[1]
    assert row0 % tm == 0
    blk0 = row0 // tm
    with_norm = next_gain is not None
    row_spec = pl.BlockSpec((tm, d), lambda i, a, b: (i, 0))
    in_specs = [
        pl.BlockSpec((tm, d), lambda i, a, b: (blk0 + i, 0)),
        pl.BlockSpec((tm, ROUTER_LANES), lambda i, a, b: (blk0 + i, 0)),
        pl.BlockSpec(memory_space=pl.ANY),
    ]
    args = [d0, d1, x, meta, yb]
    out_specs, out_shape = row_spec, jax.ShapeDtypeStruct((rows, d), F32)
    if with_norm:
        in_specs.append(pl.BlockSpec((1, d), lambda i, a, b: (0, 0)))
        args.append(next_gain.reshape(1, d))
        out_specs, out_shape = [row_spec, row_spec], [out_shape, jax.ShapeDtypeStruct((rows, d), BF16)]
    return pl.pallas_call(
        functools.partial(_combine_kernel, row0=row0, with_norm=with_norm),
        grid_spec=pltpu.PrefetchScalarGridSpec(
            num_scalar_prefetch=2,
            grid=(rows // tm,),
            in_specs=in_specs,
            out_specs=out_specs,
            scratch_shapes=[pltpu.VMEM((tm, d // 2), jnp.uint32), pltpu.VMEM((tm, d // 2), jnp.uint32),
                            pltpu.SemaphoreType.DMA((2,))],
        ),
        out_shape=out_shape,
        compiler_params=_params("arbitrary"),
        name="moe_combine",
    )(*args)


def moe_layout(meta, tot, tile, tm=MOE_TM):
    n = meta.shape[0]
    m = n * MOE_TOPK
    ids = jnp.arange(MOE_EXPERTS, dtype=jnp.int32)
    cnt = tot[:, 0, MOE_GROUPS:MOE_GROUPS + MOE_EXPERTS].astype(jnp.int32)
    counts = jnp.sum(cnt, axis=0)
    padded = (counts + tm - 1) // tm * tm
    pad_end = jnp.cumsum(padded)
    pad_start = pad_end - padded
    base = pad_start[None, :] + jnp.cumsum(cnt, axis=0) - cnt
    base_tok = jnp.repeat(base, tile, axis=0)

    def slots(e_col, r_col):
        e = meta[:, e_col].astype(jnp.int32)
        return jnp.sum(jnp.where(e[:, None] == ids[None, :], base_tok, 0), axis=1) + meta[:, r_col].astype(jnp.int32)

    d0 = slots(META_E0, META_R0)
    d1 = slots(META_E1, META_R1)
    n_blocks = (m + MOE_EXPERTS * (tm - 1) + tm - 1) // tm

    blk = jnp.arange(n_blocks, dtype=jnp.int32)
    n_used = (pad_end[-1] // tm).astype(jnp.int32)
    block_e = jnp.sum((pad_end[None, :] <= (blk * tm)[:, None]).astype(jnp.int32), axis=1)
    block_e = jnp.minimum(block_e, MOE_EXPERTS - 1)
    live = blk < n_used
    first = (live & ((blk == 0) | (block_e != jnp.roll(block_e, 1)))).astype(jnp.int32)
    has_rows = counts > 0
    later = jnp.arange(MOE_EXPERTS)[None, :] > jnp.arange(MOE_EXPERTS)[:, None]
    nxt_of_e = jnp.min(jnp.where(later & has_rows[None, :], jnp.arange(MOE_EXPERTS)[None, :], MOE_EXPERTS), axis=1)
    nxt_of_e = jnp.where(nxt_of_e == MOE_EXPERTS, -1, nxt_of_e).astype(jnp.int32)
    next_e = jnp.sum(jnp.where(block_e[:, None] == ids[None, :], nxt_of_e[None, :], 0), axis=1)
    pad_first = (pad_start + counts).astype(jnp.int32)
    pad_cnt = (padded - counts).astype(jnp.int32)
    return d0, d1, pad_first, pad_cnt, n_blocks * tm, block_e, first, next_e, n_used.reshape(1)


def hier_moe_residual(x, g, wr_g, br_g, wr_e, br_e, w_gate, w_up, w_down, layer, out_segments=None,
                      next_gain=None, tile=512):
    n, d = x.shape
    pad = ROUTER_LANES - MOE_GROUPS - MOE_EXPERTS
    wr = jnp.concatenate([wr_g, wr_e, jnp.zeros((d, pad), F32)], axis=1)
    br = jnp.concatenate([br_g, br_e, jnp.zeros((pad,), F32)]).reshape(1, ROUTER_LANES)
    hp, meta, tot = moe_route(x, g, wr, br, tile)
    d0, d1, pad_first, pad_cnt, p, block_e, first, next_e, n_used = moe_layout(meta, tot, tile)
    xb = moe_dispatch(hp, d0, d1, pad_first, pad_cnt, n_used, p)
    yb = expert_ffn(xb, block_e, first, next_e, n_used, w_gate, w_up, w_down, layer)
    if out_segments is None:
        return moe_combine(x, meta, yb, d0, d1, 0, n, next_gain)
    return [moe_combine(x, meta, yb, d0, d1, row0, rows) for row0, rows in out_segments]


def kernel(x_prompt, x_sample, norm_mix, norm_ffn, ev_w_in, ev_w_out, a_ln_g, a_ln_b, a_w_s, a_b_s, b_q_gain, b_k_gain, b_rpb, od_w_in, od_w_out, c_q_gain, c_k_gain, t5_table, moe_wr_g, moe_br_g, moe_wr_e, moe_br_e, moe_w_gate, moe_w_up, moe_w_down):
    d = x_prompt.shape[-1]
    segs = [(x_prompt.shape[0], x_prompt.shape[1]), (x_sample.shape[0], x_sample.shape[1])]
    xs = [x_prompt.reshape(-1, d), x_sample.reshape(-1, d)]
    depth = norm_mix.shape[0]
    h = rmsnorm(xs, norm_mix[0], BF16)
    for l in range(depth):
        i = l // 2
        last = l == depth - 1
        if l % 2 == 0:
            aw = a_ln_g.shape[1]
            z = matmul([[h]], [(ev_w_in[i], 0)])
            a_out = mixer_a(z, a_ln_g[i], a_ln_b[i], a_w_s[i], a_b_s[i])
            tab = natten_bias_table(b_rpb[i])
            b_out, row0 = [], 0
            for batch, seq in segs:
                b_out.append(natten(z, 2 * aw // LANES, row0, batch, seq, b_q_gain[i], b_k_gain[i], tab))
                row0 += batch * seq
            x = matmul([[a_out], b_out], [(ev_w_out[i], 0), (ev_w_out[i], aw)], residuals=xs)
        else:
            qkv = matmul([[h]], [(od_w_in[i], 0)])
            tabs = [dilated_bias_table(t5_table, dil) for _, dil in C_PATTERNS]
            c_out, row0 = [], 0
            for batch, seq in segs:
                c_out.append(dilated_attention(qkv, row0, batch, seq, c_q_gain[i], c_k_gain[i], tabs))
                row0 += batch * seq
            x = matmul([c_out], [(od_w_out[i], 0)], residuals=xs)
        n0 = segs[0][0] * segs[0][1]
        res = hier_moe_residual(x, norm_ffn[l], moe_wr_g[l], moe_br_g[l], moe_wr_e[l], moe_br_e[l],
                                moe_w_gate, moe_w_up, moe_w_down, l,
                                out_segments=[(0, n0), (n0, segs[1][0] * segs[1][1])] if last else None,
                                next_gain=None if last else norm_mix[l + 1])
        if last:
            xs = res
        else:
            xs, h = [res[0]], res[1]
    return (xs[0].reshape(x_prompt.shape), xs[1].reshape(x_sample.shape))
```

```python
import functools
import math

import jax
import jax.numpy as jnp
from jax import lax
from jax.experimental import pallas as pl
from jax.experimental.pallas import tpu as pltpu

F32 = jnp.float32
BF16 = jnp.bfloat16
EPS = 1e-6
NEG = -1e30

V7X_VMEM_LIMIT_BYTES = 56 * 1024 * 1024
LANES = 128

GRID_W = 64
CHUNK = 128
A_GROUPS = 8
NA_KH = 8
NA_KW = 16
B_DH = 64
C_DH = 128
C_PATTERNS = ((128, 1), (512, 4), (2048, 16))
C_RAD = 64
C_TQ = 128
C_TK = C_TQ + 2 * C_RAD
C_SUPER = C_TQ * 16
T5_BUCKETS = 32
T5_MAX_DIST = 1024
MOE_GROUPS = 4
MOE_PER_GROUP = 8
MOE_EXPERTS = MOE_GROUPS * MOE_PER_GROUP
MOE_TOPK = 2
MOE_TM = 256
ROUTER_LANES = 128


def _params(*sem):
    return pltpu.CompilerParams(dimension_semantics=sem, vmem_limit_bytes=V7X_VMEM_LIMIT_BYTES)


def _row_sources(arrays, tm, width, col_of):
    specs, spans, off = [], [], 0
    for a in arrays:
        assert a.shape[0] % tm == 0 and a.shape[0] >= tm
        nblk = a.shape[0] // tm
        specs.append(pl.BlockSpec(
            (tm, width), lambda *g, off=off, nblk=nblk: (jnp.clip(g[-1] - off, 0, nblk - 1), col_of(*g))))
        spans.append((off, nblk))
        off += nblk
    return specs, spans


def _active_rows(i, refs, spans):
    val = refs[-1][...]
    for ref, (off, nblk) in reversed(list(zip(refs[:-1], spans[:-1]))):
        val = jnp.where(i < off + nblk, ref[...], val)
    return val


def _rmsnorm_kernel(*refs, spans):
    x_refs, (g_ref, o_ref) = refs[:len(spans)], refs[len(spans):]
    x = _active_rows(pl.program_id(0), x_refs, spans)
    y = x * lax.rsqrt(jnp.mean(x * x, axis=-1, keepdims=True) + EPS)
    o_ref[...] = (y * g_ref[...]).astype(o_ref.dtype)


def rmsnorm(xs, g, out_dtype, tm=1024):
    d = xs[0].shape[1]
    n = sum(x.shape[0] for x in xs)
    specs, spans = _row_sources(xs, tm, d, lambda i: 0)
    return pl.pallas_call(
        functools.partial(_rmsnorm_kernel, spans=spans),
        grid=(n // tm,),
        in_specs=specs + [pl.BlockSpec((1, d), lambda i: (0, 0))],
        out_specs=pl.BlockSpec((tm, d), lambda i: (i, 0)),
        out_shape=jax.ShapeDtypeStruct((n, d), out_dtype),
        compiler_params=_params("arbitrary"),
        name="rmsnorm",
    )(*xs, g.reshape(1, d))


def _matmul_kernel(*refs, x_spans, res_spans, k_chunk):
    refs = list(refs)
    x_refs = [[refs.pop(0) for _ in spans] for spans in x_spans]
    w_refs = [refs.pop(0) for _ in x_spans]
    res_refs = [refs.pop(0) for _ in res_spans]
    o_ref, wb_refs = refs[0], refs[1:]
    i = pl.program_id(1)

    @pl.when(pl.program_id(1) == 0)
    def _():
        for w_ref, wb_ref in zip(w_refs, wb_refs):
            rows = math.gcd(k_chunk, w_ref.shape[0])

            def cast(c, carry, w_ref=w_ref, wb_ref=wb_ref, rows=rows):
                sl = pl.ds(pl.multiple_of(c * rows, rows), rows)
                wb_ref[sl, :] = w_ref[sl, :].astype(BF16)
                return carry
            lax.fori_loop(0, w_ref.shape[0] // rows, cast, 0)

    acc = None
    for pieces, spans, wb_ref in zip(x_refs, x_spans, wb_refs):
        part = jnp.dot(_active_rows(i, pieces, spans), wb_ref[...], preferred_element_type=F32)
        acc = part if acc is None else acc + part
    if res_spans:
        acc = _active_rows(i, res_refs, res_spans) + acc
    o_ref[...] = acc.astype(o_ref.dtype)


def _matmul_row_tile(xs, residuals, tn, candidates=(1024, 512, 256)):
    ks = [pieces[0].shape[1] for pieces in xs]
    weights = sum(2 * k * tn * 4 + k * tn * 2 for k in ks)
    for tm in candidates:
        rows_ok = all(p.shape[0] % tm == 0 for pieces in xs for p in pieces) and \
            all(r.shape[0] % tm == 0 for r in residuals)
        blocks = sum(2 * len(pieces) * tm * k * 2 for pieces, k in zip(xs, ks))
        blocks += (2 + 2 * len(residuals) + 1) * tm * tn * 4
        if rows_ok and weights + blocks <= 0.8 * V7X_VMEM_LIMIT_BYTES:
            return tm
    raise ValueError("no row tile fits")


def matmul(xs, ws, residuals=(), out_dtype=F32, tn=1024):
    n = sum(x.shape[0] for x in xs[0])
    m = ws[0][0].shape[1]
    tn = min(tn, m)
    tm = _matmul_row_tile(xs, residuals, tn)
    in_specs, x_spans, w_specs = [], [], []
    for pieces, (w, row) in zip(xs, ws):
        k = pieces[0].shape[1]
        assert row % k == 0
        specs, spans = _row_sources(pieces, tm, k, lambda j, i: 0)
        in_specs += specs
        x_spans.append(spans)
        w_specs.append(pl.BlockSpec((k, tn), lambda j, i, rb=row // k: (rb, j)))
    res_specs, res_spans = _row_sources(list(residuals), tm, tn, lambda j, i: j)
    kern = functools.partial(_matmul_kernel, x_spans=x_spans, res_spans=res_spans, k_chunk=256)
    return pl.pallas_call(
        kern,
        grid=(m // tn, n // tm),
        in_specs=in_specs + w_specs + res_specs,
        out_specs=pl.BlockSpec((tm, tn), lambda j, i: (i, j)),
        out_shape=jax.ShapeDtypeStruct((n, m), out_dtype),
        scratch_shapes=[pltpu.VMEM((pieces[0].shape[1], tn), BF16) for pieces in xs],
        compiler_params=_params("arbitrary", "arbitrary"),
        name="matmul",
    )(*[p for pieces in xs for p in pieces], *[w for w, _ in ws], *residuals)


def _mixer_a_kernel(u_ref, v_ref, lng_ref, lnb_ref, ws_ref, bs_ref, o_ref):
    tm = u_ref.shape[0]
    gd = u_ref.shape[1] // A_GROUPS
    for c in range(tm // CHUNK):
        rows = slice(c * CHUNK, (c + 1) * CHUNK)
        u = jax.nn.gelu(u_ref[rows, :])
        v = jax.nn.gelu(v_ref[rows, :])
        mu = jnp.mean(v, axis=-1, keepdims=True)
        vc = v - mu
        var = jnp.mean(vc * vc, axis=-1, keepdims=True)
        v = vc * lax.rsqrt(var + EPS) * lng_ref[...] + lnb_ref[...]
        for g in range(A_GROUPS):
            cols = slice(g * gd, (g + 1) * gd)
            f = jnp.dot(ws_ref[g].astype(BF16), v[:, cols].astype(BF16), preferred_element_type=F32)
            f = f + bs_ref[g]
            o_ref[rows, cols] = (u[:, cols] * f).astype(o_ref.dtype)


def mixer_a(z, ln_g, ln_b, w_s, b_s, tm=512):
    n = z.shape[0]
    aw = ln_g.shape[0]
    gd = aw // A_GROUPS
    bs_b = jnp.broadcast_to(b_s[:, :, None], (A_GROUPS, CHUNK, gd))
    return pl.pallas_call(
        _mixer_a_kernel,
        grid=(n // tm,),
        in_specs=[
            pl.BlockSpec((tm, aw), lambda i: (i, 0)),
            pl.BlockSpec((tm, aw), lambda i: (i, 1)),
            pl.BlockSpec((1, aw), lambda i: (0, 0)),
            pl.BlockSpec((1, aw), lambda i: (0, 0)),
            pl.BlockSpec((A_GROUPS, CHUNK, CHUNK), lambda i: (0, 0, 0)),
            pl.BlockSpec((A_GROUPS, CHUNK, gd), lambda i: (0, 0, 0)),
        ],
        out_specs=pl.BlockSpec((tm, aw), lambda i: (i, 0)),
        out_shape=jax.ShapeDtypeStruct((n, aw), BF16),
        compiler_params=_params("parallel"),
        name="mixer_a",
    )(z, z, ln_g.reshape(1, aw), ln_b.reshape(1, aw), w_s, bs_b)


def _skew(w, rows):
    n = w.shape[-1]
    flat = jnp.tile(w, (1,) * (w.ndim - 1) + (rows,))
    return flat[..., :rows * (n - 1)].reshape(w.shape[:-1] + (rows, n - 1))


def _pair_rmsnorm(x, gain):
    sq = x * x
    lane = lax.broadcasted_iota(jnp.int32, x.shape, 1)
    lo = lane < B_DH
    s_lo = jnp.sum(jnp.where(lo, sq, 0.0), axis=-1, keepdims=True)
    s_hi = jnp.sum(jnp.where(lo, 0.0, sq), axis=-1, keepdims=True)
    ms = jnp.where(lo, s_lo, s_hi) * (1.0 / B_DH)
    return x * lax.rsqrt(ms + EPS) * gain


def _natten_kernel(q_ref, k_ref, v_ref, gq_ref, gk_ref, bias_ref, o_ref, kn_ref, vb_ref, s_ref, m_ref, e_ref,
                   *, rows_per_step, rows):
    seq = k_ref.shape[0]
    t = pl.program_id(2)
    prep = 256

    @pl.when(t == 0)
    def _():
        def body(c, carry):
            sl = pl.ds(pl.multiple_of(c * prep, prep), prep)
            kn_ref[sl, :] = _pair_rmsnorm(k_ref[sl, :], gk_ref[...]).astype(BF16)
            vb_ref[sl, :LANES] = v_ref[sl, :].astype(BF16)
            vb_ref[sl, LANES:] = jnp.ones((prep, LANES), BF16)
            return carry
        lax.fori_loop(0, seq // prep, body, 0)

    q = (_pair_rmsnorm(q_ref[...], gq_ref[...]) * (B_DH ** -0.5)).astype(BF16)
    lo = lax.broadcasted_iota(jnp.int32, (GRID_W, LANES), 1) < B_DH
    zero = jnp.zeros((GRID_W, LANES), BF16)
    win = NA_KH * GRID_W

    def window(rr):
        r = t * rows_per_step + rr
        rs = jnp.clip(r - NA_KH // 2, 0, rows - NA_KH)
        return r - rs, pl.ds(pl.multiple_of(rs * GRID_W, GRID_W), win)

    for rr in range(rows_per_step):
        case, ksl = window(rr)
        qr = q[rr * GRID_W:(rr + 1) * GRID_W, :]
        q2 = jnp.concatenate([jnp.where(lo, qr, zero), jnp.where(lo, zero, qr)], axis=0)
        s = lax.dot_general(q2, kn_ref[ksl, :], (((1,), (1,)), ((), ())), preferred_element_type=F32)
        s_ref[rr] = s + bias_ref[0, case]
    for rr in range(rows_per_step):
        m = jnp.max(s_ref[rr], axis=-1, keepdims=True)
        m_ref[rr] = jnp.broadcast_to(m, (2 * GRID_W, LANES))
    for rr in range(rows_per_step):
        mb = m_ref[rr]
        for c in range(win // LANES):
            cols = slice(c * LANES, (c + 1) * LANES)
            e_ref[rr, :, cols] = jnp.exp(s_ref[rr, :, cols] - mb).astype(BF16)
    for rr in range(rows_per_step):
        _, ksl = window(rr)
        ol = jnp.dot(e_ref[rr], vb_ref[ksl, :], preferred_element_type=F32)
        o = ol[:, :LANES] / ol[:, LANES:]
        o_ref[rr * GRID_W:(rr + 1) * GRID_W, :] = jnp.where(lo, o[:GRID_W], o[GRID_W:]).astype(o_ref.dtype)


def natten_bias_table(rpb):
    heads = rpb.shape[0]
    col = jnp.arange(GRID_W)
    col_start = jnp.clip(col - NA_KW // 2, 0, GRID_W - NA_KW)
    col_valid = (col[None, :] >= col_start[:, None]) & (col[None, :] < col_start[:, None] + NA_KW)
    edge = GRID_W - NA_KW
    w = jnp.concatenate([jnp.repeat(rpb[..., :1], edge, axis=-1), rpb.astype(F32),
                         jnp.repeat(rpb[..., -1:], edge, axis=-1), jnp.zeros(rpb.shape[:-1] + (1,), F32)], axis=-1)
    bias_c = _skew(w, GRID_W)[..., GRID_W - 1:]
    tab = jnp.stack([bias_c[:, NA_KH - 1 - c:2 * NA_KH - 1 - c] for c in range(NA_KH)], axis=1)
    tab = jnp.where(col_valid[:, None, :], tab.transpose(0, 1, 3, 2, 4), NEG)
    tab = tab.reshape(heads // 2, 2, NA_KH, GRID_W, NA_KH * GRID_W).transpose(0, 2, 1, 3, 4)
    return tab.reshape(heads // 2, NA_KH, 2 * GRID_W, NA_KH * GRID_W)


def natten(z, col0, row0, batch, seq, gq, gk, bias_tab, rows_per_step=16):
    hp = bias_tab.shape[0]
    rows = seq // GRID_W
    assert rows >= NA_KH and rows % rows_per_step == 0 and row0 % seq == 0
    tq = rows_per_step * GRID_W
    steps = rows // rows_per_step
    qb0 = row0 // tq
    sb0 = row0 // seq
    win = NA_KH * GRID_W
    kern = functools.partial(_natten_kernel, rows_per_step=rows_per_step, rows=rows)
    gq2 = jnp.tile(gq, 2).reshape(1, 2 * B_DH)
    gk2 = jnp.tile(gk, 2).reshape(1, 2 * B_DH)
    return pl.pallas_call(
        kern,
        grid=(batch, hp, steps),
        in_specs=[
            pl.BlockSpec((tq, LANES), lambda b, h, t: (qb0 + b * steps + t, col0 + h)),
            pl.BlockSpec((seq, LANES), lambda b, h, t: (sb0 + b, col0 + hp + h)),
            pl.BlockSpec((seq, LANES), lambda b, h, t: (sb0 + b, col0 + 2 * hp + h)),
            pl.BlockSpec((1, LANES), lambda b, h, t: (0, 0)),
            pl.BlockSpec((1, LANES), lambda b, h, t: (0, 0)),
            pl.BlockSpec((1, NA_KH, 2 * GRID_W, win), lambda b, h, t: (h, 0, 0, 0)),
        ],
        out_specs=pl.BlockSpec((tq, LANES), lambda b, h, t: (b * steps + t, h)),
        out_shape=jax.ShapeDtypeStruct((batch * seq, 2 * hp * B_DH), BF16),
        scratch_shapes=[
            pltpu.VMEM((seq, LANES), BF16),
            pltpu.VMEM((seq, 2 * LANES), BF16),
            pltpu.VMEM((rows_per_step, 2 * GRID_W, win), F32),
            pltpu.VMEM((rows_per_step, 2 * GRID_W, LANES), F32),
            pltpu.VMEM((rows_per_step, 2 * GRID_W, win), BF16),
        ],
        compiler_params=_params("parallel", "parallel", "arbitrary"),
        name="natten",
    )(z, z, z, gq2, gk2, bias_tab)


def t5_bucket(rel):
    nb = T5_BUCKETS // 2
    max_exact = nb // 2
    ret = jnp.where(rel > 0, nb, 0)
    n = jnp.abs(rel)
    large = max_exact + (jnp.log(jnp.maximum(n, 1).astype(F32) / max_exact)
                         / math.log(T5_MAX_DIST / max_exact) * (nb - max_exact)).astype(jnp.int32)
    large = jnp.minimum(large, nb - 1)
    return ret + jnp.where(n < max_exact, n, large)


def dilated_bias_table(t5_table, dil):
    span = C_TK - 1
    rel = jnp.arange(-span, span + 1)
    vec = t5_table.astype(F32)[t5_bucket(rel * dil)]
    vec = jnp.where((jnp.abs(rel) <= C_RAD)[:, None], vec, NEG).T
    w = jnp.concatenate([vec, jnp.zeros((vec.shape[0], 1), F32)], axis=1)
    r = _skew(w, C_TQ)
    return jnp.stack([r[:, :, span - off:span - off + C_TK] for off in (0, C_RAD, 2 * C_RAD)], axis=1)


def _row_rmsnorm(x, gain):
    return x * lax.rsqrt(jnp.mean(x * x, axis=-1, keepdims=True) + EPS) * gain


def _dilated_kernel(q_ref, k_ref, v_ref, gq_ref, gk_ref, b0_ref, b1_ref, b2_ref, o_ref,
                    kn_ref, qn_ref, oacc_ref, lse_ref, s_ref, m_ref, e_ref, *kv_by_residue, seq, unroll):
    t = pl.program_id(2)
    prep = 256
    kv_sub = {dil: (kv_by_residue[2 * i], kv_by_residue[2 * i + 1])
              for i, dil in enumerate(d for _, d in C_PATTERNS if d > 1)}

    @pl.when(t == 0)
    def _():
        def body(c, carry):
            sl = pl.ds(pl.multiple_of(c * prep, prep), prep)
            kn_ref[sl, :] = _row_rmsnorm(k_ref[sl, :], gk_ref[...])
            return carry
        lax.fori_loop(0, seq // prep, body, 0)

        for dil, (ks_ref, vs_ref) in kv_sub.items():
            rows = min(prep, seq // dil)
            for s in range(dil):
                def split(c, carry, dil=dil, s=s, rows=rows, ks_ref=ks_ref, vs_ref=vs_ref):
                    src = pl.ds(c * rows * dil + s, rows, stride=dil)
                    dst = pl.ds(pl.multiple_of(c * rows, rows), rows)
                    ks_ref[s, dst, :] = kn_ref[src, :].astype(BF16)
                    vs_ref[s, dst, :] = v_ref[src, :].astype(BF16)
                    return carry
                lax.fori_loop(0, seq // (dil * rows), split, 0)

    qn_ref[...] = _row_rmsnorm(q_ref[...], gq_ref[...]) * (C_DH ** -0.5)
    n_tiles = C_SUPER // C_TQ

    for p, ((window, dil), b_ref) in enumerate(zip(C_PATTERNS, (b0_ref, b1_ref, b2_ref))):
        n_sub = seq // dil
        tk = min(C_TK, n_sub)
        span = C_TQ * dil
        ones = jnp.ones((tk, LANES), BF16)

        def tile(idx, dil=dil, n_sub=n_sub, tk=tk, span=span):
            u = idx // dil
            s = idx % dil
            qpos = u * span + s
            j0 = t * (C_SUPER // dil) + u * C_TQ
            ws = jnp.clip(j0 - C_RAD, 0, n_sub - tk)
            ksl = pl.ds(pl.multiple_of(ws, C_RAD), tk)
            if dil == 1:
                return pl.ds(pl.multiple_of(qpos, C_TQ), C_TQ), s, ksl, (j0 - ws) // C_RAD
            return pl.ds(qpos, C_TQ, stride=dil), s, ksl, (j0 - ws) // C_RAD

        def keys(s, ksl, dil=dil):
            return kn_ref[ksl, :].astype(BF16) if dil == 1 else kv_sub[dil][0][s, ksl, :]

        def values(s, ksl, dil=dil):
            return v_ref[ksl, :].astype(BF16) if dil == 1 else kv_sub[dil][1][s, ksl, :]

        def scores(idx, carry, tile=tile, keys=keys, tk=tk, b_ref=b_ref):
            qsl, s, ksl, case = tile(idx)
            q = qn_ref[qsl, :].astype(BF16)
            sc = lax.dot_general(q, keys(s, ksl), (((1,), (1,)), ((), ())), preferred_element_type=F32)
            s_ref[idx, :, :tk] = sc + b_ref[0, case, :, :tk]
            return carry

        def row_max(idx, carry, tk=tk):
            m = jnp.max(s_ref[idx, :, :tk], axis=-1, keepdims=True)
            m_ref[idx] = jnp.broadcast_to(m, (C_TQ, LANES))
            return carry

        def exps(idx, carry, tk=tk):
            mb = m_ref[idx]
            for c in range(tk // LANES):
                cols = slice(c * LANES, (c + 1) * LANES)
                e_ref[idx, :, cols] = jnp.exp(s_ref[idx, :, cols] - mb).astype(BF16)
            return carry

        def outputs(idx, carry, tile=tile, values=values, tk=tk, ones=ones, p=p):
            qsl, s, ksl, _ = tile(idx)
            v1 = jnp.concatenate([values(s, ksl), ones], axis=1)
            ol = jnp.dot(e_ref[idx, :, :tk], v1, preferred_element_type=F32)
            l = ol[:, LANES:]
            oacc_ref[p, qsl, :] = ol[:, :LANES] / l
            lse_ref[p, qsl, :] = m_ref[idx] + jnp.log(l)
            return carry

        for phase in (scores, row_max, exps, outputs):
            lax.fori_loop(0, n_tiles, phase, 0, unroll=unroll)

    l0, l1, l2 = lse_ref[0], lse_ref[1], lse_ref[2]
    mx = jnp.maximum(jnp.maximum(l0, l1), l2)
    w0, w1, w2 = jnp.exp(l0 - mx), jnp.exp(l1 - mx), jnp.exp(l2 - mx)
    num = w0 * oacc_ref[0] + w1 * oacc_ref[1] + w2 * oacc_ref[2]
    o_ref[...] = (num / (w0 + w1 + w2)).astype(o_ref.dtype)


def dilated_attention(qkv, row0, batch, seq, gq, gk, tabs, unroll=16):
    heads = tabs[0].shape[0]
    assert seq % C_SUPER == 0 and row0 % seq == 0
    steps = seq // C_SUPER
    qb0 = row0 // C_SUPER
    sb0 = row0 // seq
    n_tiles = C_SUPER // C_TQ
    kern = functools.partial(_dilated_kernel, seq=seq, unroll=unroll)
    tab_specs = [pl.BlockSpec((1,) + tuple(tb.shape[1:]), lambda b, h, t: (h, 0, 0, 0)) for tb in tabs]
    return pl.pallas_call(
        kern,
        grid=(batch, heads, steps),
        in_specs=[
            pl.BlockSpec((C_SUPER, LANES), lambda b, h, t: (qb0 + b * steps + t, h)),
            pl.BlockSpec((seq, LANES), lambda b, h, t: (sb0 + b, heads + h)),
            pl.BlockSpec((seq, LANES), lambda b, h, t: (sb0 + b, 2 * heads + h)),
            pl.BlockSpec((1, LANES), lambda b, h, t: (0, 0)),
            pl.BlockSpec((1, LANES), lambda b, h, t: (0, 0)),
        ] + tab_specs,
        out_specs=pl.BlockSpec((C_SUPER, LANES), lambda b, h, t: (b * steps + t, h)),
        out_shape=jax.ShapeDtypeStruct((batch * seq, heads * C_DH), BF16),
        scratch_shapes=[
            pltpu.VMEM((seq, LANES), F32),
            pltpu.VMEM((C_SUPER, LANES), F32),
            pltpu.VMEM((len(C_PATTERNS), C_SUPER, LANES), F32),
            pltpu.VMEM((len(C_PATTERNS), C_SUPER, LANES), F32),
            pltpu.VMEM((n_tiles, C_TQ, C_TK), F32),
            pltpu.VMEM((n_tiles, C_TQ, LANES), F32),
            pltpu.VMEM((n_tiles, C_TQ, C_TK), BF16),
        ] + [pltpu.VMEM((dil, seq // dil, LANES), BF16) for _, dil in C_PATTERNS if dil > 1 for _ in "kv"],
        compiler_params=_params("parallel", "parallel", "arbitrary"),
        name="dilated",
    )(qkv, qkv, qkv, gq.reshape(1, C_DH), gk.reshape(1, C_DH), *tabs)


def _pack_bf16_pairs(lo, hi):
    lo_bits = lax.bitcast_convert_type(lo.astype(BF16).astype(F32), jnp.uint32)
    hi_bits = lax.bitcast_convert_type(hi.astype(BF16).astype(F32), jnp.uint32)
    return (hi_bits & jnp.uint32(0xFFFF0000)) | (lo_bits >> 16)


def _unpack_bf16_pairs(packed):
    lo = lax.bitcast_convert_type(packed << 16, F32)
    hi = lax.bitcast_convert_type(packed & jnp.uint32(0xFFFF0000), F32)
    return lo, hi


META_E0, META_E1, META_R0, META_R1, META_G0, META_G1 = range(6)


def _route_kernel(x_ref, g_ref, wr_ref, br_ref, hp_ref, meta_ref, tot_ref):
    x = x_ref[...]
    h = x * lax.rsqrt(jnp.mean(x * x, axis=-1, keepdims=True) + EPS) * g_ref[...]
    half = h.shape[1] // 2
    hp_ref[...] = _pack_bf16_pairs(h[:, :half], h[:, half:])
    h_hi = h.astype(BF16)
    h_lo = (h - h_hi.astype(F32)).astype(BF16)
    w_hi = wr_ref[...].astype(BF16)
    w_lo = (wr_ref[...] - w_hi.astype(F32)).astype(BF16)
    logits = (jnp.dot(h_hi, w_hi, preferred_element_type=F32) + jnp.dot(h_hi, w_lo, preferred_element_type=F32)
              + jnp.dot(h_lo, w_hi, preferred_element_type=F32)) + br_ref[...]
    tm = x.shape[0]
    lane = lax.broadcasted_iota(jnp.int32, (tm, ROUTER_LANES), 1)
    lane_f = lane.astype(F32)

    def first_max(v):
        mx = jnp.max(v, axis=-1, keepdims=True)
        return mx, jnp.min(jnp.where(v == mx, lane_f, float(ROUTER_LANES)), axis=-1, keepdims=True)

    lg = jnp.where(lane < MOE_GROUPS, logits, -jnp.inf)
    gmax, gsel = first_max(lg)
    p_g = 1.0 / jnp.sum(jnp.exp(lg - gmax), axis=-1, keepdims=True)
    expert_lane = (lane >= MOE_GROUPS) & (lane < MOE_GROUPS + MOE_EXPERTS)
    in_group = expert_lane & (((lane - MOE_GROUPS) // MOE_PER_GROUP) == gsel.astype(jnp.int32))
    le = jnp.where(in_group, logits, -jnp.inf)
    v0, i0 = first_max(le)
    pick0 = lane_f == i0
    v1, i1 = first_max(jnp.where(pick0, -jnp.inf, le))
    pick1 = lane_f == i1
    t = jnp.exp(v1 - v0)
    g0 = p_g / (1.0 + t)
    g1 = p_g * t / (1.0 + t)

    onehot = jnp.where(pick0 | pick1, 1.0, 0.0).astype(BF16)
    row = lax.broadcasted_iota(jnp.int32, (tm, tm), 0)
    col = lax.broadcasted_iota(jnp.int32, (tm, tm), 1)
    earlier = jnp.where(col < row, 1.0, 0.0).astype(BF16)
    before = jnp.dot(earlier, onehot, preferred_element_type=F32)
    r0 = jnp.sum(jnp.where(pick0, before, 0.0), axis=-1, keepdims=True)
    r1 = jnp.sum(jnp.where(pick1, before, 0.0), axis=-1, keepdims=True)

    meta = jnp.zeros((tm, ROUTER_LANES), F32)
    for pos, val in ((META_E0, i0 - MOE_GROUPS), (META_E1, i1 - MOE_GROUPS), (META_R0, r0), (META_R1, r1),
                     (META_G0, g0), (META_G1, g1)):
        meta = jnp.where(lane == pos, val, meta)
    meta_ref[...] = meta
    tot_ref[0] = jnp.sum(onehot.astype(F32), axis=0, keepdims=True)


def moe_route(x, g, wr, br, tm):
    n, d = x.shape
    return pl.pallas_call(
        _route_kernel,
        grid=(n // tm,),
        in_specs=[
            pl.BlockSpec((tm, d), lambda i: (i, 0)),
            pl.BlockSpec((1, d), lambda i: (0, 0)),
            pl.BlockSpec((d, ROUTER_LANES), lambda i: (0, 0)),
            pl.BlockSpec((1, ROUTER_LANES), lambda i: (0, 0)),
        ],
        out_specs=[
            pl.BlockSpec((tm, d // 2), lambda i: (i, 0)),
            pl.BlockSpec((tm, ROUTER_LANES), lambda i: (i, 0)),
            pl.BlockSpec((1, 1, ROUTER_LANES), lambda i: (i, 0, 0)),
        ],
        out_shape=[
            jax.ShapeDtypeStruct((n, d // 2), jnp.uint32),
            jax.ShapeDtypeStruct((n, ROUTER_LANES), F32),
            jax.ShapeDtypeStruct((n // tm, 1, ROUTER_LANES), F32),
        ],
        compiler_params=_params("parallel"),
        name="moe_route",
    )(x, g.reshape(1, d), wr, br)


def _dispatch_kernel(d0_ref, d1_ref, pad_first_ref, pad_cnt_ref, nused_ref, h_ref, xb_hbm, sem):
    i = pl.program_id(0)
    tm = h_ref.shape[0]
    base = i * tm

    def row_copy(src_row, dst_row, level=0):
        return pltpu.make_async_copy(h_ref.at[pl.ds(src_row, 1), :], xb_hbm.at[pl.ds(dst_row, 1), :], sem.at[level])

    for r in range(tm):
        row_copy(r, d0_ref[base + r], 0).start(priority=0)
        row_copy(r, d1_ref[base + r], 1).start(priority=1)

    def drain(r, carry):
        row_copy(0, 0, 0).wait()
        row_copy(0, 0, 1).wait()
        return carry
    lax.fori_loop(0, tm, drain, 0, unroll=8)

    @pl.when(i == pl.num_programs(0) - 1)
    def _():
        def per_expert(e, carry):
            def fill(j, c):
                row_copy(0, pad_first_ref[e] + j).start()
                return c
            lax.fori_loop(0, pad_cnt_ref[e], fill, 0)

            def fill_wait(j, c):
                row_copy(0, 0).wait()
                return c
            lax.fori_loop(0, pad_cnt_ref[e], fill_wait, 0)
            return carry
        lax.fori_loop(0, MOE_EXPERTS, per_expert, 0)

        def block_copy(b):
            return pltpu.make_async_copy(h_ref, xb_hbm.at[pl.ds(pl.multiple_of(b * tm, tm), tm), :], sem.at[0])

        def fill_block(b, c):
            block_copy(b).start()
            return c
        lax.fori_loop(nused_ref[0], xb_hbm.shape[0] // tm, fill_block, 0)

        def fill_block_wait(b, c):
            block_copy(b).wait()
            return c
        lax.fori_loop(nused_ref[0], xb_hbm.shape[0] // tm, fill_block_wait, 0)


def moe_dispatch(hp, d0, d1, pad_first, pad_cnt, n_used, p, tm=MOE_TM):
    n, dw = hp.shape
    return pl.pallas_call(
        _dispatch_kernel,
        grid_spec=pltpu.PrefetchScalarGridSpec(
            num_scalar_prefetch=5,
            grid=(n // tm,),
            in_specs=[pl.BlockSpec((tm, dw), lambda i, *_: (i, 0))],
            out_specs=pl.BlockSpec(memory_space=pl.ANY),
            scratch_shapes=[pltpu.SemaphoreType.DMA((2,))],
        ),
        out_shape=jax.ShapeDtypeStruct((p, dw), hp.dtype),
        compiler_params=_params("arbitrary"),
        name="moe_dispatch",
    )(d0, d1, pad_first, pad_cnt, n_used, hp)


def _row_gather_copy(src_hbm, dst_ref, sem, src_row, dst_row):
    return pltpu.make_async_copy(src_hbm.at[pl.ds(src_row, 1), :], dst_ref.at[pl.ds(dst_row, 1), :], sem)


def _gather_row_sets(gathers, base, src_hbm, unroll=8):
    rows = gathers[0][1].shape[0]
    for r in range(rows):
        for level, (idx_ref, dst_ref, sem) in enumerate(gathers):
            _row_gather_copy(src_hbm, dst_ref, sem, idx_ref[base + r], r).start(priority=level)

    def drain(r, carry):
        for _, dst_ref, sem in gathers:
            _row_gather_copy(src_hbm, dst_ref, sem, 0, r).wait()
        return carry
    lax.fori_loop(0, rows, drain, 0, unroll=unroll)


def _expert_weight_copies(layer, e, w_hbm, stage, sem):
    return [pltpu.make_async_copy(w.at[layer, e], s, sem.at[k]) for k, (w, s) in enumerate(zip(w_hbm, stage))]


def _ffn_kernel(be_ref, first_ref, next_ref, nused_ref, x_ref, wg_hbm, wu_hbm, wd_hbm, o_ref,
                sg_ref, su_ref, sd_ref, wg_ref, wu_ref, wd_ref, sem, *, layer, k_chunk):
    i = pl.program_id(0)
    w_hbm = (wg_hbm, wu_hbm, wd_hbm)
    stage = (sg_ref, su_ref, sd_ref)
    resident = (wg_ref, wu_ref, wd_ref)

    @pl.when(i == 0)
    def _():
        for cp in _expert_weight_copies(layer, be_ref[0], w_hbm, stage, sem):
            cp.start()

    @pl.when(first_ref[i] == 1)
    def _():
        for cp in _expert_weight_copies(layer, be_ref[i], w_hbm, stage, sem):
            cp.wait()
        for s_ref, w_ref in zip(stage, resident):
            rows = math.gcd(k_chunk, s_ref.shape[0])

            def cast(c, carry, s_ref=s_ref, w_ref=w_ref, rows=rows):
                sl = pl.ds(pl.multiple_of(c * rows, rows), rows)
                w_ref[sl, :] = s_ref[sl, :].astype(BF16)
                return carry
            lax.fori_loop(0, s_ref.shape[0] // rows, cast, 0)

        @pl.when(next_ref[i] >= 0)
        def _():
            for cp in _expert_weight_copies(layer, next_ref[i], w_hbm, stage, sem):
                cp.start()

    live = i < nused_ref[0]

    @pl.when(live)
    def _():
        lo, hi = _unpack_bf16_pairs(x_ref[...])
        lo, hi = lo.astype(BF16), hi.astype(BF16)
        half = lo.shape[1]
        g = (jnp.dot(lo, wg_ref[:half, :], preferred_element_type=F32)
             + jnp.dot(hi, wg_ref[half:, :], preferred_element_type=F32))
        u = (jnp.dot(lo, wu_ref[:half, :], preferred_element_type=F32)
             + jnp.dot(hi, wu_ref[half:, :], preferred_element_type=F32))
        h = (jax.nn.silu(g) * u).astype(BF16)
        y = jnp.dot(h, wd_ref[...], preferred_element_type=F32)
        o_ref[...] = _pack_bf16_pairs(y[:, :half], y[:, half:])

    @pl.when(jnp.logical_not(live))
    def _():
        o_ref[...] = jnp.zeros_like(o_ref)


def expert_ffn(xb, block_e, first, next_e, n_used, wg, wu, wd, layer, tm=MOE_TM):
    p, dw = xb.shape
    d, de = wg.shape[2], wg.shape[3]
    any_spec = pl.BlockSpec(memory_space=pl.ANY)
    return pl.pallas_call(
        functools.partial(_ffn_kernel, layer=layer, k_chunk=256),
        grid_spec=pltpu.PrefetchScalarGridSpec(
            num_scalar_prefetch=4,
            grid=(p // tm,),
            in_specs=[
                pl.BlockSpec((tm, dw), lambda i, be, fi, ne, nu: (jnp.minimum(i, nu[0] - 1), 0)),
                any_spec, any_spec, any_spec,
            ],
            out_specs=pl.BlockSpec((tm, dw), lambda i, *_: (i, 0)),
            scratch_shapes=[
                pltpu.VMEM((d, de), F32), pltpu.VMEM((d, de), F32), pltpu.VMEM((de, d), F32),
                pltpu.VMEM((d, de), BF16), pltpu.VMEM((d, de), BF16), pltpu.VMEM((de, d), BF16),
                pltpu.SemaphoreType.DMA((3,)),
            ],
        ),
        out_shape=jax.ShapeDtypeStruct((p, dw), jnp.uint32),
        compiler_params=_params("arbitrary"),
        name="moe_ffn",
    )(block_e, first, next_e, n_used, xb, wg, wu, wd)


def _combine_kernel(d0_ref, d1_ref, x_ref, meta_ref, yb_hbm, *rest, row0, with_norm):
    if with_norm:
        gain_ref, o_ref, h_ref, a_ref, b_ref, sem = rest
    else:
        o_ref, a_ref, b_ref, sem = rest
    base = row0 + pl.program_id(0) * x_ref.shape[0]
    _gather_row_sets([(d0_ref, a_ref, sem.at[0]), (d1_ref, b_ref, sem.at[1])], base, yb_hbm)
    a_lo, a_hi = _unpack_bf16_pairs(a_ref[...])
    b_lo, b_hi = _unpack_bf16_pairs(b_ref[...])
    g0 = meta_ref[:, META_G0:META_G0 + 1]
    g1 = meta_ref[:, META_G1:META_G1 + 1]
    half = a_lo.shape[1]
    lo = x_ref[:, :half] + (g0 * a_lo + g1 * b_lo)
    hi = x_ref[:, half:] + (g0 * a_hi + g1 * b_hi)
    o_ref[:, :half] = lo
    o_ref[:, half:] = hi
    if with_norm:
        ms = (jnp.sum(lo * lo, axis=-1, keepdims=True) + jnp.sum(hi * hi, axis=-1, keepdims=True)) / (2 * half)
        inv = lax.rsqrt(ms + EPS)
        h_ref[:, :half] = (lo * inv * gain_ref[:, :half]).astype(h_ref.dtype)
        h_ref[:, half:] = (hi * inv * gain_ref[:, half:]).astype(h_ref.dtype)


def moe_combine(x, meta, yb, d0, d1, row0, rows, next_gain=None, tm=256):
    d = x.shape[1]
    assert row0 % tm == 0
    blk0 = row0 // tm
    with_norm = next_gain is not None
    row_spec = pl.BlockSpec((tm, d), lambda i, a, b: (i, 0))
    in_specs = [
        pl.BlockSpec((tm, d), lambda i, a, b: (blk0 + i, 0)),
        pl.BlockSpec((tm, ROUTER_LANES), lambda i, a, b: (blk0 + i, 0)),
        pl.BlockSpec(memory_space=pl.ANY),
    ]
    args = [d0, d1, x, meta, yb]
    out_specs, out_shape = row_spec, jax.ShapeDtypeStruct((rows, d), F32)
    if with_norm:
        in_specs.append(pl.BlockSpec((1, d), lambda i, a, b: (0, 0)))
        args.append(next_gain.reshape(1, d))
        out_specs, out_shape = [row_spec, row_spec], [out_shape, jax.ShapeDtypeStruct((rows, d), BF16)]
    return pl.pallas_call(
        functools.partial(_combine_kernel, row0=row0, with_norm=with_norm),
        grid_spec=pltpu.PrefetchScalarGridSpec(
            num_scalar_prefetch=2,
            grid=(rows // tm,),
            in_specs=in_specs,
            out_specs=out_specs,
            scratch_shapes=[pltpu.VMEM((tm, d // 2), jnp.uint32), pltpu.VMEM((tm, d // 2), jnp.uint32),
                            pltpu.SemaphoreType.DMA((2,))],
        ),
        out_shape=out_shape,
        compiler_params=_params("arbitrary"),
        name="moe_combine",
    )(*args)


def moe_layout(meta, tot, tile, tm=MOE_TM):
    n = meta.shape[0]
    m = n * MOE_TOPK
    ids = jnp.arange(MOE_EXPERTS, dtype=jnp.int32)
    cnt = tot[:, 0, MOE_GROUPS:MOE_GROUPS + MOE_EXPERTS].astype(jnp.int32)
    counts = jnp.sum(cnt, axis=0)
    padded = (counts + tm - 1) // tm * tm
    pad_end = jnp.cumsum(padded)
    pad_start = pad_end - padded
    base = pad_start[None, :] + jnp.cumsum(cnt, axis=0) - cnt
    base_tok = jnp.repeat(base, tile, axis=0)

    def slots(e_col, r_col):
        e = meta[:, e_col].astype(jnp.int32)
        return jnp.sum(jnp.where(e[:, None] == ids[None, :], base_tok, 0), axis=1) + meta[:, r_col].astype(jnp.int32)

    d0 = slots(META_E0, META_R0)
    d1 = slots(META_E1, META_R1)
    n_blocks = (m + MOE_EXPERTS * (tm - 1) + tm - 1) // tm

    blk = jnp.arange(n_blocks, dtype=jnp.int32)
    n_used = (pad_end[-1] // tm).astype(jnp.int32)
    block_e = jnp.sum((pad_end[None, :] <= (blk * tm)[:, None]).astype(jnp.int32), axis=1)
    block_e = jnp.minimum(block_e, MOE_EXPERTS - 1)
    live = blk < n_used
    first = (live & ((blk == 0) | (block_e != jnp.roll(block_e, 1)))).astype(jnp.int32)
    has_rows = counts > 0
    later = jnp.arange(MOE_EXPERTS)[None, :] > jnp.arange(MOE_EXPERTS)[:, None]
    nxt_of_e = jnp.min(jnp.where(later & has_rows[None, :], jnp.arange(MOE_EXPERTS)[None, :], MOE_EXPERTS), axis=1)
    nxt_of_e = jnp.where(nxt_of_e == MOE_EXPERTS, -1, nxt_of_e).astype(jnp.int32)
    next_e = jnp.sum(jnp.where(block_e[:, None] == ids[None, :], nxt_of_e[None, :], 0), axis=1)
    pad_first = (pad_start + counts).astype(jnp.int32)
    pad_cnt = (padded - counts).astype(jnp.int32)
    return d0, d1, pad_first, pad_cnt, n_blocks * tm, block_e, first, next_e, n_used.reshape(1)


def hier_moe_residual(x, g, wr_g, br_g, wr_e, br_e, w_gate, w_up, w_down, layer, out_segments=None,
                      next_gain=None, tile=512):
    n, d = x.shape
    pad = ROUTER_LANES - MOE_GROUPS - MOE_EXPERTS
    wr = jnp.concatenate([wr_g, wr_e, jnp.zeros((d, pad), F32)], axis=1)
    br = jnp.concatenate([br_g, br_e, jnp.zeros((pad,), F32)]).reshape(1, ROUTER_LANES)
    hp, meta, tot = moe_route(x, g, wr, br, tile)
    d0, d1, pad_first, pad_cnt, p, block_e, first, next_e, n_used = moe_layout(meta, tot, tile)
    xb = moe_dispatch(hp, d0, d1, pad_first, pad_cnt, n_used, p)
    yb = expert_ffn(xb, block_e, first, next_e, n_used, w_gate, w_up, w_down, layer)
    if out_segments is None:
        return moe_combine(x, meta, yb, d0, d1, 0, n, next_gain)
    return [moe_combine(x, meta, yb, d0, d1, row0, rows) for row0, rows in out_segments]


def kernel(x_prompt, x_sample, norm_mix, norm_ffn, ev_w_in, ev_w_out, a_ln_g, a_ln_b, a_w_s, a_b_s, b_q_gain, b_k_gain, b_rpb, od_w_in, od_w_out, c_q_gain, c_k_gain, t5_table, moe_wr_g, moe_br_g, moe_wr_e, moe_br_e, moe_w_gate, moe_w_up, moe_w_down):
    d = x_prompt.shape[-1]
    segs = [(x_prompt.shape[0], x_prompt.shape[1]), (x_sample.shape[0], x_sample.shape[1])]
    xs = [x_prompt.reshape(-1, d), x_sample.reshape(-1, d)]
    depth = norm_mix.shape[0]
    h = rmsnorm(xs, norm_mix[0], BF16)
    for l in range(depth):
        i = l // 2
        last = l == depth - 1
        if l % 2 == 0:
            aw = a_ln_g.shape[1]
            z = matmul([[h]], [(ev_w_in[i], 0)])
            a_out = mixer_a(z, a_ln_g[i], a_ln_b[i], a_w_s[i], a_b_s[i])
            tab = natten_bias_table(b_rpb[i])
            b_out, row0 = [], 0
            for batch, seq in segs:
                b_out.append(natten(z, 2 * aw // LANES, row0, batch, seq, b_q_gain[i], b_k_gain[i], tab))
                row0 += batch * seq
            x = matmul([[a_out], b_out], [(ev_w_out[i], 0), (ev_w_out[i], aw)], residuals=xs)
        else:
            qkv = matmul([[h]], [(od_w_in[i], 0)])
            tabs = [dilated_bias_table(t5_table, dil) for _, dil in C_PATTERNS]
            c_out, row0 = [], 0
            for batch, seq in segs:
                c_out.append(dilated_attention(qkv, row0, batch, seq, c_q_gain[i], c_k_gain[i], tabs))
                row0 += batch * seq
            x = matmul([c_out], [(od_w_out[i], 0)], residuals=xs)
        n0 = segs[0][0] * segs[0][1]
        res = hier_moe_residual(x, norm_ffn[l], moe_wr_g[l], moe_br_g[l], moe_wr_e[l], moe_br_e[l],
                                moe_w_gate, moe_w_up, moe_w_down, l,
                                out_segments=[(0, n0), (n0, segs[1][0] * segs[1][1])] if last else None,
                                next_gain=None if last else norm_mix[l + 1])
        if last:
            xs = res
        else:
            xs, h = [res[0]], res[1]
    return (xs[0].reshape(x_prompt.shape), xs[1].reshape(x_sample.shape))
```

```python
import functools
import math

import jax
import jax.numpy as jnp
from jax import lax
from jax.experimental import pallas as pl
from jax.experimental.pallas import tpu as pltpu

F32 = jnp.float32
BF16 = jnp.bfloat16
EPS = 1e-6
NEG = -1e30

V7X_VMEM_LIMIT_BYTES = 56 * 1024 * 1024
LANES = 128

GRID_W = 64
CHUNK = 128
A_GROUPS = 8
NA_KH = 8
NA_KW = 16
B_DH = 64
C_DH = 128
C_PATTERNS = ((128, 1), (512, 4), (2048, 16))
C_RAD = 64
C_TQ = 128
C_TK = C_TQ + 2 * C_RAD
C_SUPER = C_TQ * 16
T5_BUCKETS = 32
T5_MAX_DIST = 1024
MOE_GROUPS = 4
MOE_PER_GROUP = 8
MOE_EXPERTS = MOE_GROUPS * MOE_PER_GROUP
MOE_TOPK = 2
MOE_TM = 256
ROUTER_LANES = 128
PREP_UNROLL = 8
DMA_PRIORITY_LEVELS = 2


def _params(*sem):
    return pltpu.CompilerParams(dimension_semantics=sem, vmem_limit_bytes=V7X_VMEM_LIMIT_BYTES)


def _row_sources(arrays, tm, width, col_of):
    specs, spans, off = [], [], 0
    for a in arrays:
        assert a.shape[0] % tm == 0 and a.shape[0] >= tm
        nblk = a.shape[0] // tm
        specs.append(pl.BlockSpec(
            (tm, width), lambda *g, off=off, nblk=nblk: (jnp.clip(g[-1] - off, 0, nblk - 1), col_of(*g))))
        spans.append((off, nblk))
        off += nblk
    return specs, spans


def _active_rows(i, refs, spans):
    val = refs[-1][...]
    for ref, (off, nblk) in reversed(list(zip(refs[:-1], spans[:-1]))):
        val = jnp.where(i < off + nblk, ref[...], val)
    return val


def _rmsnorm_kernel(*refs, spans):
    x_refs, (g_ref, o_ref) = refs[:len(spans)], refs[len(spans):]
    x = _active_rows(pl.program_id(0), x_refs, spans)
    y = x * lax.rsqrt(jnp.mean(x * x, axis=-1, keepdims=True) + EPS)
    o_ref[...] = (y * g_ref[...]).astype(o_ref.dtype)


def rmsnorm(xs, g, out_dtype, tm=1024):
    d = xs[0].shape[1]
    n = sum(x.shape[0] for x in xs)
    specs, spans = _row_sources(xs, tm, d, lambda i: 0)
    return pl.pallas_call(
        functools.partial(_rmsnorm_kernel, spans=spans),
        grid=(n // tm,),
        in_specs=specs + [pl.BlockSpec((1, d), lambda i: (0, 0))],
        out_specs=pl.BlockSpec((tm, d), lambda i: (i, 0)),
        out_shape=jax.ShapeDtypeStruct((n, d), out_dtype),
        compiler_params=_params("arbitrary"),
        name="rmsnorm",
    )(*xs, g.reshape(1, d))


def _matmul_kernel(*refs, x_spans, res_spans, k_chunk):
    refs = list(refs)
    x_refs = [[refs.pop(0) for _ in spans] for spans in x_spans]
    w_refs = [refs.pop(0) for _ in x_spans]
    res_refs = [refs.pop(0) for _ in res_spans]
    o_ref, wb_refs = refs[0], refs[1:]
    i = pl.program_id(1)

    @pl.when(pl.program_id(1) == 0)
    def _():
        for w_ref, wb_ref in zip(w_refs, wb_refs):
            rows = math.gcd(k_chunk, w_ref.shape[0])

            def cast(c, carry, w_ref=w_ref, wb_ref=wb_ref, rows=rows):
                sl = pl.ds(pl.multiple_of(c * rows, rows), rows)
                wb_ref[sl, :] = w_ref[sl, :].astype(BF16)
                return carry
            lax.fori_loop(0, w_ref.shape[0] // rows, cast, 0, unroll=True)

    acc = None
    for pieces, spans, wb_ref in zip(x_refs, x_spans, wb_refs):
        part = jnp.dot(_active_rows(i, pieces, spans), wb_ref[...], preferred_element_type=F32)
        acc = part if acc is None else acc + part
    if res_spans:
        acc = _active_rows(i, res_refs, res_spans) + acc
    o_ref[...] = acc.astype(o_ref.dtype)


def _matmul_row_tile(xs, residuals, tn, candidates=(1024, 512, 256)):
    ks = [pieces[0].shape[1] for pieces in xs]
    weights = sum(2 * k * tn * 4 + k * tn * 2 for k in ks)
    for tm in candidates:
        rows_ok = all(p.shape[0] % tm == 0 for pieces in xs for p in pieces) and \
            all(r.shape[0] % tm == 0 for r in residuals)
        blocks = sum(2 * len(pieces) * tm * k * 2 for pieces, k in zip(xs, ks))
        blocks += (2 + 2 * len(residuals) + 1) * tm * tn * 4
        if rows_ok and weights + blocks <= 0.8 * V7X_VMEM_LIMIT_BYTES:
            return tm
    raise ValueError("no row tile fits")


def matmul(xs, ws, residuals=(), out_dtype=F32, tn=1024):
    n = sum(x.shape[0] for x in xs[0])
    m = ws[0][0].shape[1]
    tn = min(tn, m)
    tm = _matmul_row_tile(xs, residuals, tn)
    in_specs, x_spans, w_specs = [], [], []
    for pieces, (w, row) in zip(xs, ws):
        k = pieces[0].shape[1]
        assert row % k == 0
        specs, spans = _row_sources(pieces, tm, k, lambda j, i: 0)
        in_specs += specs
        x_spans.append(spans)
        w_specs.append(pl.BlockSpec((k, tn), lambda j, i, rb=row // k: (rb, j)))
    res_specs, res_spans = _row_sources(list(residuals), tm, tn, lambda j, i: j)
    kern = functools.partial(_matmul_kernel, x_spans=x_spans, res_spans=res_spans, k_chunk=256)
    return pl.pallas_call(
        kern,
        grid=(m // tn, n // tm),
        in_specs=in_specs + w_specs + res_specs,
        out_specs=pl.BlockSpec((tm, tn), lambda j, i: (i, j)),
        out_shape=jax.ShapeDtypeStruct((n, m), out_dtype),
        scratch_shapes=[pltpu.VMEM((pieces[0].shape[1], tn), BF16) for pieces in xs],
        compiler_params=_params("arbitrary", "arbitrary"),
        name="matmul",
    )(*[p for pieces in xs for p in pieces], *[w for w, _ in ws], *residuals)


def _mixer_a_kernel(u_ref, v_ref, lng_ref, lnb_ref, ws_ref, bs_ref, o_ref):
    tm = u_ref.shape[0]
    gd = u_ref.shape[1] // A_GROUPS
    for c in range(tm // CHUNK):
        rows = slice(c * CHUNK, (c + 1) * CHUNK)
        u = jax.nn.gelu(u_ref[rows, :])
        v = jax.nn.gelu(v_ref[rows, :])
        mu = jnp.mean(v, axis=-1, keepdims=True)
        vc = v - mu
        var = jnp.mean(vc * vc, axis=-1, keepdims=True)
        v = vc * lax.rsqrt(var + EPS) * lng_ref[...] + lnb_ref[...]
        for g in range(A_GROUPS):
            cols = slice(g * gd, (g + 1) * gd)
            f = jnp.dot(ws_ref[g].astype(BF16), v[:, cols].astype(BF16), preferred_element_type=F32)
            f = f + bs_ref[g]
            o_ref[rows, cols] = (u[:, cols] * f).astype(o_ref.dtype)


def mixer_a(z, ln_g, ln_b, w_s, b_s, tm=512):
    n = z.shape[0]
    aw = ln_g.shape[0]
    gd = aw // A_GROUPS
    bs_b = jnp.broadcast_to(b_s[:, :, None], (A_GROUPS, CHUNK, gd))
    return pl.pallas_call(
        _mixer_a_kernel,
        grid=(n // tm,),
        in_specs=[
            pl.BlockSpec((tm, aw), lambda i: (i, 0)),
            pl.BlockSpec((tm, aw), lambda i: (i, 1)),
            pl.BlockSpec((1, aw), lambda i: (0, 0)),
            pl.BlockSpec((1, aw), lambda i: (0, 0)),
            pl.BlockSpec((A_GROUPS, CHUNK, CHUNK), lambda i: (0, 0, 0)),
            pl.BlockSpec((A_GROUPS, CHUNK, gd), lambda i: (0, 0, 0)),
        ],
        out_specs=pl.BlockSpec((tm, aw), lambda i: (i, 0)),
        out_shape=jax.ShapeDtypeStruct((n, aw), BF16),
        compiler_params=_params("parallel"),
        name="mixer_a",
    )(z, z, ln_g.reshape(1, aw), ln_b.reshape(1, aw), w_s, bs_b)


def _skew(w, rows):
    n = w.shape[-1]
    flat = jnp.tile(w, (1,) * (w.ndim - 1) + (rows,))
    return flat[..., :rows * (n - 1)].reshape(w.shape[:-1] + (rows, n - 1))


def _pair_rmsnorm(x, gain):
    sq = x * x
    lane = lax.broadcasted_iota(jnp.int32, x.shape, 1)
    lo = lane < B_DH
    s_lo = jnp.sum(jnp.where(lo, sq, 0.0), axis=-1, keepdims=True)
    s_hi = jnp.sum(jnp.where(lo, 0.0, sq), axis=-1, keepdims=True)
    ms = jnp.where(lo, s_lo, s_hi) * (1.0 / B_DH)
    return x * lax.rsqrt(ms + EPS) * gain


def _natten_kernel(q_ref, k_ref, v_ref, gq_ref, gk_ref, bias_ref, o_ref, kn_ref, vb_ref, s_ref, m_ref, e_ref,
                   *, rows_per_step, rows):
    seq = k_ref.shape[0]
    t = pl.program_id(2)
    prep = 256

    @pl.when(t == 0)
    def _():
        def body(c, carry):
            sl = pl.ds(pl.multiple_of(c * prep, prep), prep)
            kn_ref[sl, :] = _pair_rmsnorm(k_ref[sl, :], gk_ref[...]).astype(BF16)
            vb_ref[sl, :LANES] = v_ref[sl, :].astype(BF16)
            vb_ref[sl, LANES:] = jnp.ones((prep, LANES), BF16)
            return carry
        lax.fori_loop(0, seq // prep, body, 0, unroll=PREP_UNROLL)

    q = (_pair_rmsnorm(q_ref[...], gq_ref[...]) * (B_DH ** -0.5)).astype(BF16)
    lo = lax.broadcasted_iota(jnp.int32, (GRID_W, LANES), 1) < B_DH
    zero = jnp.zeros((GRID_W, LANES), BF16)
    win = NA_KH * GRID_W

    def window(rr):
        r = t * rows_per_step + rr
        rs = jnp.clip(r - NA_KH // 2, 0, rows - NA_KH)
        return r - rs, pl.ds(pl.multiple_of(rs * GRID_W, GRID_W), win)

    for rr in range(rows_per_step):
        case, ksl = window(rr)
        qr = q[rr * GRID_W:(rr + 1) * GRID_W, :]
        q2 = jnp.concatenate([jnp.where(lo, qr, zero), jnp.where(lo, zero, qr)], axis=0)
        s = lax.dot_general(q2, kn_ref[ksl, :], (((1,), (1,)), ((), ())), preferred_element_type=F32)
        s_ref[rr] = s + bias_ref[0, case]
    for rr in range(rows_per_step):
        m = jnp.max(s_ref[rr], axis=-1, keepdims=True)
        m_ref[rr] = jnp.broadcast_to(m, (2 * GRID_W, LANES))
    for rr in range(rows_per_step):
        mb = m_ref[rr]
        for c in range(win // LANES):
            cols = slice(c * LANES, (c + 1) * LANES)
            e_ref[rr, :, cols] = jnp.exp(s_ref[rr, :, cols] - mb).astype(BF16)
    for rr in range(rows_per_step):
        _, ksl = window(rr)
        ol = jnp.dot(e_ref[rr], vb_ref[ksl, :], preferred_element_type=F32)
        o = ol[:, :LANES] / ol[:, LANES:]
        o_ref[rr * GRID_W:(rr + 1) * GRID_W, :] = jnp.where(lo, o[:GRID_W], o[GRID_W:]).astype(o_ref.dtype)


def natten_bias_table(rpb):
    heads = rpb.shape[0]
    col = jnp.arange(GRID_W)
    col_start = jnp.clip(col - NA_KW // 2, 0, GRID_W - NA_KW)
    col_valid = (col[None, :] >= col_start[:, None]) & (col[None, :] < col_start[:, None] + NA_KW)
    edge = GRID_W - NA_KW
    w = jnp.concatenate([jnp.repeat(rpb[..., :1], edge, axis=-1), rpb.astype(F32),
                         jnp.repeat(rpb[..., -1:], edge, axis=-1), jnp.zeros(rpb.shape[:-1] + (1,), F32)], axis=-1)
    bias_c = _skew(w, GRID_W)[..., GRID_W - 1:]
    tab = jnp.stack([bias_c[:, NA_KH - 1 - c:2 * NA_KH - 1 - c] for c in range(NA_KH)], axis=1)
    tab = jnp.where(col_valid[:, None, :], tab.transpose(0, 1, 3, 2, 4), NEG)
    tab = tab.reshape(heads // 2, 2, NA_KH, GRID_W, NA_KH * GRID_W).transpose(0, 2, 1, 3, 4)
    return tab.reshape(heads // 2, NA_KH, 2 * GRID_W, NA_KH * GRID_W)


def natten(z, col0, row0, batch, seq, gq, gk, bias_tab, rows_per_step=16):
    hp = bias_tab.shape[0]
    rows = seq // GRID_W
    assert rows >= NA_KH and rows % rows_per_step == 0 and row0 % seq == 0
    tq = rows_per_step * GRID_W
    steps = rows // rows_per_step
    qb0 = row0 // tq
    sb0 = row0 // seq
    win = NA_KH * GRID_W
    kern = functools.partial(_natten_kernel, rows_per_step=rows_per_step, rows=rows)
    gq2 = jnp.tile(gq, 2).reshape(1, 2 * B_DH)
    gk2 = jnp.tile(gk, 2).reshape(1, 2 * B_DH)
    return pl.pallas_call(
        kern,
        grid=(batch, hp, steps),
        in_specs=[
            pl.BlockSpec((tq, LANES), lambda b, h, t: (qb0 + b * steps + t, col0 + h)),
            pl.BlockSpec((seq, LANES), lambda b, h, t: (sb0 + b, col0 + hp + h)),
            pl.BlockSpec((seq, LANES), lambda b, h, t: (sb0 + b, col0 + 2 * hp + h)),
            pl.BlockSpec((1, LANES), lambda b, h, t: (0, 0)),
            pl.BlockSpec((1, LANES), lambda b, h, t: (0, 0)),
            pl.BlockSpec((1, NA_KH, 2 * GRID_W, win), lambda b, h, t: (h, 0, 0, 0)),
        ],
        out_specs=pl.BlockSpec((tq, LANES), lambda b, h, t: (b * steps + t, h)),
        out_shape=jax.ShapeDtypeStruct((batch * seq, 2 * hp * B_DH), BF16),
        scratch_shapes=[
            pltpu.VMEM((seq, LANES), BF16),
            pltpu.VMEM((seq, 2 * LANES), BF16),
            pltpu.VMEM((rows_per_step, 2 * GRID_W, win), F32),
            pltpu.VMEM((rows_per_step, 2 * GRID_W, LANES), F32),
            pltpu.VMEM((rows_per_step, 2 * GRID_W, win), BF16),
        ],
        compiler_params=_params("parallel", "parallel", "arbitrary"),
        name="natten",
    )(z, z, z, gq2, gk2, bias_tab)


def t5_bucket(rel):
    nb = T5_BUCKETS // 2
    max_exact = nb // 2
    ret = jnp.where(rel > 0, nb, 0)
    n = jnp.abs(rel)
    large = max_exact + (jnp.log(jnp.maximum(n, 1).astype(F32) / max_exact)
                         / math.log(T5_MAX_DIST / max_exact) * (nb - max_exact)).astype(jnp.int32)
    large = jnp.minimum(large, nb - 1)
    return ret + jnp.where(n < max_exact, n, large)


def dilated_bias_table(t5_table, dil):
    span = C_TK - 1
    rel = jnp.arange(-span, span + 1)
    vec = t5_table.astype(F32)[t5_bucket(rel * dil)]
    vec = jnp.where((jnp.abs(rel) <= C_RAD)[:, None], vec, NEG).T
    w = jnp.concatenate([vec, jnp.zeros((vec.shape[0], 1), F32)], axis=1)
    r = _skew(w, C_TQ)
    return jnp.stack([r[:, :, span - off:span - off + C_TK] for off in (0, C_RAD, 2 * C_RAD)], axis=1)


def _row_rmsnorm(x, gain):
    return x * lax.rsqrt(jnp.mean(x * x, axis=-1, keepdims=True) + EPS) * gain


def _dilated_kernel(q_ref, k_ref, v_ref, gq_ref, gk_ref, b0_ref, b1_ref, b2_ref, o_ref,
                    kn_ref, qn_ref, oacc_ref, lse_ref, s_ref, m_ref, e_ref, *kv_by_residue, seq, unroll):
    t = pl.program_id(2)
    prep = 256
    kv_sub = {dil: (kv_by_residue[2 * i], kv_by_residue[2 * i + 1])
              for i, dil in enumerate(d for _, d in C_PATTERNS if d > 1)}

    @pl.when(t == 0)
    def _():
        def body(c, carry):
            sl = pl.ds(pl.multiple_of(c * prep, prep), prep)
            kn_ref[sl, :] = _row_rmsnorm(k_ref[sl, :], gk_ref[...])
            return carry
        lax.fori_loop(0, seq // prep, body, 0, unroll=PREP_UNROLL)

        for dil, (ks_ref, vs_ref) in kv_sub.items():
            rows = min(prep, seq // dil)
            for s in range(dil):
                def split(c, carry, dil=dil, s=s, rows=rows, ks_ref=ks_ref, vs_ref=vs_ref):
                    src = pl.ds(c * rows * dil + s, rows, stride=dil)
                    dst = pl.ds(pl.multiple_of(c * rows, rows), rows)
                    ks_ref[s, dst, :] = kn_ref[src, :].astype(BF16)
                    vs_ref[s, dst, :] = v_ref[src, :].astype(BF16)
                    return carry
                lax.fori_loop(0, seq // (dil * rows), split, 0, unroll=True)

    qn_ref[...] = _row_rmsnorm(q_ref[...], gq_ref[...]) * (C_DH ** -0.5)
    n_tiles = C_SUPER // C_TQ

    for p, ((window, dil), b_ref) in enumerate(zip(C_PATTERNS, (b0_ref, b1_ref, b2_ref))):
        n_sub = seq // dil
        tk = min(C_TK, n_sub)
        span = C_TQ * dil
        ones = jnp.ones((tk, LANES), BF16)

        def tile(idx, dil=dil, n_sub=n_sub, tk=tk, span=span):
            u = idx // dil
            s = idx % dil
            qpos = u * span + s
            j0 = t * (C_SUPER // dil) + u * C_TQ
            ws = jnp.clip(j0 - C_RAD, 0, n_sub - tk)
            ksl = pl.ds(pl.multiple_of(ws, C_RAD), tk)
            if dil == 1:
                return pl.ds(pl.multiple_of(qpos, C_TQ), C_TQ), s, ksl, (j0 - ws) // C_RAD
            return pl.ds(qpos, C_TQ, stride=dil), s, ksl, (j0 - ws) // C_RAD

        def keys(s, ksl, dil=dil):
            return kn_ref[ksl, :].astype(BF16) if dil == 1 else kv_sub[dil][0][s, ksl, :]

        def values(s, ksl, dil=dil):
            return v_ref[ksl, :].astype(BF16) if dil == 1 else kv_sub[dil][1][s, ksl, :]

        def scores(idx, carry, tile=tile, keys=keys, tk=tk, b_ref=b_ref):
            qsl, s, ksl, case = tile(idx)
            q = qn_ref[qsl, :].astype(BF16)
            sc = lax.dot_general(q, keys(s, ksl), (((1,), (1,)), ((), ())), preferred_element_type=F32)
            s_ref[idx, :, :tk] = sc + b_ref[0, case, :, :tk]
            return carry

        def row_max(idx, carry, tk=tk):
            m = jnp.max(s_ref[idx, :, :tk], axis=-1, keepdims=True)
            m_ref[idx] = jnp.broadcast_to(m, (C_TQ, LANES))
            return carry

        def exps(idx, carry, tk=tk):
            mb = m_ref[idx]
            for c in range(tk // LANES):
                cols = slice(c * LANES, (c + 1) * LANES)
                e_ref[idx, :, cols] = jnp.exp(s_ref[idx, :, cols] - mb).astype(BF16)
            return carry

        def outputs(idx, carry, tile=tile, values=values, tk=tk, ones=ones, p=p):
            qsl, s, ksl, _ = tile(idx)
            v1 = jnp.concatenate([values(s, ksl), ones], axis=1)
            ol = jnp.dot(e_ref[idx, :, :tk], v1, preferred_element_type=F32)
            l = ol[:, LANES:]
            oacc_ref[p, qsl, :] = ol[:, :LANES] / l
            lse_ref[p, qsl, :] = m_ref[idx] + jnp.log(l)
            return carry

        for phase in (scores, row_max, exps, outputs):
            lax.fori_loop(0, n_tiles, phase, 0, unroll=unroll)

    l0, l1, l2 = lse_ref[0], lse_ref[1], lse_ref[2]
    mx = jnp.maximum(jnp.maximum(l0, l1), l2)
    w0, w1, w2 = jnp.exp(l0 - mx), jnp.exp(l1 - mx), jnp.exp(l2 - mx)
    num = w0 * oacc_ref[0] + w1 * oacc_ref[1] + w2 * oacc_ref[2]
    o_ref[...] = (num / (w0 + w1 + w2)).astype(o_ref.dtype)


def dilated_attention(qkv, row0, batch, seq, gq, gk, tabs, unroll=16):
    heads = tabs[0].shape[0]
    assert seq % C_SUPER == 0 and row0 % seq == 0
    steps = seq // C_SUPER
    qb0 = row0 // C_SUPER
    sb0 = row0 // seq
    n_tiles = C_SUPER // C_TQ
    kern = functools.partial(_dilated_kernel, seq=seq, unroll=unroll)
    tab_specs = [pl.BlockSpec((1,) + tuple(tb.shape[1:]), lambda b, h, t: (h, 0, 0, 0)) for tb in tabs]
    return pl.pallas_call(
        kern,
        grid=(batch, heads, steps),
        in_specs=[
            pl.BlockSpec((C_SUPER, LANES), lambda b, h, t: (qb0 + b * steps + t, h)),
            pl.BlockSpec((seq, LANES), lambda b, h, t: (sb0 + b, heads + h)),
            pl.BlockSpec((seq, LANES), lambda b, h, t: (sb0 + b, 2 * heads + h)),
            pl.BlockSpec((1, LANES), lambda b, h, t: (0, 0)),
            pl.BlockSpec((1, LANES), lambda b, h, t: (0, 0)),
        ] + tab_specs,
        out_specs=pl.BlockSpec((C_SUPER, LANES), lambda b, h, t: (b * steps + t, h)),
        out_shape=jax.ShapeDtypeStruct((batch * seq, heads * C_DH), BF16),
        scratch_shapes=[
            pltpu.VMEM((seq, LANES), F32),
            pltpu.VMEM((C_SUPER, LANES), F32),
            pltpu.VMEM((len(C_PATTERNS), C_SUPER, LANES), F32),
            pltpu.VMEM((len(C_PATTERNS), C_SUPER, LANES), F32),
            pltpu.VMEM((n_tiles, C_TQ, C_TK), F32),
            pltpu.VMEM((n_tiles, C_TQ, LANES), F32),
            pltpu.VMEM((n_tiles, C_TQ, C_TK), BF16),
        ] + [pltpu.VMEM((dil, seq // dil, LANES), BF16) for _, dil in C_PATTERNS if dil > 1 for _ in "kv"],
        compiler_params=_params("parallel", "parallel", "arbitrary"),
        name="dilated",
    )(qkv, qkv, qkv, gq.reshape(1, C_DH), gk.reshape(1, C_DH), *tabs)


def _pack_bf16_pairs(lo, hi):
    lo_bits = lax.bitcast_convert_type(lo.astype(BF16).astype(F32), jnp.uint32)
    hi_bits = lax.bitcast_convert_type(hi.astype(BF16).astype(F32), jnp.uint32)
    return (hi_bits & jnp.uint32(0xFFFF0000)) | (lo_bits >> 16)


def _unpack_bf16_pairs(packed):
    lo = lax.bitcast_convert_type(packed << 16, F32)
    hi = lax.bitcast_convert_type(packed & jnp.uint32(0xFFFF0000), F32)
    return lo, hi


META_E0, META_E1, META_R0, META_R1, META_G0, META_G1 = range(6)


def _route_kernel(x_ref, g_ref, wr_ref, br_ref, hp_ref, meta_ref, tot_ref):
    x = x_ref[...]
    h = x * lax.rsqrt(jnp.mean(x * x, axis=-1, keepdims=True) + EPS) * g_ref[...]
    half = h.shape[1] // 2
    hp_ref[...] = _pack_bf16_pairs(h[:, :half], h[:, half:])
    h_hi = h.astype(BF16)
    h_lo = (h - h_hi.astype(F32)).astype(BF16)
    w_hi = wr_ref[...].astype(BF16)
    w_lo = (wr_ref[...] - w_hi.astype(F32)).astype(BF16)
    logits = (jnp.dot(h_hi, w_hi, preferred_element_type=F32) + jnp.dot(h_hi, w_lo, preferred_element_type=F32)
              + jnp.dot(h_lo, w_hi, preferred_element_type=F32)) + br_ref[...]
    tm = x.shape[0]
    lane = lax.broadcasted_iota(jnp.int32, (tm, ROUTER_LANES), 1)
    lane_f = lane.astype(F32)

    def first_max(v):
        mx = jnp.max(v, axis=-1, keepdims=True)
        return mx, jnp.min(jnp.where(v == mx, lane_f, float(ROUTER_LANES)), axis=-1, keepdims=True)

    lg = jnp.where(lane < MOE_GROUPS, logits, -jnp.inf)
    gmax, gsel = first_max(lg)
    p_g = 1.0 / jnp.sum(jnp.exp(lg - gmax), axis=-1, keepdims=True)
    expert_lane = (lane >= MOE_GROUPS) & (lane < MOE_GROUPS + MOE_EXPERTS)
    in_group = expert_lane & (((lane - MOE_GROUPS) // MOE_PER_GROUP) == gsel.astype(jnp.int32))
    le = jnp.where(in_group, logits, -jnp.inf)
    v0, i0 = first_max(le)
    pick0 = lane_f == i0
    v1, i1 = first_max(jnp.where(pick0, -jnp.inf, le))
    pick1 = lane_f == i1
    t = jnp.exp(v1 - v0)
    g0 = p_g / (1.0 + t)
    g1 = p_g * t / (1.0 + t)

    onehot = jnp.where(pick0 | pick1, 1.0, 0.0).astype(BF16)
    row = lax.broadcasted_iota(jnp.int32, (tm, tm), 0)
    col = lax.broadcasted_iota(jnp.int32, (tm, tm), 1)
    earlier = jnp.where(col < row, 1.0, 0.0).astype(BF16)
    before = jnp.dot(earlier, onehot, preferred_element_type=F32)
    r0 = jnp.sum(jnp.where(pick0, before, 0.0), axis=-1, keepdims=True)
    r1 = jnp.sum(jnp.where(pick1, before, 0.0), axis=-1, keepdims=True)

    meta = jnp.zeros((tm, ROUTER_LANES), F32)
    for pos, val in ((META_E0, i0 - MOE_GROUPS), (META_E1, i1 - MOE_GROUPS), (META_R0, r0), (META_R1, r1),
                     (META_G0, g0), (META_G1, g1)):
        meta = jnp.where(lane == pos, val, meta)
    meta_ref[...] = meta
    tot_ref[0] = jnp.sum(onehot.astype(F32), axis=0, keepdims=True)


def moe_route(x, g, wr, br, tm):
    n, d = x.shape
    return pl.pallas_call(
        _route_kernel,
        grid=(n // tm,),
        in_specs=[
            pl.BlockSpec((tm, d), lambda i: (i, 0)),
            pl.BlockSpec((1, d), lambda i: (0, 0)),
            pl.BlockSpec((d, ROUTER_LANES), lambda i: (0, 0)),
            pl.BlockSpec((1, ROUTER_LANES), lambda i: (0, 0)),
        ],
        out_specs=[
            pl.BlockSpec((tm, d // 2), lambda i: (i, 0)),
            pl.BlockSpec((tm, ROUTER_LANES), lambda i: (i, 0)),
            pl.BlockSpec((1, 1, ROUTER_LANES), lambda i: (i, 0, 0)),
        ],
        out_shape=[
            jax.ShapeDtypeStruct((n, d // 2), jnp.uint32),
            jax.ShapeDtypeStruct((n, ROUTER_LANES), F32),
            jax.ShapeDtypeStruct((n // tm, 1, ROUTER_LANES), F32),
        ],
        compiler_params=_params("parallel"),
        name="moe_route",
    )(x, g.reshape(1, d), wr, br)


def _dispatch_kernel(d0_ref, d1_ref, pad_first_ref, pad_cnt_ref, nused_ref, h_ref, xb_hbm, sem):
    i = pl.program_id(0)
    tm = h_ref.shape[0]
    base = i * tm

    def row_copy(src_row, dst_row, level=0):
        return pltpu.make_async_copy(h_ref.at[pl.ds(src_row, 1), :], xb_hbm.at[pl.ds(dst_row, 1), :], sem.at[level])

    for r in range(tm):
        row_copy(r, d0_ref[base + r], 0).start(priority=(2 * (r % 2)) % DMA_PRIORITY_LEVELS)
        row_copy(r, d1_ref[base + r], 1).start(priority=(2 * (r % 2) + 1) % DMA_PRIORITY_LEVELS)

    def drain(r, carry):
        row_copy(0, 0, 0).wait()
        row_copy(0, 0, 1).wait()
        return carry
    lax.fori_loop(0, tm, drain, 0, unroll=8)

    @pl.when(i == pl.num_programs(0) - 1)
    def _():
        def per_expert(e, carry):
            def fill(j, c):
                row_copy(0, pad_first_ref[e] + j).start()
                return c
            lax.fori_loop(0, pad_cnt_ref[e], fill, 0)

            def fill_wait(j, c):
                row_copy(0, 0).wait()
                return c
            lax.fori_loop(0, pad_cnt_ref[e], fill_wait, 0)
            return carry
        lax.fori_loop(0, MOE_EXPERTS, per_expert, 0)

        def block_copy(b):
            return pltpu.make_async_copy(h_ref, xb_hbm.at[pl.ds(pl.multiple_of(b * tm, tm), tm), :], sem.at[0])

        def fill_block(b, c):
            block_copy(b).start()
            return c
        lax.fori_loop(nused_ref[0], xb_hbm.shape[0] // tm, fill_block, 0)

        def fill_block_wait(b, c):
            block_copy(b).wait()
            return c
        lax.fori_loop(nused_ref[0], xb_hbm.shape[0] // tm, fill_block_wait, 0)


def moe_dispatch(hp, d0, d1, pad_first, pad_cnt, n_used, p, tm=MOE_TM):
    n, dw = hp.shape
    return pl.pallas_call(
        _dispatch_kernel,
        grid_spec=pltpu.PrefetchScalarGridSpec(
            num_scalar_prefetch=5,
            grid=(n // tm,),
            in_specs=[pl.BlockSpec((tm, dw), lambda i, *_: (i, 0))],
            out_specs=pl.BlockSpec(memory_space=pl.ANY),
            scratch_shapes=[pltpu.SemaphoreType.DMA((2,))],
        ),
        out_shape=jax.ShapeDtypeStruct((p, dw), hp.dtype),
        compiler_params=_params("arbitrary"),
        name="moe_dispatch",
    )(d0, d1, pad_first, pad_cnt, n_used, hp)


def _row_gather_copy(src_hbm, dst_ref, sem, src_row, dst_row):
    return pltpu.make_async_copy(src_hbm.at[pl.ds(src_row, 1), :], dst_ref.at[pl.ds(dst_row, 1), :], sem)


def _gather_row_sets(gathers, base, src_hbm, unroll=8):
    rows = gathers[0][1].shape[0]
    for r in range(rows):
        for k, (idx_ref, dst_ref, sem) in enumerate(gathers):
            level = (len(gathers) * (r % 2) + k) % DMA_PRIORITY_LEVELS
            _row_gather_copy(src_hbm, dst_ref, sem, idx_ref[base + r], r).start(priority=level)

    def drain(r, carry):
        for _, dst_ref, sem in gathers:
            _row_gather_copy(src_hbm, dst_ref, sem, 0, r).wait()
        return carry
    lax.fori_loop(0, rows, drain, 0, unroll=unroll)


def _expert_weight_copies(layer, e, w_hbm, stage, sem):
    return [pltpu.make_async_copy(w.at[layer, e], s, sem.at[k]) for k, (w, s) in enumerate(zip(w_hbm, stage))]


def _ffn_kernel(be_ref, first_ref, next_ref, nused_ref, x_ref, wg_hbm, wu_hbm, wd_hbm, o_ref,
                sg_ref, su_ref, sd_ref, wg_ref, wu_ref, wd_ref, sem, *, layer, k_chunk):
    i = pl.program_id(0)
    w_hbm = (wg_hbm, wu_hbm, wd_hbm)
    stage = (sg_ref, su_ref, sd_ref)
    resident = (wg_ref, wu_ref, wd_ref)

    @pl.when(i == 0)
    def _():
        for cp in _expert_weight_copies(layer, be_ref[0], w_hbm, stage, sem):
            cp.start()

    @pl.when(first_ref[i] == 1)
    def _():
        for cp in _expert_weight_copies(layer, be_ref[i], w_hbm, stage, sem):
            cp.wait()
        for s_ref, w_ref in zip(stage, resident):
            rows = math.gcd(k_chunk, s_ref.shape[0])

            def cast(c, carry, s_ref=s_ref, w_ref=w_ref, rows=rows):
                sl = pl.ds(pl.multiple_of(c * rows, rows), rows)
                w_ref[sl, :] = s_ref[sl, :].astype(BF16)
                return carry
            lax.fori_loop(0, s_ref.shape[0] // rows, cast, 0, unroll=True)

        @pl.when(next_ref[i] >= 0)
        def _():
            for cp in _expert_weight_copies(layer, next_ref[i], w_hbm, stage, sem):
                cp.start()

    live = i < nused_ref[0]

    @pl.when(live)
    def _():
        lo, hi = _unpack_bf16_pairs(x_ref[...])
        lo, hi = lo.astype(BF16), hi.astype(BF16)
        half = lo.shape[1]
        g = (jnp.dot(lo, wg_ref[:half, :], preferred_element_type=F32)
             + jnp.dot(hi, wg_ref[half:, :], preferred_element_type=F32))
        u = (jnp.dot(lo, wu_ref[:half, :], preferred_element_type=F32)
             + jnp.dot(hi, wu_ref[half:, :], preferred_element_type=F32))
        h = (jax.nn.silu(g) * u).astype(BF16)
        y = jnp.dot(h, wd_ref[...], preferred_element_type=F32)
        o_ref[...] = _pack_bf16_pairs(y[:, :half], y[:, half:])

    @pl.when(jnp.logical_not(live))
    def _():
        o_ref[...] = jnp.zeros_like(o_ref)


def expert_ffn(xb, block_e, first, next_e, n_used, wg, wu, wd, layer, tm=MOE_TM):
    p, dw = xb.shape
    d, de = wg.shape[2], wg.shape[3]
    any_spec = pl.BlockSpec(memory_space=pl.ANY)
    return pl.pallas_call(
        functools.partial(_ffn_kernel, layer=layer, k_chunk=256),
        grid_spec=pltpu.PrefetchScalarGridSpec(
            num_scalar_prefetch=4,
            grid=(p // tm,),
            in_specs=[
                pl.BlockSpec((tm, dw), lambda i, be, fi, ne, nu: (jnp.minimum(i, nu[0] - 1), 0)),
                any_spec, any_spec, any_spec,
            ],
            out_specs=pl.BlockSpec((tm, dw), lambda i, *_: (i, 0)),
            scratch_shapes=[
                pltpu.VMEM((d, de), F32), pltpu.VMEM((d, de), F32), pltpu.VMEM((de, d), F32),
                pltpu.VMEM((d, de), BF16), pltpu.VMEM((d, de), BF16), pltpu.VMEM((de, d), BF16),
                pltpu.SemaphoreType.DMA((3,)),
            ],
        ),
        out_shape=jax.ShapeDtypeStruct((p, dw), jnp.uint32),
        compiler_params=_params("arbitrary"),
        name="moe_ffn",
    )(block_e, first, next_e, n_used, xb, wg, wu, wd)


def _combine_kernel(d0_ref, d1_ref, x_ref, meta_ref, yb_hbm, *rest, row0, with_norm):
    if with_norm:
        gain_ref, o_ref, h_ref, a_ref, b_ref, sem = rest
    else:
        o_ref, a_ref, b_ref, sem = rest
    base = row0 + pl.program_id(0) * x_ref.shape[0]
    _gather_row_sets([(d0_ref, a_ref, sem.at[0]), (d1_ref, b_ref, sem.at[1])], base, yb_hbm)
    a_lo, a_hi = _unpack_bf16_pairs(a_ref[...])
    b_lo, b_hi = _unpack_bf16_pairs(b_ref[...])
    g0 = meta_ref[:, META_G0:META_G0 + 1]
    g1 = meta_ref[:, META_G1:META_G1 + 1]
    half = a_lo.shape[1]
    lo = x_ref[:, :half] + (g0 * a_lo + g1 * b_lo)
    hi = x_ref[:, half:] + (g0 * a_hi + g1 * b_hi)
    o_ref[:, :half] = lo
    o_ref[:, half:] = hi
    if with_norm:
        ms = (jnp.sum(lo * lo, axis=-1, keepdims=True) + jnp.sum(hi * hi, axis=-1, keepdims=True)) / (2 * half)
        inv = lax.rsqrt(ms + EPS)
        h_ref[:, :half] = (lo * inv * gain_ref[:, :half]).astype(h_ref.dtype)
        h_ref[:, half:] = (hi * inv * gain_ref[:, half:]).astype(h_ref.dtype)


def moe_combine(x, meta, yb, d0, d1, row0, rows, next_gain=None, tm=256):
    d = x.shape[1]
    assert row0 % tm == 0
    blk0 = row0 // tm
    with_norm = next_gain is not None
    row_spec = pl.BlockSpec((tm, d), lambda i, a, b: (i, 0))
    in_specs = [
        pl.BlockSpec((tm, d), lambda i, a, b: (blk0 + i, 0)),
        pl.BlockSpec((tm, ROUTER_LANES), lambda i, a, b: (blk0 + i, 0)),
        pl.BlockSpec(memory_space=pl.ANY),
    ]
    args = [d0, d1, x, meta, yb]
    out_specs, out_shape = row_spec, jax.ShapeDtypeStruct((rows, d), F32)
    if with_norm:
        in_specs.append(pl.BlockSpec((1, d), lambda i, a, b: (0, 0)))
        args.append(next_gain.reshape(1, d))
        out_specs, out_shape = [row_spec, row_spec], [out_shape, jax.ShapeDtypeStruct((rows, d), BF16)]
    return pl.pallas_call(
        functools.partial(_combine_kernel, row0=row0, with_norm=with_norm),
        grid_spec=pltpu.PrefetchScalarGridSpec(
            num_scalar_prefetch=2,
            grid=(rows // tm,),
            in_specs=in_specs,
            out_specs=out_specs,
            scratch_shapes=[pltpu.VMEM((tm, d // 2), jnp.uint32), pltpu.VMEM((tm, d // 2), jnp.uint32),
                            pltpu.SemaphoreType.DMA((2,))],
        ),
        out_shape=out_shape,
        compiler_params=_params("arbitrary"),
        name="moe_combine",
    )(*args)


def moe_layout(meta, tot, tile, tm=MOE_TM):
    n = meta.shape[0]
    m = n * MOE_TOPK
    ids = jnp.arange(MOE_EXPERTS, dtype=jnp.int32)
    cnt = tot[:, 0, MOE_GROUPS:MOE_GROUPS + MOE_EXPERTS].astype(jnp.int32)
    counts = jnp.sum(cnt, axis=0)
    padded = (counts + tm - 1) // tm * tm
    pad_end = jnp.cumsum(padded)
    pad_start = pad_end - padded
    base = pad_start[None, :] + jnp.cumsum(cnt, axis=0) - cnt
    base_tok = jnp.repeat(base, tile, axis=0)

    def slots(e_col, r_col):
        e = meta[:, e_col].astype(jnp.int32)
        return jnp.sum(jnp.where(e[:, None] == ids[None, :], base_tok, 0), axis=1) + meta[:, r_col].astype(jnp.int32)

    d0 = slots(META_E0, META_R0)
    d1 = slots(META_E1, META_R1)
    n_blocks = (m + MOE_EXPERTS * (tm - 1) + tm - 1) // tm

    blk = jnp.arange(n_blocks, dtype=jnp.int32)
    n_used = (pad_end[-1] // tm).astype(jnp.int32)
    block_e = jnp.sum((pad_end[None, :] <= (blk * tm)[:, None]).astype(jnp.int32), axis=1)
    block_e = jnp.minimum(block_e, MOE_EXPERTS - 1)
    live = blk < n_used
    first = (live & ((blk == 0) | (block_e != jnp.roll(block_e, 1)))).astype(jnp.int32)
    has_rows = counts > 0
    later = jnp.arange(MOE_EXPERTS)[None, :] > jnp.arange(MOE_EXPERTS)[:, None]
    nxt_of_e = jnp.min(jnp.where(later & has_rows[None, :], jnp.arange(MOE_EXPERTS)[None, :], MOE_EXPERTS), axis=1)
    nxt_of_e = jnp.where(nxt_of_e == MOE_EXPERTS, -1, nxt_of_e).astype(jnp.int32)
    next_e = jnp.sum(jnp.where(block_e[:, None] == ids[None, :], nxt_of_e[None, :], 0), axis=1)
    pad_first = (pad_start + counts).astype(jnp.int32)
    pad_cnt = (padded - counts).astype(jnp.int32)
    return d0, d1, pad_first, pad_cnt, n_blocks * tm, block_e, first, next_e, n_used.reshape(1)


def hier_moe_residual(x, g, wr_g, br_g, wr_e, br_e, w_gate, w_up, w_down, layer, out_segments=None,
                      next_gain=None, tile=512):
    n, d = x.shape
    pad = ROUTER_LANES - MOE_GROUPS - MOE_EXPERTS
    wr = jnp.concatenate([wr_g, wr_e, jnp.zeros((d, pad), F32)], axis=1)
    br = jnp.concatenate([br_g, br_e, jnp.zeros((pad,), F32)]).reshape(1, ROUTER_LANES)
    hp, meta, tot = moe_route(x, g, wr, br, tile)
    d0, d1, pad_first, pad_cnt, p, block_e, first, next_e, n_used = moe_layout(meta, tot, tile)
    xb = moe_dispatch(hp, d0, d1, pad_first, pad_cnt, n_used, p)
    yb = expert_ffn(xb, block_e, first, next_e, n_used, w_gate, w_up, w_down, layer)
    if out_segments is None:
        return moe_combine(x, meta, yb, d0, d1, 0, n, next_gain)
    return [moe_combine(x, meta, yb, d0, d1, row0, rows) for row0, rows in out_segments]


def kernel(x_prompt, x_sample, norm_mix, norm_ffn, ev_w_in, ev_w_out, a_ln_g, a_ln_b, a_w_s, a_b_s, b_q_gain, b_k_gain, b_rpb, od_w_in, od_w_out, c_q_gain, c_k_gain, t5_table, moe_wr_g, moe_br_g, moe_wr_e, moe_br_e, moe_w_gate, moe_w_up, moe_w_down):
    d = x_prompt.shape[-1]
    segs = [(x_prompt.shape[0], x_prompt.shape[1]), (x_sample.shape[0], x_sample.shape[1])]
    xs = [x_prompt.reshape(-1, d), x_sample.reshape(-1, d)]
    depth = norm_mix.shape[0]
    h = rmsnorm(xs, norm_mix[0], BF16)
    for l in range(depth):
        i = l // 2
        last = l == depth - 1
        if l % 2 == 0:
            aw = a_ln_g.shape[1]
            z = matmul([[h]], [(ev_w_in[i], 0)])
            a_out = mixer_a(z, a_ln_g[i], a_ln_b[i], a_w_s[i], a_b_s[i])
            tab = natten_bias_table(b_rpb[i])
            b_out, row0 = [], 0
            for batch, seq in segs:
                b_out.append(natten(z, 2 * aw // LANES, row0, batch, seq, b_q_gain[i], b_k_gain[i], tab))
                row0 += batch * seq
            x = matmul([[a_out], b_out], [(ev_w_out[i], 0), (ev_w_out[i], aw)], residuals=xs)
        else:
            qkv = matmul([[h]], [(od_w_in[i], 0)])
            tabs = [dilated_bias_table(t5_table, dil) for _, dil in C_PATTERNS]
            c_out, row0 = [], 0
            for batch, seq in segs:
                c_out.append(dilated_attention(qkv, row0, batch, seq, c_q_gain[i], c_k_gain[i], tabs))
                row0 += batch * seq
            x = matmul([c_out], [(od_w_out[i], 0)], residuals=xs)
        n0 = segs[0][0] * segs[0][1]
        res = hier_moe_residual(x, norm_ffn[l], moe_wr_g[l], moe_br_g[l], moe_wr_e[l], moe_br_e[l],
                                moe_w_gate, moe_w_up, moe_w_down, l,
                                out_segments=[(0, n0), (n0, segs[1][0] * segs[1][1])] if last else None,
                                next_gain=None if last else norm_mix[l + 1])
        if last:
            xs = res
        else:
            xs, h = [res[0]], res[1]
    return (xs[0].reshape(x_prompt.shape), xs[1].reshape(x_sample.shape))
```

```python
import functools
import math

import jax
import jax.numpy as jnp
from jax import lax
from jax.experimental import pallas as pl
from jax.experimental.pallas import tpu as pltpu

F32 = jnp.float32
BF16 = jnp.bfloat16
EPS = 1e-6
NEG = -1e30

V7X_VMEM_LIMIT_BYTES = 56 * 1024 * 1024
LANES = 128

GRID_W = 64
CHUNK = 128
A_GROUPS = 8
NA_KH = 8
NA_KW = 16
B_DH = 64
C_DH = 128
C_PATTERNS = ((128, 1), (512, 4), (2048, 16))
C_RAD = 64
C_TQ = 128
C_TK = C_TQ + 2 * C_RAD
C_SUPER = C_TQ * 16
T5_BUCKETS = 32
T5_MAX_DIST = 1024
MOE_GROUPS = 4
MOE_PER_GROUP = 8
MOE_EXPERTS = MOE_GROUPS * MOE_PER_GROUP
MOE_TOPK = 2
MOE_TM = 256
ROUTER_LANES = 128
PREP_UNROLL = 8
DMA_PRIORITY_LEVELS = 2


def _params(*sem):
    return pltpu.CompilerParams(dimension_semantics=sem, vmem_limit_bytes=V7X_VMEM_LIMIT_BYTES)


def _row_sources(arrays, tm, width, col_of):
    specs, spans, off = [], [], 0
    for a in arrays:
        assert a.shape[0] % tm == 0 and a.shape[0] >= tm
        nblk = a.shape[0] // tm
        specs.append(pl.BlockSpec(
            (tm, width), lambda *g, off=off, nblk=nblk: (jnp.clip(g[-1] - off, 0, nblk - 1), col_of(*g))))
        spans.append((off, nblk))
        off += nblk
    return specs, spans


def _active_rows(i, refs, spans):
    val = refs[-1][...]
    for ref, (off, nblk) in reversed(list(zip(refs[:-1], spans[:-1]))):
        val = jnp.where(i < off + nblk, ref[...], val)
    return val


def _rmsnorm_kernel(*refs, spans):
    x_refs, (g_ref, o_ref) = refs[:len(spans)], refs[len(spans):]
    x = _active_rows(pl.program_id(0), x_refs, spans)
    y = x * lax.rsqrt(jnp.mean(x * x, axis=-1, keepdims=True) + EPS)
    o_ref[...] = (y * g_ref[...]).astype(o_ref.dtype)


def rmsnorm(xs, g, out_dtype, tm=1024):
    d = xs[0].shape[1]
    n = sum(x.shape[0] for x in xs)
    specs, spans = _row_sources(xs, tm, d, lambda i: 0)
    return pl.pallas_call(
        functools.partial(_rmsnorm_kernel, spans=spans),
        grid=(n // tm,),
        in_specs=specs + [pl.BlockSpec((1, d), lambda i: (0, 0))],
        out_specs=pl.BlockSpec((tm, d), lambda i: (i, 0)),
        out_shape=jax.ShapeDtypeStruct((n, d), out_dtype),
        compiler_params=_params("arbitrary"),
        name="rmsnorm",
    )(*xs, g.reshape(1, d))


def _matmul_kernel(*refs, x_spans, res_spans, k_chunk):
    refs = list(refs)
    x_refs = [[refs.pop(0) for _ in spans] for spans in x_spans]
    w_refs = [refs.pop(0) for _ in x_spans]
    res_refs = [refs.pop(0) for _ in res_spans]
    o_ref, wb_refs = refs[0], refs[1:]
    i = pl.program_id(1)

    @pl.when(pl.program_id(1) == 0)
    def _():
        for w_ref, wb_ref in zip(w_refs, wb_refs):
            rows = math.gcd(k_chunk, w_ref.shape[0])

            def cast(c, carry, w_ref=w_ref, wb_ref=wb_ref, rows=rows):
                sl = pl.ds(pl.multiple_of(c * rows, rows), rows)
                wb_ref[sl, :] = w_ref[sl, :].astype(BF16)
                return carry
            lax.fori_loop(0, w_ref.shape[0] // rows, cast, 0, unroll=True)

    acc = None
    for pieces, spans, wb_ref in zip(x_refs, x_spans, wb_refs):
        part = jnp.dot(_active_rows(i, pieces, spans), wb_ref[...], preferred_element_type=F32)
        acc = part if acc is None else acc + part
    if res_spans:
        acc = _active_rows(i, res_refs, res_spans) + acc
    o_ref[...] = acc.astype(o_ref.dtype)


def _matmul_row_tile(xs, residuals, tn, candidates=(1024, 512, 256)):
    ks = [pieces[0].shape[1] for pieces in xs]
    weights = sum(2 * k * tn * 4 + k * tn * 2 for k in ks)
    for tm in candidates:
        rows_ok = all(p.shape[0] % tm == 0 for pieces in xs for p in pieces) and \
            all(r.shape[0] % tm == 0 for r in residuals)
        blocks = sum(2 * len(pieces) * tm * k * 2 for pieces, k in zip(xs, ks))
        blocks += (2 + 2 * len(residuals) + 1) * tm * tn * 4
        if rows_ok and weights + blocks <= 0.8 * V7X_VMEM_LIMIT_BYTES:
            return tm
    raise ValueError("no row tile fits")


def matmul(xs, ws, residuals=(), out_dtype=F32, tn=1024):
    n = sum(x.shape[0] for x in xs[0])
    m = ws[0][0].shape[1]
    tn = min(tn, m)
    tm = _matmul_row_tile(xs, residuals, tn)
    in_specs, x_spans, w_specs = [], [], []
    for pieces, (w, row) in zip(xs, ws):
        k = pieces[0].shape[1]
        assert row % k == 0
        specs, spans = _row_sources(pieces, tm, k, lambda j, i: 0)
        in_specs += specs
        x_spans.append(spans)
        w_specs.append(pl.BlockSpec((k, tn), lambda j, i, rb=row // k: (rb, j)))
    res_specs, res_spans = _row_sources(list(residuals), tm, tn, lambda j, i: j)
    kern = functools.partial(_matmul_kernel, x_spans=x_spans, res_spans=res_spans, k_chunk=256)
    return pl.pallas_call(
        kern,
        grid=(m // tn, n // tm),
        in_specs=in_specs + w_specs + res_specs,
        out_specs=pl.BlockSpec((tm, tn), lambda j, i: (i, j)),
        out_shape=jax.ShapeDtypeStruct((n, m), out_dtype),
        scratch_shapes=[pltpu.VMEM((pieces[0].shape[1], tn), BF16) for pieces in xs],
        compiler_params=_params("arbitrary", "arbitrary"),
        name="matmul",
    )(*[p for pieces in xs for p in pieces], *[w for w, _ in ws], *residuals)


def _mixer_a_kernel(u_ref, v_ref, lng_ref, lnb_ref, ws_ref, bs_ref, o_ref):
    tm = u_ref.shape[0]
    gd = u_ref.shape[1] // A_GROUPS
    for c in range(tm // CHUNK):
        rows = slice(c * CHUNK, (c + 1) * CHUNK)
        u = jax.nn.gelu(u_ref[rows, :])
        v = jax.nn.gelu(v_ref[rows, :])
        mu = jnp.mean(v, axis=-1, keepdims=True)
        vc = v - mu
        var = jnp.mean(vc * vc, axis=-1, keepdims=True)
        v = vc * lax.rsqrt(var + EPS) * lng_ref[...] + lnb_ref[...]
        for g in range(A_GROUPS):
            cols = slice(g * gd, (g + 1) * gd)
            f = jnp.dot(ws_ref[g].astype(BF16), v[:, cols].astype(BF16), preferred_element_type=F32)
            f = f + bs_ref[g]
            o_ref[rows, cols] = (u[:, cols] * f).astype(o_ref.dtype)


def mixer_a(z, ln_g, ln_b, w_s, b_s, tm=512):
    n = z.shape[0]
    aw = ln_g.shape[0]
    gd = aw // A_GROUPS
    bs_b = jnp.broadcast_to(b_s[:, :, None], (A_GROUPS, CHUNK, gd))
    return pl.pallas_call(
        _mixer_a_kernel,
        grid=(n // tm,),
        in_specs=[
            pl.BlockSpec((tm, aw), lambda i: (i, 0)),
            pl.BlockSpec((tm, aw), lambda i: (i, 1)),
            pl.BlockSpec((1, aw), lambda i: (0, 0)),
            pl.BlockSpec((1, aw), lambda i: (0, 0)),
            pl.BlockSpec((A_GROUPS, CHUNK, CHUNK), lambda i: (0, 0, 0)),
            pl.BlockSpec((A_GROUPS, CHUNK, gd), lambda i: (0, 0, 0)),
        ],
        out_specs=pl.BlockSpec((tm, aw), lambda i: (i, 0)),
        out_shape=jax.ShapeDtypeStruct((n, aw), BF16),
        compiler_params=_params("parallel"),
        name="mixer_a",
    )(z, z, ln_g.reshape(1, aw), ln_b.reshape(1, aw), w_s, bs_b)


def _skew(w, rows):
    n = w.shape[-1]
    flat = jnp.tile(w, (1,) * (w.ndim - 1) + (rows,))
    return flat[..., :rows * (n - 1)].reshape(w.shape[:-1] + (rows, n - 1))


def _pair_rmsnorm(x, gain):
    sq = x * x
    lane = lax.broadcasted_iota(jnp.int32, x.shape, 1)
    lo = lane < B_DH
    s_lo = jnp.sum(jnp.where(lo, sq, 0.0), axis=-1, keepdims=True)
    s_hi = jnp.sum(jnp.where(lo, 0.0, sq), axis=-1, keepdims=True)
    ms = jnp.where(lo, s_lo, s_hi) * (1.0 / B_DH)
    return x * lax.rsqrt(ms + EPS) * gain


def _natten_kernel(q_ref, k_ref, v_ref, gq_ref, gk_ref, bias_ref, o_ref, kn_ref, vb_ref, s_ref, m_ref, e_ref,
                   *, rows_per_step, rows):
    seq = k_ref.shape[0]
    t = pl.program_id(2)
    prep = 256

    @pl.when(t == 0)
    def _():
        def body(c, carry):
            sl = pl.ds(pl.multiple_of(c * prep, prep), prep)
            kn_ref[sl, :] = _pair_rmsnorm(k_ref[sl, :], gk_ref[...]).astype(BF16)
            vb_ref[sl, :LANES] = v_ref[sl, :].astype(BF16)
            vb_ref[sl, LANES:] = jnp.ones((prep, LANES), BF16)
            return carry
        lax.fori_loop(0, seq // prep, body, 0, unroll=PREP_UNROLL)

    q = (_pair_rmsnorm(q_ref[...], gq_ref[...]) * (B_DH ** -0.5)).astype(BF16)
    lo = lax.broadcasted_iota(jnp.int32, (GRID_W, LANES), 1) < B_DH
    zero = jnp.zeros((GRID_W, LANES), BF16)
    win = NA_KH * GRID_W

    def window(rr):
        r = t * rows_per_step + rr
        rs = jnp.clip(r - NA_KH // 2, 0, rows - NA_KH)
        return r - rs, pl.ds(pl.multiple_of(rs * GRID_W, GRID_W), win)

    for rr in range(rows_per_step):
        case, ksl = window(rr)
        qr = q[rr * GRID_W:(rr + 1) * GRID_W, :]
        q2 = jnp.concatenate([jnp.where(lo, qr, zero), jnp.where(lo, zero, qr)], axis=0)
        s = lax.dot_general(q2, kn_ref[ksl, :], (((1,), (1,)), ((), ())), preferred_element_type=F32)
        s_ref[rr] = s + bias_ref[0, case]
    for rr in range(rows_per_step):
        m = jnp.max(s_ref[rr], axis=-1, keepdims=True)
        m_ref[rr] = jnp.broadcast_to(m, (2 * GRID_W, LANES))
    for rr in range(rows_per_step):
        mb = m_ref[rr]
        for c in range(win // LANES):
            cols = slice(c * LANES, (c + 1) * LANES)
            e_ref[rr, :, cols] = jnp.exp(s_ref[rr, :, cols] - mb).astype(BF16)
    for rr in range(rows_per_step):
        _, ksl = window(rr)
        ol = jnp.dot(e_ref[rr], vb_ref[ksl, :], preferred_element_type=F32)
        o = ol[:, :LANES] / ol[:, LANES:]
        o_ref[rr * GRID_W:(rr + 1) * GRID_W, :] = jnp.where(lo, o[:GRID_W], o[GRID_W:]).astype(o_ref.dtype)


def natten_bias_table(rpb):
    heads = rpb.shape[0]
    col = jnp.arange(GRID_W)
    col_start = jnp.clip(col - NA_KW // 2, 0, GRID_W - NA_KW)
    col_valid = (col[None, :] >= col_start[:, None]) & (col[None, :] < col_start[:, None] + NA_KW)
    edge = GRID_W - NA_KW
    w = jnp.concatenate([jnp.repeat(rpb[..., :1], edge, axis=-1), rpb.astype(F32),
                         jnp.repeat(rpb[..., -1:], edge, axis=-1), jnp.zeros(rpb.shape[:-1] + (1,), F32)], axis=-1)
    bias_c = _skew(w, GRID_W)[..., GRID_W - 1:]
    tab = jnp.stack([bias_c[:, NA_KH - 1 - c:2 * NA_KH - 1 - c] for c in range(NA_KH)], axis=1)
    tab = jnp.where(col_valid[:, None, :], tab.transpose(0, 1, 3, 2, 4), NEG)
    tab = tab.reshape(heads // 2, 2, NA_KH, GRID_W, NA_KH * GRID_W).transpose(0, 2, 1, 3, 4)
    return tab.reshape(heads // 2, NA_KH, 2 * GRID_W, NA_KH * GRID_W)


def natten(z, col0, row0, batch, seq, gq, gk, bias_tab, rows_per_step=16):
    hp = bias_tab.shape[0]
    rows = seq // GRID_W
    assert rows >= NA_KH and rows % rows_per_step == 0 and row0 % seq == 0
    tq = rows_per_step * GRID_W
    steps = rows // rows_per_step
    qb0 = row0 // tq
    sb0 = row0 // seq
    win = NA_KH * GRID_W
    kern = functools.partial(_natten_kernel, rows_per_step=rows_per_step, rows=rows)
    gq2 = jnp.tile(gq, 2).reshape(1, 2 * B_DH)
    gk2 = jnp.tile(gk, 2).reshape(1, 2 * B_DH)
    return pl.pallas_call(
        kern,
        grid=(batch, hp, steps),
        in_specs=[
            pl.BlockSpec((tq, LANES), lambda b, h, t: (qb0 + b * steps + t, col0 + h)),
            pl.BlockSpec((seq, LANES), lambda b, h, t: (sb0 + b, col0 + hp + h)),
            pl.BlockSpec((seq, LANES), lambda b, h, t: (sb0 + b, col0 + 2 * hp + h)),
            pl.BlockSpec((1, LANES), lambda b, h, t: (0, 0)),
            pl.BlockSpec((1, LANES), lambda b, h, t: (0, 0)),
            pl.BlockSpec((1, NA_KH, 2 * GRID_W, win), lambda b, h, t: (h, 0, 0, 0)),
        ],
        out_specs=pl.BlockSpec((tq, LANES), lambda b, h, t: (b * steps + t, h)),
        out_shape=jax.ShapeDtypeStruct((batch * seq, 2 * hp * B_DH), BF16),
        scratch_shapes=[
            pltpu.VMEM((seq, LANES), BF16),
            pltpu.VMEM((seq, 2 * LANES), BF16),
            pltpu.VMEM((rows_per_step, 2 * GRID_W, win), F32),
            pltpu.VMEM((rows_per_step, 2 * GRID_W, LANES), F32),
            pltpu.VMEM((rows_per_step, 2 * GRID_W, win), BF16),
        ],
        compiler_params=_params("parallel", "parallel", "arbitrary"),
        name="natten",
    )(z, z, z, gq2, gk2, bias_tab)


def t5_bucket(rel):
    nb = T5_BUCKETS // 2
    max_exact = nb // 2
    ret = jnp.where(rel > 0, nb, 0)
    n = jnp.abs(rel)
    large = max_exact + (jnp.log(jnp.maximum(n, 1).astype(F32) / max_exact)
                         / math.log(T5_MAX_DIST / max_exact) * (nb - max_exact)).astype(jnp.int32)
    large = jnp.minimum(large, nb - 1)
    return ret + jnp.where(n < max_exact, n, large)


def dilated_bias_table(t5_table, dil):
    span = C_TK - 1
    rel = jnp.arange(-span, span + 1)
    vec = t5_table.astype(F32)[t5_bucket(rel * dil)]
    vec = jnp.where((jnp.abs(rel) <= C_RAD)[:, None], vec, NEG).T
    w = jnp.concatenate([vec, jnp.zeros((vec.shape[0], 1), F32)], axis=1)
    r = _skew(w, C_TQ)
    return jnp.stack([r[:, :, span - off:span - off + C_TK] for off in (0, C_RAD, 2 * C_RAD)], axis=1)


def _row_rmsnorm(x, gain):
    return x * lax.rsqrt(jnp.mean(x * x, axis=-1, keepdims=True) + EPS) * gain


def _dilated_kernel(q_ref, k_ref, v_ref, gq_ref, gk_ref, b0_ref, b1_ref, b2_ref, o_ref,
                    kn_ref, qn_ref, oacc_ref, lse_ref, s_ref, m_ref, e_ref, *kv_by_residue, seq, unroll):
    t = pl.program_id(2)
    prep = 256
    kv_sub = {dil: (kv_by_residue[2 * i], kv_by_residue[2 * i + 1])
              for i, dil in enumerate(d for _, d in C_PATTERNS if d > 1)}

    @pl.when(t == 0)
    def _():
        def body(c, carry):
            sl = pl.ds(pl.multiple_of(c * prep, prep), prep)
            kn_ref[sl, :] = _row_rmsnorm(k_ref[sl, :], gk_ref[...])
            return carry
        lax.fori_loop(0, seq // prep, body, 0, unroll=PREP_UNROLL)

        for dil, (ks_ref, vs_ref) in kv_sub.items():
            rows = min(prep, seq // dil)
            for s in range(dil):
                def split(c, carry, dil=dil, s=s, rows=rows, ks_ref=ks_ref, vs_ref=vs_ref):
                    src = pl.ds(c * rows * dil + s, rows, stride=dil)
                    dst = pl.ds(pl.multiple_of(c * rows, rows), rows)
                    ks_ref[s, dst, :] = kn_ref[src, :].astype(BF16)
                    vs_ref[s, dst, :] = v_ref[src, :].astype(BF16)
                    return carry
                lax.fori_loop(0, seq // (dil * rows), split, 0, unroll=True)

    qn_ref[...] = _row_rmsnorm(q_ref[...], gq_ref[...]) * (C_DH ** -0.5)
    n_tiles = C_SUPER // C_TQ

    for p, ((_, dil), b_ref) in enumerate(zip(C_PATTERNS, (b0_ref, b1_ref, b2_ref))):
        n_sub = seq // dil
        tk = min(C_TK, n_sub)
        span = C_TQ * dil
        ones = jnp.ones((tk, LANES), BF16)

        def tile(idx, dil=dil, n_sub=n_sub, tk=tk, span=span):
            u = idx // dil
            s = idx % dil
            qpos = u * span + s
            j0 = t * (C_SUPER // dil) + u * C_TQ
            ws = jnp.clip(j0 - C_RAD, 0, n_sub - tk)
            ksl = pl.ds(pl.multiple_of(ws, C_RAD), tk)
            if dil == 1:
                return pl.ds(pl.multiple_of(qpos, C_TQ), C_TQ), s, ksl, (j0 - ws) // C_RAD
            return pl.ds(qpos, C_TQ, stride=dil), s, ksl, (j0 - ws) // C_RAD

        def keys(s, ksl, dil=dil):
            return kn_ref[ksl, :].astype(BF16) if dil == 1 else kv_sub[dil][0][s, ksl, :]

        def values(s, ksl, dil=dil):
            return v_ref[ksl, :].astype(BF16) if dil == 1 else kv_sub[dil][1][s, ksl, :]

        def scores(idx, carry, tile=tile, keys=keys, tk=tk, b_ref=b_ref):
            qsl, s, ksl, case = tile(idx)
            q = qn_ref[qsl, :].astype(BF16)
            sc = lax.dot_general(q, keys(s, ksl), (((1,), (1,)), ((), ())), preferred_element_type=F32)
            s_ref[idx, :, :tk] = sc + b_ref[0, case, :, :tk]
            return carry

        def row_max(idx, carry, tk=tk):
            m = jnp.max(s_ref[idx, :, :tk], axis=-1, keepdims=True)
            m_ref[idx] = jnp.broadcast_to(m, (C_TQ, LANES))
            return carry

        def exps(idx, carry, tk=tk):
            mb = m_ref[idx]
            for c in range(tk // LANES):
                cols = slice(c * LANES, (c + 1) * LANES)
                e_ref[idx, :, cols] = jnp.exp(s_ref[idx, :, cols] - mb).astype(BF16)
            return carry

        def outputs(idx, carry, tile=tile, values=values, tk=tk, ones=ones, p=p):
            qsl, s, ksl, _ = tile(idx)
            v1 = jnp.concatenate([values(s, ksl), ones], axis=1)
            ol = jnp.dot(e_ref[idx, :, :tk], v1, preferred_element_type=F32)
            l = ol[:, LANES:]
            oacc_ref[p, qsl, :] = ol[:, :LANES] / l
            lse_ref[p, qsl, :] = m_ref[idx] + jnp.log(l)
            return carry

        for phase in (scores, row_max, exps, outputs):
            lax.fori_loop(0, n_tiles, phase, 0, unroll=unroll)

    l0, l1, l2 = lse_ref[0], lse_ref[1], lse_ref[2]
    mx = jnp.maximum(jnp.maximum(l0, l1), l2)
    w0, w1, w2 = jnp.exp(l0 - mx), jnp.exp(l1 - mx), jnp.exp(l2 - mx)
    num = w0 * oacc_ref[0] + w1 * oacc_ref[1] + w2 * oacc_ref[2]
    o_ref[...] = (num / (w0 + w1 + w2)).astype(o_ref.dtype)


def dilated_attention(qkv, row0, batch, seq, gq, gk, tabs, unroll=16):
    heads = tabs[0].shape[0]
    assert seq % C_SUPER == 0 and row0 % seq == 0
    steps = seq // C_SUPER
    qb0 = row0 // C_SUPER
    sb0 = row0 // seq
    n_tiles = C_SUPER // C_TQ
    kern = functools.partial(_dilated_kernel, seq=seq, unroll=unroll)
    tab_specs = [pl.BlockSpec((1,) + tuple(tb.shape[1:]), lambda b, h, t: (h, 0, 0, 0)) for tb in tabs]
    return pl.pallas_call(
        kern,
        grid=(batch, heads, steps),
        in_specs=[
            pl.BlockSpec((C_SUPER, LANES), lambda b, h, t: (qb0 + b * steps + t, h)),
            pl.BlockSpec((seq, LANES), lambda b, h, t: (sb0 + b, heads + h)),
            pl.BlockSpec((seq, LANES), lambda b, h, t: (sb0 + b, 2 * heads + h)),
            pl.BlockSpec((1, LANES), lambda b, h, t: (0, 0)),
            pl.BlockSpec((1, LANES), lambda b, h, t: (0, 0)),
        ] + tab_specs,
        out_specs=pl.BlockSpec((C_SUPER, LANES), lambda b, h, t: (b * steps + t, h)),
        out_shape=jax.ShapeDtypeStruct((batch * seq, heads * C_DH), BF16),
        scratch_shapes=[
            pltpu.VMEM((seq, LANES), F32),
            pltpu.VMEM((C_SUPER, LANES), F32),
            pltpu.VMEM((len(C_PATTERNS), C_SUPER, LANES), F32),
            pltpu.VMEM((len(C_PATTERNS), C_SUPER, LANES), F32),
            pltpu.VMEM((n_tiles, C_TQ, C_TK), F32),
            pltpu.VMEM((n_tiles, C_TQ, LANES), F32),
            pltpu.VMEM((n_tiles, C_TQ, C_TK), BF16),
        ] + [pltpu.VMEM((dil, seq // dil, LANES), BF16) for _, dil in C_PATTERNS if dil > 1 for _ in "kv"],
        compiler_params=_params("parallel", "parallel", "arbitrary"),
        name="dilated",
    )(qkv, qkv, qkv, gq.reshape(1, C_DH), gk.reshape(1, C_DH), *tabs)


def _pack_bf16_pairs(lo, hi):
    lo_bits = lax.bitcast_convert_type(lo.astype(BF16).astype(F32), jnp.uint32)
    hi_bits = lax.bitcast_convert_type(hi.astype(BF16).astype(F32), jnp.uint32)
    return (hi_bits & jnp.uint32(0xFFFF0000)) | (lo_bits >> 16)


def _unpack_bf16_pairs(packed):
    lo = lax.bitcast_convert_type(packed << 16, F32)
    hi = lax.bitcast_convert_type(packed & jnp.uint32(0xFFFF0000), F32)
    return lo, hi


META_E0, META_E1, META_R0, META_R1, META_G0, META_G1 = range(6)


def _route_kernel(x_ref, g_ref, wr_ref, br_ref, hp_ref, meta_ref, tot_ref):
    x = x_ref[...]
    h = x * lax.rsqrt(jnp.mean(x * x, axis=-1, keepdims=True) + EPS) * g_ref[...]
    half = h.shape[1] // 2
    hp_ref[...] = _pack_bf16_pairs(h[:, :half], h[:, half:])
    h_hi = h.astype(BF16)
    h_lo = (h - h_hi.astype(F32)).astype(BF16)
    w_hi = wr_ref[...].astype(BF16)
    w_lo = (wr_ref[...] - w_hi.astype(F32)).astype(BF16)
    hh = jnp.dot(h_hi, jnp.concatenate([w_hi, w_lo], axis=1), preferred_element_type=F32)
    logits = (hh[:, :ROUTER_LANES] + hh[:, ROUTER_LANES:]
              + jnp.dot(h_lo, w_hi, preferred_element_type=F32)) + br_ref[...]
    tm = x.shape[0]
    lane = lax.broadcasted_iota(jnp.int32, (tm, ROUTER_LANES), 1)
    lane_f = lane.astype(F32)

    def first_max(v):
        mx = jnp.max(v, axis=-1, keepdims=True)
        return mx, jnp.min(jnp.where(v == mx, lane_f, float(ROUTER_LANES)), axis=-1, keepdims=True)

    lg = jnp.where(lane < MOE_GROUPS, logits, -jnp.inf)
    gmax, gsel = first_max(lg)
    p_g = 1.0 / jnp.sum(jnp.exp(lg - gmax), axis=-1, keepdims=True)
    expert_lane = (lane >= MOE_GROUPS) & (lane < MOE_GROUPS + MOE_EXPERTS)
    in_group = expert_lane & (((lane - MOE_GROUPS) // MOE_PER_GROUP) == gsel.astype(jnp.int32))
    le = jnp.where(in_group, logits, -jnp.inf)
    v0, i0 = first_max(le)
    pick0 = lane_f == i0
    v1, i1 = first_max(jnp.where(pick0, -jnp.inf, le))
    pick1 = lane_f == i1
    t = jnp.exp(v1 - v0)
    g0 = p_g / (1.0 + t)
    g1 = p_g * t / (1.0 + t)

    onehot = jnp.where(pick0 | pick1, 1.0, 0.0).astype(BF16)
    row = lax.broadcasted_iota(jnp.int32, (tm, tm), 0)
    col = lax.broadcasted_iota(jnp.int32, (tm, tm), 1)
    earlier = jnp.where(col < row, 1.0, 0.0).astype(BF16)
    before = jnp.dot(earlier, onehot, preferred_element_type=F32)
    r0 = jnp.sum(jnp.where(pick0, before, 0.0), axis=-1, keepdims=True)
    r1 = jnp.sum(jnp.where(pick1, before, 0.0), axis=-1, keepdims=True)

    meta = jnp.zeros((tm, ROUTER_LANES), F32)
    for pos, val in ((META_E0, i0 - MOE_GROUPS), (META_E1, i1 - MOE_GROUPS), (META_R0, r0), (META_R1, r1),
                     (META_G0, g0), (META_G1, g1)):
        meta = jnp.where(lane == pos, val, meta)
    meta_ref[...] = meta
    tot_ref[0] = jnp.sum(onehot.astype(F32), axis=0, keepdims=True)


def moe_route(x, g, wr, br, tm):
    n, d = x.shape
    return pl.pallas_call(
        _route_kernel,
        grid=(n // tm,),
        in_specs=[
            pl.BlockSpec((tm, d), lambda i: (i, 0)),
            pl.BlockSpec((1, d), lambda i: (0, 0)),
            pl.BlockSpec((d, ROUTER_LANES), lambda i: (0, 0)),
            pl.BlockSpec((1, ROUTER_LANES), lambda i: (0, 0)),
        ],
        out_specs=[
            pl.BlockSpec((tm, d // 2), lambda i: (i, 0)),
            pl.BlockSpec((tm, ROUTER_LANES), lambda i: (i, 0)),
            pl.BlockSpec((1, 1, ROUTER_LANES), lambda i: (i, 0, 0)),
        ],
        out_shape=[
            jax.ShapeDtypeStruct((n, d // 2), jnp.uint32),
            jax.ShapeDtypeStruct((n, ROUTER_LANES), F32),
            jax.ShapeDtypeStruct((n // tm, 1, ROUTER_LANES), F32),
        ],
        compiler_params=_params("parallel"),
        name="moe_route",
    )(x, g.reshape(1, d), wr, br)


def _dispatch_kernel(d0_ref, d1_ref, pad_first_ref, pad_cnt_ref, nused_ref, h_ref, xb_hbm, sem):
    i = pl.program_id(0)
    tm = h_ref.shape[0]
    base = i * tm

    def row_copy(src_row, dst_row, level=0):
        return pltpu.make_async_copy(h_ref.at[pl.ds(src_row, 1), :], xb_hbm.at[pl.ds(dst_row, 1), :], sem.at[level])

    for r in range(tm):
        row_copy(r, d0_ref[base + r], 0).start(priority=(2 * (r % 2)) % DMA_PRIORITY_LEVELS)
        row_copy(r, d1_ref[base + r], 1).start(priority=(2 * (r % 2) + 1) % DMA_PRIORITY_LEVELS)

    def drain(r, carry):
        row_copy(0, 0, 0).wait()
        row_copy(0, 0, 1).wait()
        return carry
    lax.fori_loop(0, tm, drain, 0, unroll=8)

    @pl.when(i == pl.num_programs(0) - 1)
    def _():
        def per_expert(e, carry):
            def fill(j, c):
                row_copy(0, pad_first_ref[e] + j).start()
                return c
            lax.fori_loop(0, pad_cnt_ref[e], fill, 0)

            def fill_wait(j, c):
                row_copy(0, 0).wait()
                return c
            lax.fori_loop(0, pad_cnt_ref[e], fill_wait, 0)
            return carry
        lax.fori_loop(0, MOE_EXPERTS, per_expert, 0)

        def block_copy(b):
            return pltpu.make_async_copy(h_ref, xb_hbm.at[pl.ds(pl.multiple_of(b * tm, tm), tm), :], sem.at[0])

        def fill_block(b, c):
            block_copy(b).start()
            return c
        lax.fori_loop(nused_ref[0], xb_hbm.shape[0] // tm, fill_block, 0)

        def fill_block_wait(b, c):
            block_copy(b).wait()
            return c
        lax.fori_loop(nused_ref[0], xb_hbm.shape[0] // tm, fill_block_wait, 0)


def moe_dispatch(hp, d0, d1, pad_first, pad_cnt, n_used, p, tm=MOE_TM):
    n, dw = hp.shape
    return pl.pallas_call(
        _dispatch_kernel,
        grid_spec=pltpu.PrefetchScalarGridSpec(
            num_scalar_prefetch=5,
            grid=(n // tm,),
            in_specs=[pl.BlockSpec((tm, dw), lambda i, *_: (i, 0))],
            out_specs=pl.BlockSpec(memory_space=pl.ANY),
            scratch_shapes=[pltpu.SemaphoreType.DMA((2,))],
        ),
        out_shape=jax.ShapeDtypeStruct((p, dw), hp.dtype),
        compiler_params=_params("arbitrary"),
        name="moe_dispatch",
    )(d0, d1, pad_first, pad_cnt, n_used, hp)


def _row_gather_copy(src_hbm, dst_ref, sem, src_row, dst_row):
    return pltpu.make_async_copy(src_hbm.at[pl.ds(src_row, 1), :], dst_ref.at[pl.ds(dst_row, 1), :], sem)


def _gather_row_sets(gathers, base, src_hbm, unroll=8):
    rows = gathers[0][1].shape[0]
    for r in range(rows):
        for k, (idx_ref, dst_ref, sem) in enumerate(gathers):
            level = (len(gathers) * (r % 2) + k) % DMA_PRIORITY_LEVELS
            _row_gather_copy(src_hbm, dst_ref, sem, idx_ref[base + r], r).start(priority=level)

    def drain(r, carry):
        for _, dst_ref, sem in gathers:
            _row_gather_copy(src_hbm, dst_ref, sem, 0, r).wait()
        return carry
    lax.fori_loop(0, rows, drain, 0, unroll=unroll)


def _expert_weight_copies(layer, e, w_hbm, stage, sem):
    return [pltpu.make_async_copy(w.at[layer, e], s, sem.at[k]) for k, (w, s) in enumerate(zip(w_hbm, stage))]


def _ffn_kernel(be_ref, first_ref, next_ref, nused_ref, x_ref, wg_hbm, wu_hbm, wd_hbm, o_ref,
                sg_ref, su_ref, sd_ref, wg_ref, wu_ref, wd_ref, sem, *, layer, k_chunk):
    i = pl.program_id(0)
    w_hbm = (wg_hbm, wu_hbm, wd_hbm)
    stage = (sg_ref, su_ref, sd_ref)
    resident = (wg_ref, wu_ref, wd_ref)

    @pl.when(i == 0)
    def _():
        for cp in _expert_weight_copies(layer, be_ref[0], w_hbm, stage, sem):
            cp.start()

    @pl.when(first_ref[i] == 1)
    def _():
        for cp in _expert_weight_copies(layer, be_ref[i], w_hbm, stage, sem):
            cp.wait()
        for s_ref, w_ref in zip(stage, resident):
            rows = math.gcd(k_chunk, s_ref.shape[0])

            def cast(c, carry, s_ref=s_ref, w_ref=w_ref, rows=rows):
                sl = pl.ds(pl.multiple_of(c * rows, rows), rows)
                w_ref[sl, :] = s_ref[sl, :].astype(BF16)
                return carry
            lax.fori_loop(0, s_ref.shape[0] // rows, cast, 0, unroll=True)

        @pl.when(next_ref[i] >= 0)
        def _():
            for cp in _expert_weight_copies(layer, next_ref[i], w_hbm, stage, sem):
                cp.start()

    live = i < nused_ref[0]

    @pl.when(live)
    def _():
        lo, hi = _unpack_bf16_pairs(x_ref[...])
        lo, hi = lo.astype(BF16), hi.astype(BF16)
        half = lo.shape[1]
        g = (jnp.dot(lo, wg_ref[:half, :], preferred_element_type=F32)
             + jnp.dot(hi, wg_ref[half:, :], preferred_element_type=F32))
        u = (jnp.dot(lo, wu_ref[:half, :], preferred_element_type=F32)
             + jnp.dot(hi, wu_ref[half:, :], preferred_element_type=F32))
        h = (jax.nn.silu(g) * u).astype(BF16)
        y = jnp.dot(h, wd_ref[...], preferred_element_type=F32)
        o_ref[...] = _pack_bf16_pairs(y[:, :half], y[:, half:])

    @pl.when(jnp.logical_not(live))
    def _():
        o_ref[...] = jnp.zeros_like(o_ref)


def expert_ffn(xb, block_e, first, next_e, n_used, wg, wu, wd, layer, tm=MOE_TM):
    p, dw = xb.shape
    d, de = wg.shape[2], wg.shape[3]
    any_spec = pl.BlockSpec(memory_space=pl.ANY)
    return pl.pallas_call(
        functools.partial(_ffn_kernel, layer=layer, k_chunk=256),
        grid_spec=pltpu.PrefetchScalarGridSpec(
            num_scalar_prefetch=4,
            grid=(p // tm,),
            in_specs=[
                pl.BlockSpec((tm, dw), lambda i, be, fi, ne, nu: (jnp.minimum(i, nu[0] - 1), 0)),
                any_spec, any_spec, any_spec,
            ],
            out_specs=pl.BlockSpec((tm, dw), lambda i, *_: (i, 0)),
            scratch_shapes=[
                pltpu.VMEM((d, de), F32), pltpu.VMEM((d, de), F32), pltpu.VMEM((de, d), F32),
                pltpu.VMEM((d, de), BF16), pltpu.VMEM((d, de), BF16), pltpu.VMEM((de, d), BF16),
                pltpu.SemaphoreType.DMA((3,)),
            ],
        ),
        out_shape=jax.ShapeDtypeStruct((p, dw), jnp.uint32),
        compiler_params=_params("arbitrary"),
        name="moe_ffn",
    )(block_e, first, next_e, n_used, xb, wg, wu, wd)


def _combine_kernel(d0_ref, d1_ref, x_ref, meta_ref, yb_hbm, *rest, row0, with_norm):
    if with_norm:
        gain_ref, o_ref, h_ref, a_ref, b_ref, sem = rest
    else:
        o_ref, a_ref, b_ref, sem = rest
    base = row0 + pl.program_id(0) * x_ref.shape[0]
    _gather_row_sets([(d0_ref, a_ref, sem.at[0]), (d1_ref, b_ref, sem.at[1])], base, yb_hbm)
    a_lo, a_hi = _unpack_bf16_pairs(a_ref[...])
    b_lo, b_hi = _unpack_bf16_pairs(b_ref[...])
    g0 = meta_ref[:, META_G0:META_G0 + 1]
    g1 = meta_ref[:, META_G1:META_G1 + 1]
    half = a_lo.shape[1]
    lo = x_ref[:, :half] + (g0 * a_lo + g1 * b_lo)
    hi = x_ref[:, half:] + (g0 * a_hi + g1 * b_hi)
    o_ref[:, :half] = lo
    o_ref[:, half:] = hi
    if with_norm:
        ms = (jnp.sum(lo * lo, axis=-1, keepdims=True) + jnp.sum(hi * hi, axis=-1, keepdims=True)) / (2 * half)
        inv = lax.rsqrt(ms + EPS)
        h_ref[:, :half] = (lo * inv * gain_ref[:, :half]).astype(h_ref.dtype)
        h_ref[:, half:] = (hi * inv * gain_ref[:, half:]).astype(h_ref.dtype)


def moe_combine(x, meta, yb, d0, d1, row0, rows, next_gain=None, tm=512):
    d = x.shape[1]
    assert row0 % tm == 0
    blk0 = row0 // tm
    with_norm = next_gain is not None
    row_spec = pl.BlockSpec((tm, d), lambda i, a, b: (i, 0))
    in_specs = [
        pl.BlockSpec((tm, d), lambda i, a, b: (blk0 + i, 0)),
        pl.BlockSpec((tm, ROUTER_LANES), lambda i, a, b: (blk0 + i, 0)),
        pl.BlockSpec(memory_space=pl.ANY),
    ]
    args = [d0, d1, x, meta, yb]
    out_specs, out_shape = row_spec, jax.ShapeDtypeStruct((rows, d), F32)
    if with_norm:
        in_specs.append(pl.BlockSpec((1, d), lambda i, a, b: (0, 0)))
        args.append(next_gain.reshape(1, d))
        out_specs, out_shape = [row_spec, row_spec], [out_shape, jax.ShapeDtypeStruct((rows, d), BF16)]
    return pl.pallas_call(
        functools.partial(_combine_kernel, row0=row0, with_norm=with_norm),
        grid_spec=pltpu.PrefetchScalarGridSpec(
            num_scalar_prefetch=2,
            grid=(rows // tm,),
            in_specs=in_specs,
            out_specs=out_specs,
            scratch_shapes=[pltpu.VMEM((tm, d // 2), jnp.uint32), pltpu.VMEM((tm, d // 2), jnp.uint32),
                            pltpu.SemaphoreType.DMA((2,))],
        ),
        out_shape=out_shape,
        compiler_params=_params("arbitrary"),
        name="moe_combine",
    )(*args)


def moe_layout(meta, tot, tile, tm=MOE_TM):
    n = meta.shape[0]
    m = n * MOE_TOPK
    ids = jnp.arange(MOE_EXPERTS, dtype=jnp.int32)
    cnt = tot[:, 0, MOE_GROUPS:MOE_GROUPS + MOE_EXPERTS].astype(jnp.int32)
    counts = jnp.sum(cnt, axis=0)
    padded = (counts + tm - 1) // tm * tm
    pad_end = jnp.cumsum(padded)
    pad_start = pad_end - padded
    base = pad_start[None, :] + jnp.cumsum(cnt, axis=0) - cnt
    base_tok = jnp.repeat(base, tile, axis=0)

    def slots(e_col, r_col):
        e = meta[:, e_col].astype(jnp.int32)
        return jnp.sum(jnp.where(e[:, None] == ids[None, :], base_tok, 0), axis=1) + meta[:, r_col].astype(jnp.int32)

    d0 = slots(META_E0, META_R0)
    d1 = slots(META_E1, META_R1)
    n_blocks = (m + MOE_EXPERTS * (tm - 1) + tm - 1) // tm

    blk = jnp.arange(n_blocks, dtype=jnp.int32)
    n_used = (pad_end[-1] // tm).astype(jnp.int32)
    block_e = jnp.sum((pad_end[None, :] <= (blk * tm)[:, None]).astype(jnp.int32), axis=1)
    block_e = jnp.minimum(block_e, MOE_EXPERTS - 1)
    live = blk < n_used
    first = (live & ((blk == 0) | (block_e != jnp.roll(block_e, 1)))).astype(jnp.int32)
    has_rows = counts > 0
    later = jnp.arange(MOE_EXPERTS)[None, :] > jnp.arange(MOE_EXPERTS)[:, None]
    nxt_of_e = jnp.min(jnp.where(later & has_rows[None, :], jnp.arange(MOE_EXPERTS)[None, :], MOE_EXPERTS), axis=1)
    nxt_of_e = jnp.where(nxt_of_e == MOE_EXPERTS, -1, nxt_of_e).astype(jnp.int32)
    next_e = jnp.sum(jnp.where(block_e[:, None] == ids[None, :], nxt_of_e[None, :], 0), axis=1)
    pad_first = (pad_start + counts).astype(jnp.int32)
    pad_cnt = (padded - counts).astype(jnp.int32)
    return d0, d1, pad_first, pad_cnt, n_blocks * tm, block_e, first, next_e, n_used.reshape(1)


def hier_moe_residual(x, g, wr_g, br_g, wr_e, br_e, w_gate, w_up, w_down, layer, out_segments=None,
                      next_gain=None, tile=512):
    n, d = x.shape
    pad = ROUTER_LANES - MOE_GROUPS - MOE_EXPERTS
    wr = jnp.concatenate([wr_g, wr_e, jnp.zeros((d, pad), F32)], axis=1)
    br = jnp.concatenate([br_g, br_e, jnp.zeros((pad,), F32)]).reshape(1, ROUTER_LANES)
    hp, meta, tot = moe_route(x, g, wr, br, tile)
    d0, d1, pad_first, pad_cnt, p, block_e, first, next_e, n_used = moe_layout(meta, tot, tile)
    xb = moe_dispatch(hp, d0, d1, pad_first, pad_cnt, n_used, p)
    yb = expert_ffn(xb, block_e, first, next_e, n_used, w_gate, w_up, w_down, layer)
    if out_segments is None:
        return moe_combine(x, meta, yb, d0, d1, 0, n, next_gain)
    return [moe_combine(x, meta, yb, d0, d1, row0, rows) for row0, rows in out_segments]


def kernel(x_prompt, x_sample, norm_mix, norm_ffn, ev_w_in, ev_w_out, a_ln_g, a_ln_b, a_w_s, a_b_s, b_q_gain, b_k_gain, b_rpb, od_w_in, od_w_out, c_q_gain, c_k_gain, t5_table, moe_wr_g, moe_br_g, moe_wr_e, moe_br_e, moe_w_gate, moe_w_up, moe_w_down):
    d = x_prompt.shape[-1]
    segs = [(x_prompt.shape[0], x_prompt.shape[1]), (x_sample.shape[0], x_sample.shape[1])]
    xs = [x_prompt.reshape(-1, d), x_sample.reshape(-1, d)]
    depth = norm_mix.shape[0]
    h = rmsnorm(xs, norm_mix[0], BF16)
    for l in range(depth):
        i = l // 2
        last = l == depth - 1
        if l % 2 == 0:
            aw = a_ln_g.shape[1]
            z = matmul([[h]], [(ev_w_in[i], 0)])
            a_out = mixer_a(z, a_ln_g[i], a_ln_b[i], a_w_s[i], a_b_s[i])
            tab = natten_bias_table(b_rpb[i])
            b_out, row0 = [], 0
            for batch, seq in segs:
                b_out.append(natten(z, 2 * aw // LANES, row0, batch, seq, b_q_gain[i], b_k_gain[i], tab))
                row0 += batch * seq
            x = matmul([[a_out], b_out], [(ev_w_out[i], 0), (ev_w_out[i], aw)], residuals=xs)
        else:
            qkv = matmul([[h]], [(od_w_in[i], 0)])
            tabs = [dilated_bias_table(t5_table, dil) for _, dil in C_PATTERNS]
            c_out, row0 = [], 0
            for batch, seq in segs:
                c_out.append(dilated_attention(qkv, row0, batch, seq, c_q_gain[i], c_k_gain[i], tabs))
                row0 += batch * seq
            x = matmul([c_out], [(od_w_out[i], 0)], residuals=xs)
        n0 = segs[0][0] * segs[0][1]
        res = hier_moe_residual(x, norm_ffn[l], moe_wr_g[l], moe_br_g[l], moe_wr_e[l], moe_br_e[l],
                                moe_w_gate, moe_w_up, moe_w_down, l,
                                out_segments=[(0, n0), (n0, segs[1][0] * segs[1][1])] if last else None,
                                next_gain=None if last else norm_mix[l + 1])
        if last:
            xs = res
        else:
            xs, h = [res[0]], res[1]
    return (xs[0].reshape(x_prompt.shape), xs[1].reshape(x_sample.shape))
```

```python
import functools
import math

import jax
import jax.numpy as jnp
from jax import lax
from jax.experimental import pallas as pl
from jax.experimental.pallas import tpu as pltpu

F32 = jnp.float32
BF16 = jnp.bfloat16
EPS = 1e-6
NEG = -1e30

V7X_VMEM_LIMIT_BYTES = 56 * 1024 * 1024
LANES = 128

GRID_W = 64
CHUNK = 128
A_GROUPS = 8
NA_KH = 8
NA_KW = 16
B_DH = 64
C_DH = 128
C_PATTERNS = ((128, 1), (512, 4), (2048, 16))
C_RAD = 64
C_TQ = 128
C_TK = C_TQ + 2 * C_RAD
C_SUPER = C_TQ * 16
T5_BUCKETS = 32
T5_MAX_DIST = 1024
MOE_GROUPS = 4
MOE_PER_GROUP = 8
MOE_EXPERTS = MOE_GROUPS * MOE_PER_GROUP
MOE_TOPK = 2
MOE_TM = 256
ROUTER_LANES = 128
PREP_UNROLL = 8
DMA_PRIORITY_LEVELS = 2


def _params(*sem):
    return pltpu.CompilerParams(dimension_semantics=sem, vmem_limit_bytes=V7X_VMEM_LIMIT_BYTES)


def _row_sources(arrays, tm, width, col_of):
    specs, spans, off = [], [], 0
    for a in arrays:
        assert a.shape[0] % tm == 0 and a.shape[0] >= tm
        nblk = a.shape[0] // tm
        specs.append(pl.BlockSpec(
            (tm, width), lambda *g, off=off, nblk=nblk: (jnp.clip(g[-1] - off, 0, nblk - 1), col_of(*g))))
        spans.append((off, nblk))
        off += nblk
    return specs, spans


def _active_rows(i, refs, spans):
    val = refs[-1][...]
    for ref, (off, nblk) in reversed(list(zip(refs[:-1], spans[:-1]))):
        val = jnp.where(i < off + nblk, ref[...], val)
    return val


def _rmsnorm_kernel(*refs, spans):
    x_refs, (g_ref, o_ref) = refs[:len(spans)], refs[len(spans):]
    x = _active_rows(pl.program_id(0), x_refs, spans)
    y = x * lax.rsqrt(jnp.mean(x * x, axis=-1, keepdims=True) + EPS)
    o_ref[...] = (y * g_ref[...]).astype(o_ref.dtype)


def rmsnorm(xs, g, out_dtype, tm=1024):
    d = xs[0].shape[1]
    n = sum(x.shape[0] for x in xs)
    specs, spans = _row_sources(xs, tm, d, lambda i: 0)
    return pl.pallas_call(
        functools.partial(_rmsnorm_kernel, spans=spans),
        grid=(n // tm,),
        in_specs=specs + [pl.BlockSpec((1, d), lambda i: (0, 0))],
        out_specs=pl.BlockSpec((tm, d), lambda i: (i, 0)),
        out_shape=jax.ShapeDtypeStruct((n, d), out_dtype),
        compiler_params=_params("arbitrary"),
        name="rmsnorm",
    )(*xs, g.reshape(1, d))


def _matmul_kernel(*refs, x_spans, res_spans, k_chunk):
    refs = list(refs)
    x_refs = [[refs.pop(0) for _ in spans] for spans in x_spans]
    w_refs = [refs.pop(0) for _ in x_spans]
    res_refs = [refs.pop(0) for _ in res_spans]
    o_ref, wb_refs = refs[0], refs[1:]
    i = pl.program_id(1)

    @pl.when(pl.program_id(1) == 0)
    def _():
        for w_ref, wb_ref in zip(w_refs, wb_refs):
            rows = math.gcd(k_chunk, w_ref.shape[0])

            def cast(c, carry, w_ref=w_ref, wb_ref=wb_ref, rows=rows):
                sl = pl.ds(pl.multiple_of(c * rows, rows), rows)
                wb_ref[sl, :] = w_ref[sl, :].astype(BF16)
                return carry
            lax.fori_loop(0, w_ref.shape[0] // rows, cast, 0, unroll=True)

    acc = None
    for pieces, spans, wb_ref in zip(x_refs, x_spans, wb_refs):
        part = jnp.dot(_active_rows(i, pieces, spans), wb_ref[...], preferred_element_type=F32)
        acc = part if acc is None else acc + part
    if res_spans:
        acc = _active_rows(i, res_refs, res_spans) + acc
    o_ref[...] = acc.astype(o_ref.dtype)


def _matmul_row_tile(xs, residuals, tn, candidates=(1024, 512, 256)):
    ks = [pieces[0].shape[1] for pieces in xs]
    weights = sum(2 * k * tn * 4 + k * tn * 2 for k in ks)
    for tm in candidates:
        rows_ok = all(p.shape[0] % tm == 0 for pieces in xs for p in pieces) and \
            all(r.shape[0] % tm == 0 for r in residuals)
        blocks = sum(2 * len(pieces) * tm * k * 2 for pieces, k in zip(xs, ks))
        blocks += (2 + 2 * len(residuals) + 1) * tm * tn * 4
        if rows_ok and weights + blocks <= 0.8 * V7X_VMEM_LIMIT_BYTES:
            return tm
    raise ValueError("no row tile fits")


def matmul(xs, ws, residuals=(), out_dtype=F32, tn=1024):
    n = sum(x.shape[0] for x in xs[0])
    m = ws[0][0].shape[1]
    tn = min(tn, m)
    tm = _matmul_row_tile(xs, residuals, tn)
    in_specs, x_spans, w_specs = [], [], []
    for pieces, (w, row) in zip(xs, ws):
        k = pieces[0].shape[1]
        assert row % k == 0
        specs, spans = _row_sources(pieces, tm, k, lambda j, i: 0)
        in_specs += specs
        x_spans.append(spans)
        w_specs.append(pl.BlockSpec((k, tn), lambda j, i, rb=row // k: (rb, j)))
    res_specs, res_spans = _row_sources(list(residuals), tm, tn, lambda j, i: j)
    kern = functools.partial(_matmul_kernel, x_spans=x_spans, res_spans=res_spans, k_chunk=256)
    return pl.pallas_call(
        kern,
        grid=(m // tn, n // tm),
        in_specs=in_specs + w_specs + res_specs,
        out_specs=pl.BlockSpec((tm, tn), lambda j, i: (i, j)),
        out_shape=jax.ShapeDtypeStruct((n, m), out_dtype),
        scratch_shapes=[pltpu.VMEM((pieces[0].shape[1], tn), BF16) for pieces in xs],
        compiler_params=_params("arbitrary", "arbitrary"),
        name="matmul",
    )(*[p for pieces in xs for p in pieces], *[w for w, _ in ws], *residuals)


def _mixer_a_kernel(u_ref, v_ref, lng_ref, lnb_ref, ws_ref, bs_ref, o_ref):
    tm = u_ref.shape[0]
    gd = u_ref.shape[1] // A_GROUPS
    for c in range(tm // CHUNK):
        rows = slice(c * CHUNK, (c + 1) * CHUNK)
        u = jax.nn.gelu(u_ref[rows, :])
        v = jax.nn.gelu(v_ref[rows, :])
        mu = jnp.mean(v, axis=-1, keepdims=True)
        vc = v - mu
        var = jnp.mean(vc * vc, axis=-1, keepdims=True)
        v = vc * lax.rsqrt(var + EPS) * lng_ref[...] + lnb_ref[...]
        for g in range(A_GROUPS):
            cols = slice(g * gd, (g + 1) * gd)
            f = jnp.dot(ws_ref[g].astype(BF16), v[:, cols].astype(BF16), preferred_element_type=F32)
            f = f + bs_ref[g]
            o_ref[rows, cols] = (u[:, cols] * f).astype(o_ref.dtype)


def mixer_a(z, ln_g, ln_b, w_s, b_s, tm=512):
    n = z.shape[0]
    aw = ln_g.shape[0]
    gd = aw // A_GROUPS
    bs_b = jnp.broadcast_to(b_s[:, :, None], (A_GROUPS, CHUNK, gd))
    return pl.pallas_call(
        _mixer_a_kernel,
        grid=(n // tm,),
        in_specs=[
            pl.BlockSpec((tm, aw), lambda i: (i, 0)),
            pl.BlockSpec((tm, aw), lambda i: (i, 1)),
            pl.BlockSpec((1, aw), lambda i: (0, 0)),
            pl.BlockSpec((1, aw), lambda i: (0, 0)),
            pl.BlockSpec((A_GROUPS, CHUNK, CHUNK), lambda i: (0, 0, 0)),
            pl.BlockSpec((A_GROUPS, CHUNK, gd), lambda i: (0, 0, 0)),
        ],
        out_specs=pl.BlockSpec((tm, aw), lambda i: (i, 0)),
        out_shape=jax.ShapeDtypeStruct((n, aw), BF16),
        compiler_params=_params("parallel"),
        name="mixer_a",
    )(z, z, ln_g.reshape(1, aw), ln_b.reshape(1, aw), w_s, bs_b)


def _skew(w, rows):
    n = w.shape[-1]
    flat = jnp.tile(w, (1,) * (w.ndim - 1) + (rows,))
    return flat[..., :rows * (n - 1)].reshape(w.shape[:-1] + (rows, n - 1))


def _pair_rmsnorm(x, gain):
    sq = x * x
    lane = lax.broadcasted_iota(jnp.int32, x.shape, 1)
    lo = lane < B_DH
    s_lo = jnp.sum(jnp.where(lo, sq, 0.0), axis=-1, keepdims=True)
    s_hi = jnp.sum(jnp.where(lo, 0.0, sq), axis=-1, keepdims=True)
    ms = jnp.where(lo, s_lo, s_hi) * (1.0 / B_DH)
    return x * lax.rsqrt(ms + EPS) * gain


def _natten_kernel(q_ref, k_ref, v_ref, gq_ref, gk_ref, bias_ref, o_ref, kn_ref, vb_ref, s_ref, m_ref, e_ref,
                   *, rows_per_step, rows):
    seq = k_ref.shape[0]
    t = pl.program_id(2)
    prep = 256

    @pl.when(t == 0)
    def _():
        def body(c, carry):
            sl = pl.ds(pl.multiple_of(c * prep, prep), prep)
            kn_ref[sl, :] = _pair_rmsnorm(k_ref[sl, :], gk_ref[...]).astype(BF16)
            vb_ref[sl, :LANES] = v_ref[sl, :].astype(BF16)
            vb_ref[sl, LANES:] = jnp.ones((prep, LANES), BF16)
            return carry
        lax.fori_loop(0, seq // prep, body, 0, unroll=PREP_UNROLL)

    q = (_pair_rmsnorm(q_ref[...], gq_ref[...]) * (B_DH ** -0.5)).astype(BF16)
    lo = lax.broadcasted_iota(jnp.int32, (GRID_W, LANES), 1) < B_DH
    zero = jnp.zeros((GRID_W, LANES), BF16)
    win = NA_KH * GRID_W

    def window(rr):
        r = t * rows_per_step + rr
        rs = jnp.clip(r - NA_KH // 2, 0, rows - NA_KH)
        return r - rs, pl.ds(pl.multiple_of(rs * GRID_W, GRID_W), win)

    for rr in range(rows_per_step):
        case, ksl = window(rr)
        qr = q[rr * GRID_W:(rr + 1) * GRID_W, :]
        q2 = jnp.concatenate([jnp.where(lo, qr, zero), jnp.where(lo, zero, qr)], axis=0)
        s = lax.dot_general(q2, kn_ref[ksl, :], (((1,), (1,)), ((), ())), preferred_element_type=F32)
        s_ref[rr] = s + bias_ref[0, case]
    for rr in range(rows_per_step):
        m = jnp.max(s_ref[rr], axis=-1, keepdims=True)
        m_ref[rr] = jnp.broadcast_to(m, (2 * GRID_W, LANES))
    for rr in range(rows_per_step):
        mb = m_ref[rr]
        for c in range(win // LANES):
            cols = slice(c * LANES, (c + 1) * LANES)
            e_ref[rr, :, cols] = jnp.exp(s_ref[rr, :, cols] - mb).astype(BF16)
    for rr in range(rows_per_step):
        _, ksl = window(rr)
        ol = jnp.dot(e_ref[rr], vb_ref[ksl, :], preferred_element_type=F32)
        o = ol[:, :LANES] / ol[:, LANES:]
        o_ref[rr * GRID_W:(rr + 1) * GRID_W, :] = jnp.where(lo, o[:GRID_W], o[GRID_W:]).astype(o_ref.dtype)


def natten_bias_table(rpb):
    heads = rpb.shape[0]
    col = jnp.arange(GRID_W)
    col_start = jnp.clip(col - NA_KW // 2, 0, GRID_W - NA_KW)
    col_valid = (col[None, :] >= col_start[:, None]) & (col[None, :] < col_start[:, None] + NA_KW)
    edge = GRID_W - NA_KW
    w = jnp.concatenate([jnp.repeat(rpb[..., :1], edge, axis=-1), rpb.astype(F32),
                         jnp.repeat(rpb[..., -1:], edge, axis=-1), jnp.zeros(rpb.shape[:-1] + (1,), F32)], axis=-1)
    bias_c = _skew(w, GRID_W)[..., GRID_W - 1:]
    tab = jnp.stack([bias_c[:, NA_KH - 1 - c:2 * NA_KH - 1 - c] for c in range(NA_KH)], axis=1)
    tab = jnp.where(col_valid[:, None, :], tab.transpose(0, 1, 3, 2, 4), NEG)
    tab = tab.reshape(heads // 2, 2, NA_KH, GRID_W, NA_KH * GRID_W).transpose(0, 2, 1, 3, 4)
    return tab.reshape(heads // 2, NA_KH, 2 * GRID_W, NA_KH * GRID_W)


def natten(z, col0, row0, batch, seq, gq, gk, bias_tab, rows_per_step=16):
    hp = bias_tab.shape[0]
    rows = seq // GRID_W
    assert rows >= NA_KH and rows % rows_per_step == 0 and row0 % seq == 0
    tq = rows_per_step * GRID_W
    steps = rows // rows_per_step
    qb0 = row0 // tq
    sb0 = row0 // seq
    win = NA_KH * GRID_W
    kern = functools.partial(_natten_kernel, rows_per_step=rows_per_step, rows=rows)
    gq2 = jnp.tile(gq, 2).reshape(1, 2 * B_DH)
    gk2 = jnp.tile(gk, 2).reshape(1, 2 * B_DH)
    return pl.pallas_call(
        kern,
        grid=(batch, hp, steps),
        in_specs=[
            pl.BlockSpec((tq, LANES), lambda b, h, t: (qb0 + b * steps + t, col0 + h)),
            pl.BlockSpec((seq, LANES), lambda b, h, t: (sb0 + b, col0 + hp + h)),
            pl.BlockSpec((seq, LANES), lambda b, h, t: (sb0 + b, col0 + 2 * hp + h)),
            pl.BlockSpec((1, LANES), lambda b, h, t: (0, 0)),
            pl.BlockSpec((1, LANES), lambda b, h, t: (0, 0)),
            pl.BlockSpec((1, NA_KH, 2 * GRID_W, win), lambda b, h, t: (h, 0, 0, 0)),
        ],
        out_specs=pl.BlockSpec((tq, LANES), lambda b, h, t: (b * steps + t, h)),
        out_shape=jax.ShapeDtypeStruct((batch * seq, 2 * hp * B_DH), BF16),
        scratch_shapes=[
            pltpu.VMEM((seq, LANES), BF16),
            pltpu.VMEM((seq, 2 * LANES), BF16),
            pltpu.VMEM((rows_per_step, 2 * GRID_W, win), F32),
            pltpu.VMEM((rows_per_step, 2 * GRID_W, LANES), F32),
            pltpu.VMEM((rows_per_step, 2 * GRID_W, win), BF16),
        ],
        compiler_params=_params("parallel", "parallel", "arbitrary"),
        name="natten",
    )(z, z, z, gq2, gk2, bias_tab)


def t5_bucket(rel):
    nb = T5_BUCKETS // 2
    max_exact = nb // 2
    ret = jnp.where(rel > 0, nb, 0)
    n = jnp.abs(rel)
    large = max_exact + (jnp.log(jnp.maximum(n, 1).astype(F32) / max_exact)
                         / math.log(T5_MAX_DIST / max_exact) * (nb - max_exact)).astype(jnp.int32)
    large = jnp.minimum(large, nb - 1)
    return ret + jnp.where(n < max_exact, n, large)


def dilated_bias_table(t5_table, dil):
    span = C_TK - 1
    rel = jnp.arange(-span, span + 1)
    vec = t5_table.astype(F32)[t5_bucket(rel * dil)]
    vec = jnp.where((jnp.abs(rel) <= C_RAD)[:, None], vec, NEG).T
    w = jnp.concatenate([vec, jnp.zeros((vec.shape[0], 1), F32)], axis=1)
    r = _skew(w, C_TQ)
    return jnp.stack([r[:, :, span - off:span - off + C_TK] for off in (0, C_RAD, 2 * C_RAD)], axis=1)


def _row_rmsnorm(x, gain):
    return x * lax.rsqrt(jnp.mean(x * x, axis=-1, keepdims=True) + EPS) * gain


def _dilated_kernel(q_ref, k_ref, v_ref, gq_ref, gk_ref, b0_ref, b1_ref, b2_ref, o_ref,
                    kn_ref, qn_ref, oacc_ref, lse_ref, s_ref, m_ref, e_ref, *kv_by_residue, seq, unroll):
    t = pl.program_id(2)
    prep = 256
    kv_sub = {dil: (kv_by_residue[2 * i], kv_by_residue[2 * i + 1])
              for i, dil in enumerate(d for _, d in C_PATTERNS if d > 1)}

    @pl.when(t == 0)
    def _():
        def body(c, carry):
            sl = pl.ds(pl.multiple_of(c * prep, prep), prep)
            kn_ref[sl, :] = _row_rmsnorm(k_ref[sl, :], gk_ref[...])
            return carry
        lax.fori_loop(0, seq // prep, body, 0, unroll=PREP_UNROLL)

        for dil, (ks_ref, vs_ref) in kv_sub.items():
            rows = min(prep, seq // dil)
            for s in range(dil):
                def split(c, carry, dil=dil, s=s, rows=rows, ks_ref=ks_ref, vs_ref=vs_ref):
                    src = pl.ds(c * rows * dil + s, rows, stride=dil)
                    dst = pl.ds(pl.multiple_of(c * rows, rows), rows)
                    ks_ref[s, dst, :] = kn_ref[src, :].astype(BF16)
                    vs_ref[s, dst, :] = v_ref[src, :].astype(BF16)
                    return carry
                lax.fori_loop(0, seq // (dil * rows), split, 0, unroll=True)

    qn_ref[...] = _row_rmsnorm(q_ref[...], gq_ref[...]) * (C_DH ** -0.5)
    n_tiles = C_SUPER // C_TQ

    for p, ((_, dil), b_ref) in enumerate(zip(C_PATTERNS, (b0_ref, b1_ref, b2_ref))):
        n_sub = seq // dil
        tk = min(C_TK, n_sub)
        span = C_TQ * dil
        ones = jnp.ones((tk, LANES), BF16)

        def tile(idx, dil=dil, n_sub=n_sub, tk=tk, span=span):
            u = idx // dil
            s = idx % dil
            qpos = u * span + s
            j0 = t * (C_SUPER // dil) + u * C_TQ
            ws = jnp.clip(j0 - C_RAD, 0, n_sub - tk)
            ksl = pl.ds(pl.multiple_of(ws, C_RAD), tk)
            if dil == 1:
                return pl.ds(pl.multiple_of(qpos, C_TQ), C_TQ), s, ksl, (j0 - ws) // C_RAD
            return pl.ds(qpos, C_TQ, stride=dil), s, ksl, (j0 - ws) // C_RAD

        def keys(s, ksl, dil=dil):
            return kn_ref[ksl, :].astype(BF16) if dil == 1 else kv_sub[dil][0][s, ksl, :]

        def values(s, ksl, dil=dil):
            return v_ref[ksl, :].astype(BF16) if dil == 1 else kv_sub[dil][1][s, ksl, :]

        def scores(idx, carry, tile=tile, keys=keys, tk=tk, b_ref=b_ref):
            qsl, s, ksl, case = tile(idx)
            q = qn_ref[qsl, :].astype(BF16)
            sc = lax.dot_general(q, keys(s, ksl), (((1,), (1,)), ((), ())), preferred_element_type=F32)
            s_ref[idx, :, :tk] = sc + b_ref[0, case, :, :tk]
            return carry

        def row_max(idx, carry, tk=tk):
            m = jnp.max(s_ref[idx, :, :tk], axis=-1, keepdims=True)
            m_ref[idx] = jnp.broadcast_to(m, (C_TQ, LANES))
            return carry

        def exps(idx, carry, tk=tk):
            mb = m_ref[idx]
            for c in range(tk // LANES):
                cols = slice(c * LANES, (c + 1) * LANES)
                e_ref[idx, :, cols] = jnp.exp(s_ref[idx, :, cols] - mb).astype(BF16)
            return carry

        def outputs(idx, carry, tile=tile, values=values, tk=tk, ones=ones, p=p):
            qsl, s, ksl, _ = tile(idx)
            v1 = jnp.concatenate([values(s, ksl), ones], axis=1)
            ol = jnp.dot(e_ref[idx, :, :tk], v1, preferred_element_type=F32)
            l = ol[:, LANES:]
            oacc_ref[p, qsl, :] = ol[:, :LANES] / l
            lse_ref[p, qsl, :] = m_ref[idx] + jnp.log(l)
            return carry

        for phase in (scores, row_max, exps, outputs):
            lax.fori_loop(0, n_tiles, phase, 0, unroll=unroll)

    l0, l1, l2 = lse_ref[0], lse_ref[1], lse_ref[2]
    mx = jnp.maximum(jnp.maximum(l0, l1), l2)
    w0, w1, w2 = jnp.exp(l0 - mx), jnp.exp(l1 - mx), jnp.exp(l2 - mx)
    num = w0 * oacc_ref[0] + w1 * oacc_ref[1] + w2 * oacc_ref[2]
    o_ref[...] = (num / (w0 + w1 + w2)).astype(o_ref.dtype)


def dilated_attention(qkv, row0, batch, seq, gq, gk, tabs, unroll=16):
    heads = tabs[0].shape[0]
    assert seq % C_SUPER == 0 and row0 % seq == 0
    steps = seq // C_SUPER
    qb0 = row0 // C_SUPER
    sb0 = row0 // seq
    n_tiles = C_SUPER // C_TQ
    kern = functools.partial(_dilated_kernel, seq=seq, unroll=unroll)
    tab_specs = [pl.BlockSpec((1,) + tuple(tb.shape[1:]), lambda b, h, t: (h, 0, 0, 0)) for tb in tabs]
    return pl.pallas_call(
        kern,
        grid=(batch, heads, steps),
        in_specs=[
            pl.BlockSpec((C_SUPER, LANES), lambda b, h, t: (qb0 + b * steps + t, h)),
            pl.BlockSpec((seq, LANES), lambda b, h, t: (sb0 + b, heads + h)),
            pl.BlockSpec((seq, LANES), lambda b, h, t: (sb0 + b, 2 * heads + h)),
            pl.BlockSpec((1, LANES), lambda b, h, t: (0, 0)),
            pl.BlockSpec((1, LANES), lambda b, h, t: (0, 0)),
        ] + tab_specs,
        out_specs=pl.BlockSpec((C_SUPER, LANES), lambda b, h, t: (b * steps + t, h)),
        out_shape=jax.ShapeDtypeStruct((batch * seq, heads * C_DH), BF16),
        scratch_shapes=[
            pltpu.VMEM((seq, LANES), F32),
            pltpu.VMEM((C_SUPER, LANES), F32),
            pltpu.VMEM((len(C_PATTERNS), C_SUPER, LANES), F32),
            pltpu.VMEM((len(C_PATTERNS), C_SUPER, LANES), F32),
            pltpu.VMEM((n_tiles, C_TQ, C_TK), F32),
            pltpu.VMEM((n_tiles, C_TQ, LANES), F32),
            pltpu.VMEM((n_tiles, C_TQ, C_TK), BF16),
        ] + [pltpu.VMEM((dil, seq // dil, LANES), BF16) for _, dil in C_PATTERNS if dil > 1 for _ in "kv"],
        compiler_params=_params("parallel", "parallel", "arbitrary"),
        name="dilated",
    )(qkv, qkv, qkv, gq.reshape(1, C_DH), gk.reshape(1, C_DH), *tabs)


def _pack_bf16_pairs(lo, hi):
    lo_bits = lax.bitcast_convert_type(lo.astype(BF16).astype(F32), jnp.uint32)
    hi_bits = lax.bitcast_convert_type(hi.astype(BF16).astype(F32), jnp.uint32)
    return (hi_bits & jnp.uint32(0xFFFF0000)) | (lo_bits >> 16)


def _unpack_bf16_pairs(packed):
    lo = lax.bitcast_convert_type(packed << 16, F32)
    hi = lax.bitcast_convert_type(packed & jnp.uint32(0xFFFF0000), F32)
    return lo, hi


META_E0, META_E1, META_R0, META_R1, META_G0, META_G1 = range(6)


def _route_kernel(x_ref, g_ref, wr_ref, br_ref, hp_ref, meta_ref, tot_ref):
    x = x_ref[...]
    h = x * lax.rsqrt(jnp.mean(x * x, axis=-1, keepdims=True) + EPS) * g_ref[...]
    half = h.shape[1] // 2
    hp_ref[...] = _pack_bf16_pairs(h[:, :half], h[:, half:])
    h_hi = h.astype(BF16)
    h_lo = (h - h_hi.astype(F32)).astype(BF16)
    w_hi = wr_ref[...].astype(BF16)
    w_lo = (wr_ref[...] - w_hi.astype(F32)).astype(BF16)
    hh = jnp.dot(h_hi, jnp.concatenate([w_hi, w_lo], axis=1), preferred_element_type=F32)
    logits = (hh[:, :ROUTER_LANES] + hh[:, ROUTER_LANES:]
              + jnp.dot(h_lo, w_hi, preferred_element_type=F32)) + br_ref[...]
    tm = x.shape[0]
    lane = lax.broadcasted_iota(jnp.int32, (tm, ROUTER_LANES), 1)
    lane_f = lane.astype(F32)

    def first_max(v):
        mx = jnp.max(v, axis=-1, keepdims=True)
        return mx, jnp.min(jnp.where(v == mx, lane_f, float(ROUTER_LANES)), axis=-1, keepdims=True)

    lg = jnp.where(lane < MOE_GROUPS, logits, -jnp.inf)
    gmax, gsel = first_max(lg)
    p_g = 1.0 / jnp.sum(jnp.exp(lg - gmax), axis=-1, keepdims=True)
    expert_lane = (lane >= MOE_GROUPS) & (lane < MOE_GROUPS + MOE_EXPERTS)
    in_group = expert_lane & (((lane - MOE_GROUPS) // MOE_PER_GROUP) == gsel.astype(jnp.int32))
    le = jnp.where(in_group, logits, -jnp.inf)
    v0, i0 = first_max(le)
    pick0 = lane_f == i0
    v1, i1 = first_max(jnp.where(pick0, -jnp.inf, le))
    pick1 = lane_f == i1
    t = jnp.exp(v1 - v0)
    g0 = p_g / (1.0 + t)
    g1 = p_g * t / (1.0 + t)

    onehot = jnp.where(pick0 | pick1, 1.0, 0.0).astype(BF16)
    row = lax.broadcasted_iota(jnp.int32, (tm, tm), 0)
    col = lax.broadcasted_iota(jnp.int32, (tm, tm), 1)
    earlier = jnp.where(col < row, 1.0, 0.0).astype(BF16)
    before = jnp.dot(earlier, onehot, preferred_element_type=F32)
    r0 = jnp.sum(jnp.where(pick0, before, 0.0), axis=-1, keepdims=True)
    r1 = jnp.sum(jnp.where(pick1, before, 0.0), axis=-1, keepdims=True)

    meta = jnp.zeros((tm, ROUTER_LANES), F32)
    for pos, val in ((META_E0, i0 - MOE_GROUPS), (META_E1, i1 - MOE_GROUPS), (META_R0, r0), (META_R1, r1),
                     (META_G0, g0), (META_G1, g1)):
        meta = jnp.where(lane == pos, val, meta)
    meta_ref[...] = meta
    tot_ref[0] = jnp.sum(onehot.astype(F32), axis=0, keepdims=True)


def moe_route(x, g, wr, br, tm):
    n, d = x.shape
    return pl.pallas_call(
        _route_kernel,
        grid=(n // tm,),
        in_specs=[
            pl.BlockSpec((tm, d), lambda i: (i, 0)),
            pl.BlockSpec((1, d), lambda i: (0, 0)),
            pl.BlockSpec((d, ROUTER_LANES), lambda i: (0, 0)),
            pl.BlockSpec((1, ROUTER_LANES), lambda i: (0, 0)),
        ],
        out_specs=[
            pl.BlockSpec((tm, d // 2), lambda i: (i, 0)),
            pl.BlockSpec((tm, ROUTER_LANES), lambda i: (i, 0)),
            pl.BlockSpec((1, 1, ROUTER_LANES), lambda i: (i, 0, 0)),
        ],
        out_shape=[
            jax.ShapeDtypeStruct((n, d // 2), jnp.uint32),
            jax.ShapeDtypeStruct((n, ROUTER_LANES), F32),
            jax.ShapeDtypeStruct((n // tm, 1, ROUTER_LANES), F32),
        ],
        compiler_params=_params("parallel"),
        name="moe_route",
    )(x, g.reshape(1, d), wr, br)


def _dispatch_kernel(d0_ref, d1_ref, pad_first_ref, pad_cnt_ref, nused_ref, h_ref, xb_hbm, sem):
    i = pl.program_id(0)
    tm = h_ref.shape[0]
    base = i * tm

    def row_copy(src_row, dst_row, level=0):
        return pltpu.make_async_copy(h_ref.at[pl.ds(src_row, 1), :], xb_hbm.at[pl.ds(dst_row, 1), :], sem.at[level])

    for r in range(tm):
        row_copy(r, d0_ref[base + r], 0).start(priority=(2 * (r % 2)) % DMA_PRIORITY_LEVELS)
        row_copy(r, d1_ref[base + r], 1).start(priority=(2 * (r % 2) + 1) % DMA_PRIORITY_LEVELS)

    def drain(r, carry):
        row_copy(0, 0, 0).wait()
        row_copy(0, 0, 1).wait()
        return carry
    lax.fori_loop(0, tm, drain, 0, unroll=8)

    @pl.when(i == pl.num_programs(0) - 1)
    def _():
        def per_expert(e, carry):
            def fill(j, c):
                row_copy(0, pad_first_ref[e] + j).start()
                return c
            lax.fori_loop(0, pad_cnt_ref[e], fill, 0)

            def fill_wait(j, c):
                row_copy(0, 0).wait()
                return c
            lax.fori_loop(0, pad_cnt_ref[e], fill_wait, 0)
            return carry
        lax.fori_loop(0, MOE_EXPERTS, per_expert, 0)

        def block_copy(b):
            return pltpu.make_async_copy(h_ref.at[pl.ds(0, MOE_TM), :],
                                         xb_hbm.at[pl.ds(pl.multiple_of(b * MOE_TM, MOE_TM), MOE_TM), :], sem.at[0])

        def fill_block(b, c):
            block_copy(b).start()
            return c
        lax.fori_loop(nused_ref[0], xb_hbm.shape[0] // MOE_TM, fill_block, 0)

        def fill_block_wait(b, c):
            block_copy(b).wait()
            return c
        lax.fori_loop(nused_ref[0], xb_hbm.shape[0] // MOE_TM, fill_block_wait, 0)


def moe_dispatch(hp, d0, d1, pad_first, pad_cnt, n_used, p, tm=512):
    n, dw = hp.shape
    assert tm >= MOE_TM and n % tm == 0
    return pl.pallas_call(
        _dispatch_kernel,
        grid_spec=pltpu.PrefetchScalarGridSpec(
            num_scalar_prefetch=5,
            grid=(n // tm,),
            in_specs=[pl.BlockSpec((tm, dw), lambda i, *_: (i, 0))],
            out_specs=pl.BlockSpec(memory_space=pl.ANY),
            scratch_shapes=[pltpu.SemaphoreType.DMA((2,))],
        ),
        out_shape=jax.ShapeDtypeStruct((p, dw), hp.dtype),
        compiler_params=_params("arbitrary"),
        name="moe_dispatch",
    )(d0, d1, pad_first, pad_cnt, n_used, hp)


def _row_gather_copy(src_hbm, dst_ref, sem, src_row, dst_row):
    return pltpu.make_async_copy(src_hbm.at[pl.ds(src_row, 1), :], dst_ref.at[pl.ds(dst_row, 1), :], sem)


def _gather_row_sets(gathers, base, src_hbm, unroll=8):
    rows = gathers[0][1].shape[0]
    for r in range(rows):
        for k, (idx_ref, dst_ref, sem) in enumerate(gathers):
            level = (len(gathers) * (r % 2) + k) % DMA_PRIORITY_LEVELS
            _row_gather_copy(src_hbm, dst_ref, sem, idx_ref[base + r], r).start(priority=level)

    def drain(r, carry):
        for _, dst_ref, sem in gathers:
            _row_gather_copy(src_hbm, dst_ref, sem, 0, r).wait()
        return carry
    lax.fori_loop(0, rows, drain, 0, unroll=unroll)


def _expert_weight_copies(layer, e, w_hbm, stage, sem):
    return [pltpu.make_async_copy(w.at[layer, e], s, sem.at[k]) for k, (w, s) in enumerate(zip(w_hbm, stage))]


def _ffn_kernel(be_ref, first_ref, next_ref, nused_ref, x_ref, wg_hbm, wu_hbm, wd_hbm, o_ref,
                sg_ref, su_ref, sd_ref, wg_ref, wu_ref, wd_ref, sem, *, layer, k_chunk):
    i = pl.program_id(0)
    w_hbm = (wg_hbm, wu_hbm, wd_hbm)
    stage = (sg_ref, su_ref, sd_ref)
    resident = (wg_ref, wu_ref, wd_ref)

    @pl.when(i == 0)
    def _():
        for cp in _expert_weight_copies(layer, be_ref[0], w_hbm, stage, sem):
            cp.start()

    @pl.when(first_ref[i] == 1)
    def _():
        for cp in _expert_weight_copies(layer, be_ref[i], w_hbm, stage, sem):
            cp.wait()
        for s_ref, w_ref in zip(stage, resident):
            rows = math.gcd(k_chunk, s_ref.shape[0])

            def cast(c, carry, s_ref=s_ref, w_ref=w_ref, rows=rows):
                sl = pl.ds(pl.multiple_of(c * rows, rows), rows)
                w_ref[sl, :] = s_ref[sl, :].astype(BF16)
                return carry
            lax.fori_loop(0, s_ref.shape[0] // rows, cast, 0, unroll=True)

        @pl.when(next_ref[i] >= 0)
        def _():
            for cp in _expert_weight_copies(layer, next_ref[i], w_hbm, stage, sem):
                cp.start(priority=1)

    live = i < nused_ref[0]

    @pl.when(live)
    def _():
        lo, hi = _unpack_bf16_pairs(x_ref[...])
        lo, hi = lo.astype(BF16), hi.astype(BF16)
        half = lo.shape[1]
        g = (jnp.dot(lo, wg_ref[:half, :], preferred_element_type=F32)
             + jnp.dot(hi, wg_ref[half:, :], preferred_element_type=F32))
        u = (jnp.dot(lo, wu_ref[:half, :], preferred_element_type=F32)
             + jnp.dot(hi, wu_ref[half:, :], preferred_element_type=F32))
        h = (jax.nn.silu(g) * u).astype(BF16)
        y = jnp.dot(h, wd_ref[...], preferred_element_type=F32)
        o_ref[...] = _pack_bf16_pairs(y[:, :half], y[:, half:])

    @pl.when(jnp.logical_not(live))
    def _():
        o_ref[...] = jnp.zeros_like(o_ref)


def expert_ffn(xb, block_e, first, next_e, n_used, wg, wu, wd, layer, tm=MOE_TM):
    p, dw = xb.shape
    d, de = wg.shape[2], wg.shape[3]
    any_spec = pl.BlockSpec(memory_space=pl.ANY)
    return pl.pallas_call(
        functools.partial(_ffn_kernel, layer=layer, k_chunk=256),
        grid_spec=pltpu.PrefetchScalarGridSpec(
            num_scalar_prefetch=4,
            grid=(p // tm,),
            in_specs=[
                pl.BlockSpec((tm, dw), lambda i, be, fi, ne, nu: (jnp.minimum(i, nu[0] - 1), 0)),
                any_spec, any_spec, any_spec,
            ],
            out_specs=pl.BlockSpec((tm, dw), lambda i, *_: (i, 0)),
            scratch_shapes=[
                pltpu.VMEM((d, de), F32), pltpu.VMEM((d, de), F32), pltpu.VMEM((de, d), F32),
                pltpu.VMEM((d, de), BF16), pltpu.VMEM((d, de), BF16), pltpu.VMEM((de, d), BF16),
                pltpu.SemaphoreType.DMA((3,)),
            ],
        ),
        out_shape=jax.ShapeDtypeStruct((p, dw), jnp.uint32),
        compiler_params=_params("arbitrary"),
        name="moe_ffn",
    )(block_e, first, next_e, n_used, xb, wg, wu, wd)


def _combine_kernel(d0_ref, d1_ref, x_ref, meta_ref, yb_hbm, *rest, row0, with_norm):
    if with_norm:
        gain_ref, o_ref, h_ref, a_ref, b_ref, sem = rest
    else:
        o_ref, a_ref, b_ref, sem = rest
    base = row0 + pl.program_id(0) * x_ref.shape[0]
    _gather_row_sets([(d0_ref, a_ref, sem.at[0]), (d1_ref, b_ref, sem.at[1])], base, yb_hbm)
    a_lo, a_hi = _unpack_bf16_pairs(a_ref[...])
    b_lo, b_hi = _unpack_bf16_pairs(b_ref[...])
    g0 = meta_ref[:, META_G0:META_G0 + 1]
    g1 = meta_ref[:, META_G1:META_G1 + 1]
    half = a_lo.shape[1]
    lo = x_ref[:, :half] + (g0 * a_lo + g1 * b_lo)
    hi = x_ref[:, half:] + (g0 * a_hi + g1 * b_hi)
    o_ref[:, :half] = lo
    o_ref[:, half:] = hi
    if with_norm:
        ms = (jnp.sum(lo * lo, axis=-1, keepdims=True) + jnp.sum(hi * hi, axis=-1, keepdims=True)) / (2 * half)
        inv = lax.rsqrt(ms + EPS)
        h_ref[:, :half] = (lo * inv * gain_ref[:, :half]).astype(h_ref.dtype)
        h_ref[:, half:] = (hi * inv * gain_ref[:, half:]).astype(h_ref.dtype)


def moe_combine(x, meta, yb, d0, d1, row0, rows, next_gain=None, tm=512):
    d = x.shape[1]
    assert row0 % tm == 0
    blk0 = row0 // tm
    with_norm = next_gain is not None
    row_spec = pl.BlockSpec((tm, d), lambda i, a, b: (i, 0))
    in_specs = [
        pl.BlockSpec((tm, d), lambda i, a, b: (blk0 + i, 0)),
        pl.BlockSpec((tm, ROUTER_LANES), lambda i, a, b: (blk0 + i, 0)),
        pl.BlockSpec(memory_space=pl.ANY),
    ]
    args = [d0, d1, x, meta, yb]
    out_specs, out_shape = row_spec, jax.ShapeDtypeStruct((rows, d), F32)
    if with_norm:
        in_specs.append(pl.BlockSpec((1, d), lambda i, a, b: (0, 0)))
        args.append(next_gain.reshape(1, d))
        out_specs, out_shape = [row_spec, row_spec], [out_shape, jax.ShapeDtypeStruct((rows, d), BF16)]
    return pl.pallas_call(
        functools.partial(_combine_kernel, row0=row0, with_norm=with_norm),
        grid_spec=pltpu.PrefetchScalarGridSpec(
            num_scalar_prefetch=2,
            grid=(rows // tm,),
            in_specs=in_specs,
            out_specs=out_specs,
            scratch_shapes=[pltpu.VMEM((tm, d // 2), jnp.uint32), pltpu.VMEM((tm, d // 2), jnp.uint32),
                            pltpu.SemaphoreType.DMA((2,))],
        ),
        out_shape=out_shape,
        compiler_params=_params("arbitrary"),
        name="moe_combine",
    )(*args)


def moe_layout(meta, tot, tile, tm=MOE_TM):
    n = meta.shape[0]
    m = n * MOE_TOPK
    ids = jnp.arange(MOE_EXPERTS, dtype=jnp.int32)
    cnt = tot[:, 0, MOE_GROUPS:MOE_GROUPS + MOE_EXPERTS].astype(jnp.int32)
    counts = jnp.sum(cnt, axis=0)
    padded = (counts + tm - 1) // tm * tm
    pad_end = jnp.cumsum(padded)
    pad_start = pad_end - padded
    base = pad_start[None, :] + jnp.cumsum(cnt, axis=0) - cnt
    base_tok = jnp.repeat(base, tile, axis=0)

    def slots(e_col, r_col):
        e = meta[:, e_col].astype(jnp.int32)
        return jnp.sum(jnp.where(e[:, None] == ids[None, :], base_tok, 0), axis=1) + meta[:, r_col].astype(jnp.int32)

    d0 = slots(META_E0, META_R0)
    d1 = slots(META_E1, META_R1)
    n_blocks = (m + MOE_EXPERTS * (tm - 1) + tm - 1) // tm

    blk = jnp.arange(n_blocks, dtype=jnp.int32)
    n_used = (pad_end[-1] // tm).astype(jnp.int32)
    block_e = jnp.sum((pad_end[None, :] <= (blk * tm)[:, None]).astype(jnp.int32), axis=1)
    block_e = jnp.minimum(block_e, MOE_EXPERTS - 1)
    live = blk < n_used
    first = (live & ((blk == 0) | (block_e != jnp.roll(block_e, 1)))).astype(jnp.int32)
    has_rows = counts > 0
    later = jnp.arange(MOE_EXPERTS)[None, :] > jnp.arange(MOE_EXPERTS)[:, None]
    nxt_of_e = jnp.min(jnp.where(later & has_rows[None, :], jnp.arange(MOE_EXPERTS)[None, :], MOE_EXPERTS), axis=1)
    nxt_of_e = jnp.where(nxt_of_e == MOE_EXPERTS, -1, nxt_of_e).astype(jnp.int32)
    next_e = jnp.sum(jnp.where(block_e[:, None] == ids[None, :], nxt_of_e[None, :], 0), axis=1)
    pad_first = (pad_start + counts).astype(jnp.int32)
    pad_cnt = (padded - counts).astype(jnp.int32)
    return d0, d1, pad_first, pad_cnt, n_blocks * tm, block_e, first, next_e, n_used.reshape(1)


def hier_moe_residual(x, g, wr_g, br_g, wr_e, br_e, w_gate, w_up, w_down, layer, out_segments=None,
                      next_gain=None, tile=512):
    n, d = x.shape
    pad = ROUTER_LANES - MOE_GROUPS - MOE_EXPERTS
    wr = jnp.concatenate([wr_g, wr_e, jnp.zeros((d, pad), F32)], axis=1)
    br = jnp.concatenate([br_g, br_e, jnp.zeros((pad,), F32)]).reshape(1, ROUTER_LANES)
    hp, meta, tot = moe_route(x, g, wr, br, tile)
    d0, d1, pad_first, pad_cnt, p, block_e, first, next_e, n_used = moe_layout(meta, tot, tile)
    xb = moe_dispatch(hp, d0, d1, pad_first, pad_cnt, n_used, p)
    yb = expert_ffn(xb, block_e, first, next_e, n_used, w_gate, w_up, w_down, layer)
    if out_segments is None:
        return moe_combine(x, meta, yb, d0, d1, 0, n, next_gain)
    return [moe_combine(x, meta, yb, d0, d1, row0, rows) for row0, rows in out_segments]


def kernel(x_prompt, x_sample, norm_mix, norm_ffn, ev_w_in, ev_w_out, a_ln_g, a_ln_b, a_w_s, a_b_s, b_q_gain, b_k_gain, b_rpb, od_w_in, od_w_out, c_q_gain, c_k_gain, t5_table, moe_wr_g, moe_br_g, moe_wr_e, moe_br_e, moe_w_gate, moe_w_up, moe_w_down):
    d = x_prompt.shape[-1]
    segs = [(x_prompt.shape[0], x_prompt.shape[1]), (x_sample.shape[0], x_sample.shape[1])]
    xs = [x_prompt.reshape(-1, d), x_sample.reshape(-1, d)]
    depth = norm_mix.shape[0]
    h = rmsnorm(xs, norm_mix[0], BF16)
    for l in range(depth):
        i = l // 2
        last = l == depth - 1
        if l % 2 == 0:
            aw = a_ln_g.shape[1]
            z = matmul([[h]], [(ev_w_in[i], 0)])
            a_out = mixer_a(z, a_ln_g[i], a_ln_b[i], a_w_s[i], a_b_s[i])
            tab = natten_bias_table(b_rpb[i])
            b_out, row0 = [], 0
            for batch, seq in segs:
                b_out.append(natten(z, 2 * aw // LANES, row0, batch, seq, b_q_gain[i], b_k_gain[i], tab))
                row0 += batch * seq
            x = matmul([[a_out], b_out], [(ev_w_out[i], 0), (ev_w_out[i], aw)], residuals=xs)
        else:
            qkv = matmul([[h]], [(od_w_in[i], 0)])
            tabs = [dilated_bias_table(t5_table, dil) for _, dil in C_PATTERNS]
            c_out, row0 = [], 0
            for batch, seq in segs:
                c_out.append(dilated_attention(qkv, row0, batch, seq, c_q_gain[i], c_k_gain[i], tabs))
                row0 += batch * seq
            x = matmul([c_out], [(od_w_out[i], 0)], residuals=xs)
        n0 = segs[0][0] * segs[0][1]
        res = hier_moe_residual(x, norm_ffn[l], moe_wr_g[l], moe_br_g[l], moe_wr_e[l], moe_br_e[l],
                                moe_w_gate, moe_w_up, moe_w_down, l,
                                out_segments=[(0, n0), (n0, segs[1][0] * segs[1][1])] if last else None,
                                next_gain=None if last else norm_mix[l + 1])
        if last:
            xs = res
        else:
            xs, h = [res[0]], res[1]
    return (xs[0].reshape(x_prompt.shape), xs[1].reshape(x_sample.shape))
```

```python
import functools
import math

import jax
import jax.numpy as jnp
from jax import lax
from jax.experimental import pallas as pl
from jax.experimental.pallas import tpu as pltpu

F32 = jnp.float32
BF16 = jnp.bfloat16
EPS = 1e-6
NEG = -1e30

V7X_VMEM_LIMIT_BYTES = 56 * 1024 * 1024
LANES = 128

GRID_W = 64
CHUNK = 128
A_GROUPS = 8
NA_KH = 8
NA_KW = 16
B_DH = 64
C_DH = 128
C_PATTERNS = ((128, 1), (512, 4), (2048, 16))
C_RAD = 64
C_TQ = 128
C_TK = C_TQ + 2 * C_RAD
C_SUPER = C_TQ * 16
T5_BUCKETS = 32
T5_MAX_DIST = 1024
MOE_GROUPS = 4
MOE_PER_GROUP = 8
MOE_EXPERTS = MOE_GROUPS * MOE_PER_GROUP
MOE_TOPK = 2
MOE_TM = 256
ROUTER_LANES = 128
PREP_UNROLL = 8
DMA_PRIORITY_LEVELS = 2


def _params(*sem):
    return pltpu.CompilerParams(dimension_semantics=sem, vmem_limit_bytes=V7X_VMEM_LIMIT_BYTES)


def _row_sources(arrays, tm, width, col_of):
    specs, spans, off = [], [], 0
    for a in arrays:
        assert a.shape[0] % tm == 0 and a.shape[0] >= tm
        nblk = a.shape[0] // tm
        specs.append(pl.BlockSpec(
            (tm, width), lambda *g, off=off, nblk=nblk: (jnp.clip(g[-1] - off, 0, nblk - 1), col_of(*g))))
        spans.append((off, nblk))
        off += nblk
    return specs, spans


def _active_rows(i, refs, spans):
    val = refs[-1][...]
    for ref, (off, nblk) in reversed(list(zip(refs[:-1], spans[:-1]))):
        val = jnp.where(i < off + nblk, ref[...], val)
    return val


def _rmsnorm_kernel(*refs, spans):
    x_refs, (g_ref, o_ref) = refs[:len(spans)], refs[len(spans):]
    x = _active_rows(pl.program_id(0), x_refs, spans)
    y = x * lax.rsqrt(jnp.mean(x * x, axis=-1, keepdims=True) + EPS)
    o_ref[...] = (y * g_ref[...]).astype(o_ref.dtype)


def rmsnorm(xs, g, out_dtype, tm=1024):
    d = xs[0].shape[1]
    n = sum(x.shape[0] for x in xs)
    specs, spans = _row_sources(xs, tm, d, lambda i: 0)
    return pl.pallas_call(
        functools.partial(_rmsnorm_kernel, spans=spans),
        grid=(n // tm,),
        in_specs=specs + [pl.BlockSpec((1, d), lambda i: (0, 0))],
        out_specs=pl.BlockSpec((tm, d), lambda i: (i, 0)),
        out_shape=jax.ShapeDtypeStruct((n, d), out_dtype),
        compiler_params=_params("arbitrary"),
        name="rmsnorm",
    )(*xs, g.reshape(1, d))


def _matmul_kernel(*refs, x_spans, res_spans, k_chunk):
    refs = list(refs)
    x_refs = [[refs.pop(0) for _ in spans] for spans in x_spans]
    w_refs = [refs.pop(0) for _ in x_spans]
    res_refs = [refs.pop(0) for _ in res_spans]
    o_ref, wb_refs = refs[0], refs[1:]
    i = pl.program_id(1)

    @pl.when(pl.program_id(1) == 0)
    def _():
        for w_ref, wb_ref in zip(w_refs, wb_refs):
            rows = math.gcd(k_chunk, w_ref.shape[0])

            def cast(c, carry, w_ref=w_ref, wb_ref=wb_ref, rows=rows):
                sl = pl.ds(pl.multiple_of(c * rows, rows), rows)
                wb_ref[sl, :] = w_ref[sl, :].astype(BF16)
                return carry
            lax.fori_loop(0, w_ref.shape[0] // rows, cast, 0, unroll=True)

    acc = None
    for pieces, spans, wb_ref in zip(x_refs, x_spans, wb_refs):
        part = jnp.dot(_active_rows(i, pieces, spans), wb_ref[...], preferred_element_type=F32)
        acc = part if acc is None else acc + part
    if res_spans:
        acc = _active_rows(i, res_refs, res_spans) + acc
    o_ref[...] = acc.astype(o_ref.dtype)


def _matmul_row_tile(xs, residuals, tn, candidates=(1024, 512, 256)):
    ks = [pieces[0].shape[1] for pieces in xs]
    weights = sum(2 * k * tn * 4 + k * tn * 2 for k in ks)
    for tm in candidates:
        rows_ok = all(p.shape[0] % tm == 0 for pieces in xs for p in pieces) and \
            all(r.shape[0] % tm == 0 for r in residuals)
        blocks = sum(2 * len(pieces) * tm * k * 2 for pieces, k in zip(xs, ks))
        blocks += (2 + 2 * len(residuals) + 1) * tm * tn * 4
        if rows_ok and weights + blocks <= 0.8 * V7X_VMEM_LIMIT_BYTES:
            return tm
    raise ValueError("no row tile fits")


def matmul(xs, ws, residuals=(), out_dtype=F32, tn=1024):
    n = sum(x.shape[0] for x in xs[0])
    m = ws[0][0].shape[1]
    tn = min(tn, m)
    tm = _matmul_row_tile(xs, residuals, tn)
    in_specs, x_spans, w_specs = [], [], []
    for pieces, (w, row) in zip(xs, ws):
        k = pieces[0].shape[1]
        assert row % k == 0
        specs, spans = _row_sources(pieces, tm, k, lambda j, i: 0)
        in_specs += specs
        x_spans.append(spans)
        w_specs.append(pl.BlockSpec((k, tn), lambda j, i, rb=row // k: (rb, j)))
    res_specs, res_spans = _row_sources(list(residuals), tm, tn, lambda j, i: j)
    kern = functools.partial(_matmul_kernel, x_spans=x_spans, res_spans=res_spans, k_chunk=256)
    return pl.pallas_call(
        kern,
        grid=(m // tn, n // tm),
        in_specs=in_specs + w_specs + res_specs,
        out_specs=pl.BlockSpec((tm, tn), lambda j, i: (i, j)),
        out_shape=jax.ShapeDtypeStruct((n, m), out_dtype),
        scratch_shapes=[pltpu.VMEM((pieces[0].shape[1], tn), BF16) for pieces in xs],
        compiler_params=_params("arbitrary", "arbitrary"),
        name="matmul",
    )(*[p for pieces in xs for p in pieces], *[w for w, _ in ws], *residuals)


def _mixer_a_kernel(u_ref, v_ref, lng_ref, lnb_ref, ws_ref, bs_ref, o_ref):
    tm = u_ref.shape[0]
    gd = u_ref.shape[1] // A_GROUPS
    for c in range(tm // CHUNK):
        rows = slice(c * CHUNK, (c + 1) * CHUNK)
        u = jax.nn.gelu(u_ref[rows, :])
        v = jax.nn.gelu(v_ref[rows, :])
        mu = jnp.mean(v, axis=-1, keepdims=True)
        vc = v - mu
        var = jnp.mean(vc * vc, axis=-1, keepdims=True)
        v = vc * lax.rsqrt(var + EPS) * lng_ref[...] + lnb_ref[...]
        for g in range(A_GROUPS):
            cols = slice(g * gd, (g + 1) * gd)
            f = jnp.dot(ws_ref[g].astype(BF16), v[:, cols].astype(BF16), preferred_element_type=F32)
            f = f + bs_ref[g]
            o_ref[rows, cols] = (u[:, cols] * f).astype(o_ref.dtype)


def mixer_a(z, ln_g, ln_b, w_s, b_s, tm=1024):
    n = z.shape[0]
    aw = ln_g.shape[0]
    gd = aw // A_GROUPS
    bs_b = jnp.broadcast_to(b_s[:, :, None], (A_GROUPS, CHUNK, gd))
    return pl.pallas_call(
        _mixer_a_kernel,
        grid=(n // tm,),
        in_specs=[
            pl.BlockSpec((tm, aw), lambda i: (i, 0)),
            pl.BlockSpec((tm, aw), lambda i: (i, 1)),
            pl.BlockSpec((1, aw), lambda i: (0, 0)),
            pl.BlockSpec((1, aw), lambda i: (0, 0)),
            pl.BlockSpec((A_GROUPS, CHUNK, CHUNK), lambda i: (0, 0, 0)),
            pl.BlockSpec((A_GROUPS, CHUNK, gd), lambda i: (0, 0, 0)),
        ],
        out_specs=pl.BlockSpec((tm, aw), lambda i: (i, 0)),
        out_shape=jax.ShapeDtypeStruct((n, aw), BF16),
        compiler_params=_params("parallel"),
        name="mixer_a",
    )(z, z, ln_g.reshape(1, aw), ln_b.reshape(1, aw), w_s, bs_b)


def _skew(w, rows):
    n = w.shape[-1]
    flat = jnp.tile(w, (1,) * (w.ndim - 1) + (rows,))
    return flat[..., :rows * (n - 1)].reshape(w.shape[:-1] + (rows, n - 1))


def _pair_rmsnorm(x, gain):
    sq = x * x
    lane = lax.broadcasted_iota(jnp.int32, x.shape, 1)
    lo = lane < B_DH
    s_lo = jnp.sum(jnp.where(lo, sq, 0.0), axis=-1, keepdims=True)
    s_hi = jnp.sum(jnp.where(lo, 0.0, sq), axis=-1, keepdims=True)
    ms = jnp.where(lo, s_lo, s_hi) * (1.0 / B_DH)
    return x * lax.rsqrt(ms + EPS) * gain


def _natten_kernel(q_ref, k_ref, v_ref, gq_ref, gk_ref, bias_ref, o_ref, kn_ref, vb_ref, s_ref, m_ref, e_ref,
                   *, rows_per_step, rows):
    seq = k_ref.shape[0]
    t = pl.program_id(2)
    prep = 256

    @pl.when(t == 0)
    def _():
        def body(c, carry):
            sl = pl.ds(pl.multiple_of(c * prep, prep), prep)
            kn_ref[sl, :] = _pair_rmsnorm(k_ref[sl, :], gk_ref[...]).astype(BF16)
            vb_ref[sl, :LANES] = v_ref[sl, :].astype(BF16)
            vb_ref[sl, LANES:] = jnp.ones((prep, LANES), BF16)
            return carry
        lax.fori_loop(0, seq // prep, body, 0, unroll=PREP_UNROLL)

    q = (_pair_rmsnorm(q_ref[...], gq_ref[...]) * (B_DH ** -0.5)).astype(BF16)
    lo = lax.broadcasted_iota(jnp.int32, (GRID_W, LANES), 1) < B_DH
    zero = jnp.zeros((GRID_W, LANES), BF16)
    win = NA_KH * GRID_W

    def window(rr):
        r = t * rows_per_step + rr
        rs = jnp.clip(r - NA_KH // 2, 0, rows - NA_KH)
        return r - rs, pl.ds(pl.multiple_of(rs * GRID_W, GRID_W), win)

    for rr in range(rows_per_step):
        case, ksl = window(rr)
        qr = q[rr * GRID_W:(rr + 1) * GRID_W, :]
        q2 = jnp.concatenate([jnp.where(lo, qr, zero), jnp.where(lo, zero, qr)], axis=0)
        s = lax.dot_general(q2, kn_ref[ksl, :], (((1,), (1,)), ((), ())), preferred_element_type=F32)
        s_ref[rr] = s + bias_ref[0, case]
    for rr in range(rows_per_step):
        m = jnp.max(s_ref[rr], axis=-1, keepdims=True)
        m_ref[rr] = jnp.broadcast_to(m, (2 * GRID_W, LANES))
    for rr in range(rows_per_step):
        mb = m_ref[rr]
        for c in range(win // LANES):
            cols = slice(c * LANES, (c + 1) * LANES)
            e_ref[rr, :, cols] = jnp.exp(s_ref[rr, :, cols] - mb).astype(BF16)
    for rr in range(rows_per_step):
        _, ksl = window(rr)
        ol = jnp.dot(e_ref[rr], vb_ref[ksl, :], preferred_element_type=F32)
        o = ol[:, :LANES] / ol[:, LANES:]
        o_ref[rr * GRID_W:(rr + 1) * GRID_W, :] = jnp.where(lo, o[:GRID_W], o[GRID_W:]).astype(o_ref.dtype)


def natten_bias_table(rpb):
    heads = rpb.shape[0]
    col = jnp.arange(GRID_W)
    col_start = jnp.clip(col - NA_KW // 2, 0, GRID_W - NA_KW)
    col_valid = (col[None, :] >= col_start[:, None]) & (col[None, :] < col_start[:, None] + NA_KW)
    edge = GRID_W - NA_KW
    w = jnp.concatenate([jnp.repeat(rpb[..., :1], edge, axis=-1), rpb.astype(F32),
                         jnp.repeat(rpb[..., -1:], edge, axis=-1), jnp.zeros(rpb.shape[:-1] + (1,), F32)], axis=-1)
    bias_c = _skew(w, GRID_W)[..., GRID_W - 1:]
    tab = jnp.stack([bias_c[:, NA_KH - 1 - c:2 * NA_KH - 1 - c] for c in range(NA_KH)], axis=1)
    tab = jnp.where(col_valid[:, None, :], tab.transpose(0, 1, 3, 2, 4), NEG)
    tab = tab.reshape(heads // 2, 2, NA_KH, GRID_W, NA_KH * GRID_W).transpose(0, 2, 1, 3, 4)
    return tab.reshape(heads // 2, NA_KH, 2 * GRID_W, NA_KH * GRID_W)


def natten(z, col0, row0, batch, seq, gq, gk, bias_tab, rows_per_step=16):
    hp = bias_tab.shape[0]
    rows = seq // GRID_W
    assert rows >= NA_KH and rows % rows_per_step == 0 and row0 % seq == 0
    tq = rows_per_step * GRID_W
    steps = rows // rows_per_step
    qb0 = row0 // tq
    sb0 = row0 // seq
    win = NA_KH * GRID_W
    kern = functools.partial(_natten_kernel, rows_per_step=rows_per_step, rows=rows)
    gq2 = jnp.tile(gq, 2).reshape(1, 2 * B_DH)
    gk2 = jnp.tile(gk, 2).reshape(1, 2 * B_DH)
    return pl.pallas_call(
        kern,
        grid=(batch, hp, steps),
        in_specs=[
            pl.BlockSpec((tq, LANES), lambda b, h, t: (qb0 + b * steps + t, col0 + h)),
            pl.BlockSpec((seq, LANES), lambda b, h, t: (sb0 + b, col0 + hp + h)),
            pl.BlockSpec((seq, LANES), lambda b, h, t: (sb0 + b, col0 + 2 * hp + h)),
            pl.BlockSpec((1, LANES), lambda b, h, t: (0, 0)),
            pl.BlockSpec((1, LANES), lambda b, h, t: (0, 0)),
            pl.BlockSpec((1, NA_KH, 2 * GRID_W, win), lambda b, h, t: (h, 0, 0, 0)),
        ],
        out_specs=pl.BlockSpec((tq, LANES), lambda b, h, t: (b * steps + t, h)),
        out_shape=jax.ShapeDtypeStruct((batch * seq, 2 * hp * B_DH), BF16),
        scratch_shapes=[
            pltpu.VMEM((seq, LANES), BF16),
            pltpu.VMEM((seq, 2 * LANES), BF16),
            pltpu.VMEM((rows_per_step, 2 * GRID_W, win), F32),
            pltpu.VMEM((rows_per_step, 2 * GRID_W, LANES), F32),
            pltpu.VMEM((rows_per_step, 2 * GRID_W, win), BF16),
        ],
        compiler_params=_params("parallel", "parallel", "arbitrary"),
        name="natten",
    )(z, z, z, gq2, gk2, bias_tab)


def t5_bucket(rel):
    nb = T5_BUCKETS // 2
    max_exact = nb // 2
    ret = jnp.where(rel > 0, nb, 0)
    n = jnp.abs(rel)
    large = max_exact + (jnp.log(jnp.maximum(n, 1).astype(F32) / max_exact)
                         / math.log(T5_MAX_DIST / max_exact) * (nb - max_exact)).astype(jnp.int32)
    large = jnp.minimum(large, nb - 1)
    return ret + jnp.where(n < max_exact, n, large)


def dilated_bias_table(t5_table, dil):
    span = C_TK - 1
    rel = jnp.arange(-span, span + 1)
    vec = t5_table.astype(F32)[t5_bucket(rel * dil)]
    vec = jnp.where((jnp.abs(rel) <= C_RAD)[:, None], vec, NEG).T
    w = jnp.concatenate([vec, jnp.zeros((vec.shape[0], 1), F32)], axis=1)
    r = _skew(w, C_TQ)
    return jnp.stack([r[:, :, span - off:span - off + C_TK] for off in (0, C_RAD, 2 * C_RAD)], axis=1)


def _row_rmsnorm(x, gain):
    return x * lax.rsqrt(jnp.mean(x * x, axis=-1, keepdims=True) + EPS) * gain


def _dilated_kernel(q_ref, k_ref, v_ref, gq_ref, gk_ref, b0_ref, b1_ref, b2_ref, o_ref,
                    kn_ref, qn_ref, oacc_ref, lse_ref, s_ref, m_ref, e_ref, *kv_by_residue, seq, unroll):
    t = pl.program_id(2)
    prep = 256
    kv_sub = {dil: (kv_by_residue[2 * i], kv_by_residue[2 * i + 1])
              for i, dil in enumerate(d for _, d in C_PATTERNS if d > 1)}

    @pl.when(t == 0)
    def _():
        def body(c, carry):
            sl = pl.ds(pl.multiple_of(c * prep, prep), prep)
            kn_ref[sl, :] = _row_rmsnorm(k_ref[sl, :], gk_ref[...])
            return carry
        lax.fori_loop(0, seq // prep, body, 0, unroll=PREP_UNROLL)

        for dil, (ks_ref, vs_ref) in kv_sub.items():
            rows = min(prep, seq // dil)
            for s in range(dil):
                def split(c, carry, dil=dil, s=s, rows=rows, ks_ref=ks_ref, vs_ref=vs_ref):
                    src = pl.ds(c * rows * dil + s, rows, stride=dil)
                    dst = pl.ds(pl.multiple_of(c * rows, rows), rows)
                    ks_ref[s, dst, :] = kn_ref[src, :].astype(BF16)
                    vs_ref[s, dst, :] = v_ref[src, :].astype(BF16)
                    return carry
                lax.fori_loop(0, seq // (dil * rows), split, 0, unroll=True)

    qn_ref[...] = _row_rmsnorm(q_ref[...], gq_ref[...]) * (C_DH ** -0.5)
    n_tiles = C_SUPER // C_TQ

    for p, ((_, dil), b_ref) in enumerate(zip(C_PATTERNS, (b0_ref, b1_ref, b2_ref))):
        n_sub = seq // dil
        tk = min(C_TK, n_sub)
        span = C_TQ * dil
        ones = jnp.ones((tk, LANES), BF16)

        def tile(idx, dil=dil, n_sub=n_sub, tk=tk, span=span):
            u = idx // dil
            s = idx % dil
            qpos = u * span + s
            j0 = t * (C_SUPER // dil) + u * C_TQ
            ws = jnp.clip(j0 - C_RAD, 0, n_sub - tk)
            ksl = pl.ds(pl.multiple_of(ws, C_RAD), tk)
            if dil == 1:
                return pl.ds(pl.multiple_of(qpos, C_TQ), C_TQ), s, ksl, (j0 - ws) // C_RAD
            return pl.ds(qpos, C_TQ, stride=dil), s, ksl, (j0 - ws) // C_RAD

        def keys(s, ksl, dil=dil):
            return kn_ref[ksl, :].astype(BF16) if dil == 1 else kv_sub[dil][0][s, ksl, :]

        def values(s, ksl, dil=dil):
            return v_ref[ksl, :].astype(BF16) if dil == 1 else kv_sub[dil][1][s, ksl, :]

        def scores(idx, carry, tile=tile, keys=keys, tk=tk, b_ref=b_ref):
            qsl, s, ksl, case = tile(idx)
            q = qn_ref[qsl, :].astype(BF16)
            sc = lax.dot_general(q, keys(s, ksl), (((1,), (1,)), ((), ())), preferred_element_type=F32)
            s_ref[idx, :, :tk] = sc + b_ref[0, case, :, :tk]
            return carry

        def row_max(idx, carry, tk=tk):
            m = jnp.max(s_ref[idx, :, :tk], axis=-1, keepdims=True)
            m_ref[idx] = jnp.broadcast_to(m, (C_TQ, LANES))
            return carry

        def exps(idx, carry, tk=tk):
            mb = m_ref[idx]
            for c in range(tk // LANES):
                cols = slice(c * LANES, (c + 1) * LANES)
                e_ref[idx, :, cols] = jnp.exp(s_ref[idx, :, cols] - mb).astype(BF16)
            return carry

        def outputs(idx, carry, tile=tile, values=values, tk=tk, ones=ones, p=p):
            qsl, s, ksl, _ = tile(idx)
            v1 = jnp.concatenate([values(s, ksl), ones], axis=1)
            ol = jnp.dot(e_ref[idx, :, :tk], v1, preferred_element_type=F32)
            l = ol[:, LANES:]
            oacc_ref[p, qsl, :] = ol[:, :LANES] / l
            lse_ref[p, qsl, :] = m_ref[idx] + jnp.log(l)
            return carry

        for phase in (scores, row_max, exps, outputs):
            lax.fori_loop(0, n_tiles, phase, 0, unroll=unroll)

    l0, l1, l2 = lse_ref[0], lse_ref[1], lse_ref[2]
    mx = jnp.maximum(jnp.maximum(l0, l1), l2)
    w0, w1, w2 = jnp.exp(l0 - mx), jnp.exp(l1 - mx), jnp.exp(l2 - mx)
    num = w0 * oacc_ref[0] + w1 * oacc_ref[1] + w2 * oacc_ref[2]
    o_ref[...] = (num / (w0 + w1 + w2)).astype(o_ref.dtype)


def dilated_attention(qkv, row0, batch, seq, gq, gk, tabs, unroll=16):
    heads = tabs[0].shape[0]
    assert seq % C_SUPER == 0 and row0 % seq == 0
    steps = seq // C_SUPER
    qb0 = row0 // C_SUPER
    sb0 = row0 // seq
    n_tiles = C_SUPER // C_TQ
    kern = functools.partial(_dilated_kernel, seq=seq, unroll=unroll)
    tab_specs = [pl.BlockSpec((1,) + tuple(tb.shape[1:]), lambda b, h, t: (h, 0, 0, 0)) for tb in tabs]
    return pl.pallas_call(
        kern,
        grid=(batch, heads, steps),
        in_specs=[
            pl.BlockSpec((C_SUPER, LANES), lambda b, h, t: (qb0 + b * steps + t, h)),
            pl.BlockSpec((seq, LANES), lambda b, h, t: (sb0 + b, heads + h)),
            pl.BlockSpec((seq, LANES), lambda b, h, t: (sb0 + b, 2 * heads + h)),
            pl.BlockSpec((1, LANES), lambda b, h, t: (0, 0)),
            pl.BlockSpec((1, LANES), lambda b, h, t: (0, 0)),
        ] + tab_specs,
        out_specs=pl.BlockSpec((C_SUPER, LANES), lambda b, h, t: (b * steps + t, h)),
        out_shape=jax.ShapeDtypeStruct((batch * seq, heads * C_DH), BF16),
        scratch_shapes=[
            pltpu.VMEM((seq, LANES), F32),
            pltpu.VMEM((C_SUPER, LANES), F32),
            pltpu.VMEM((len(C_PATTERNS), C_SUPER, LANES), F32),
            pltpu.VMEM((len(C_PATTERNS), C_SUPER, LANES), F32),
            pltpu.VMEM((n_tiles, C_TQ, C_TK), F32),
            pltpu.VMEM((n_tiles, C_TQ, LANES), F32),
            pltpu.VMEM((n_tiles, C_TQ, C_TK), BF16),
        ] + [pltpu.VMEM((dil, seq // dil, LANES), BF16) for _, dil in C_PATTERNS if dil > 1 for _ in "kv"],
        compiler_params=_params("parallel", "parallel", "arbitrary"),
        name="dilated",
    )(qkv, qkv, qkv, gq.reshape(1, C_DH), gk.reshape(1, C_DH), *tabs)


def _pack_bf16_pairs(lo, hi):
    lo_bits = lax.bitcast_convert_type(lo.astype(BF16).astype(F32), jnp.uint32)
    hi_bits = lax.bitcast_convert_type(hi.astype(BF16).astype(F32), jnp.uint32)
    return (hi_bits & jnp.uint32(0xFFFF0000)) | (lo_bits >> 16)


def _unpack_bf16_pairs(packed):
    lo = lax.bitcast_convert_type(packed << 16, F32)
    hi = lax.bitcast_convert_type(packed & jnp.uint32(0xFFFF0000), F32)
    return lo, hi


META_E0, META_E1, META_R0, META_R1, META_G0, META_G1 = range(6)


def _route_kernel(x_ref, g_ref, wr_ref, br_ref, hp_ref, meta_ref, tot_ref):
    x = x_ref[...]
    h = x * lax.rsqrt(jnp.mean(x * x, axis=-1, keepdims=True) + EPS) * g_ref[...]
    half = h.shape[1] // 2
    hp_ref[...] = _pack_bf16_pairs(h[:, :half], h[:, half:])
    h_hi = h.astype(BF16)
    h_lo = (h - h_hi.astype(F32)).astype(BF16)
    w_hi = wr_ref[...].astype(BF16)
    w_lo = (wr_ref[...] - w_hi.astype(F32)).astype(BF16)
    hh = jnp.dot(h_hi, jnp.concatenate([w_hi, w_lo], axis=1), preferred_element_type=F32)
    logits = (hh[:, :ROUTER_LANES] + hh[:, ROUTER_LANES:]
              + jnp.dot(h_lo, w_hi, preferred_element_type=F32)) + br_ref[...]
    tm = x.shape[0]
    lane = lax.broadcasted_iota(jnp.int32, (tm, ROUTER_LANES), 1)
    lane_f = lane.astype(F32)

    def first_max(v):
        mx = jnp.max(v, axis=-1, keepdims=True)
        return mx, jnp.min(jnp.where(v == mx, lane_f, float(ROUTER_LANES)), axis=-1, keepdims=True)

    lg = jnp.where(lane < MOE_GROUPS, logits, -jnp.inf)
    gmax, gsel = first_max(lg)
    p_g = 1.0 / jnp.sum(jnp.exp(lg - gmax), axis=-1, keepdims=True)
    expert_lane = (lane >= MOE_GROUPS) & (lane < MOE_GROUPS + MOE_EXPERTS)
    in_group = expert_lane & (((lane - MOE_GROUPS) // MOE_PER_GROUP) == gsel.astype(jnp.int32))
    le = jnp.where(in_group, logits, -jnp.inf)
    v0, i0 = first_max(le)
    pick0 = lane_f == i0
    v1, i1 = first_max(jnp.where(pick0, -jnp.inf, le))
    pick1 = lane_f == i1
    t = jnp.exp(v1 - v0)
    g0 = p_g / (1.0 + t)
    g1 = p_g * t / (1.0 + t)

    onehot = jnp.where(pick0 | pick1, 1.0, 0.0).astype(BF16)
    row = lax.broadcasted_iota(jnp.int32, (tm, tm), 0)
    col = lax.broadcasted_iota(jnp.int32, (tm, tm), 1)
    earlier = jnp.where(col < row, 1.0, 0.0).astype(BF16)
    before = jnp.dot(earlier, onehot, preferred_element_type=F32)
    r0 = jnp.sum(jnp.where(pick0, before, 0.0), axis=-1, keepdims=True)
    r1 = jnp.sum(jnp.where(pick1, before, 0.0), axis=-1, keepdims=True)

    meta = jnp.zeros((tm, ROUTER_LANES), F32)
    for pos, val in ((META_E0, i0 - MOE_GROUPS), (META_E1, i1 - MOE_GROUPS), (META_R0, r0), (META_R1, r1),
                     (META_G0, g0), (META_G1, g1)):
        meta = jnp.where(lane == pos, val, meta)
    meta_ref[...] = meta
    tot_ref[0] = jnp.sum(onehot.astype(F32), axis=0, keepdims=True)


def moe_route(x, g, wr, br, tm):
    n, d = x.shape
    return pl.pallas_call(
        _route_kernel,
        grid=(n // tm,),
        in_specs=[
            pl.BlockSpec((tm, d), lambda i: (i, 0)),
            pl.BlockSpec((1, d), lambda i: (0, 0)),
            pl.BlockSpec((d, ROUTER_LANES), lambda i: (0, 0)),
            pl.BlockSpec((1, ROUTER_LANES), lambda i: (0, 0)),
        ],
        out_specs=[
            pl.BlockSpec((tm, d // 2), lambda i: (i, 0)),
            pl.BlockSpec((tm, ROUTER_LANES), lambda i: (i, 0)),
            pl.BlockSpec((1, 1, ROUTER_LANES), lambda i: (i, 0, 0)),
        ],
        out_shape=[
            jax.ShapeDtypeStruct((n, d // 2), jnp.uint32),
            jax.ShapeDtypeStruct((n, ROUTER_LANES), F32),
            jax.ShapeDtypeStruct((n // tm, 1, ROUTER_LANES), F32),
        ],
        compiler_params=_params("parallel"),
        name="moe_route",
    )(x, g.reshape(1, d), wr, br)


def _dispatch_kernel(d0_ref, d1_ref, pad_first_ref, pad_cnt_ref, nused_ref, h_ref, xb_hbm, sem):
    i = pl.program_id(0)
    tm = h_ref.shape[0]
    base = i * tm

    def row_copy(src_row, dst_row, level=0):
        return pltpu.make_async_copy(h_ref.at[pl.ds(src_row, 1), :], xb_hbm.at[pl.ds(dst_row, 1), :], sem.at[level])

    for r in range(tm):
        row_copy(r, d0_ref[base + r], 0).start(priority=(2 * (r % 2)) % DMA_PRIORITY_LEVELS)
        row_copy(r, d1_ref[base + r], 1).start(priority=(2 * (r % 2) + 1) % DMA_PRIORITY_LEVELS)

    def drain(r, carry):
        row_copy(0, 0, 0).wait()
        row_copy(0, 0, 1).wait()
        return carry
    lax.fori_loop(0, tm, drain, 0, unroll=8)

    @pl.when(i == pl.num_programs(0) - 1)
    def _():
        def per_expert(e, carry):
            def fill(j, c):
                row_copy(0, pad_first_ref[e] + j).start()
                return c
            lax.fori_loop(0, pad_cnt_ref[e], fill, 0)

            def fill_wait(j, c):
                row_copy(0, 0).wait()
                return c
            lax.fori_loop(0, pad_cnt_ref[e], fill_wait, 0)
            return carry
        lax.fori_loop(0, MOE_EXPERTS, per_expert, 0)

        def block_copy(b):
            return pltpu.make_async_copy(h_ref.at[pl.ds(0, MOE_TM), :],
                                         xb_hbm.at[pl.ds(pl.multiple_of(b * MOE_TM, MOE_TM), MOE_TM), :], sem.at[0])

        def fill_block(b, c):
            block_copy(b).start()
            return c
        lax.fori_loop(nused_ref[0], xb_hbm.shape[0] // MOE_TM, fill_block, 0)

        def fill_block_wait(b, c):
            block_copy(b).wait()
            return c
        lax.fori_loop(nused_ref[0], xb_hbm.shape[0] // MOE_TM, fill_block_wait, 0)


def moe_dispatch(hp, d0, d1, pad_first, pad_cnt, n_used, p, tm=512):
    n, dw = hp.shape
    assert tm >= MOE_TM and n % tm == 0
    return pl.pallas_call(
        _dispatch_kernel,
        grid_spec=pltpu.PrefetchScalarGridSpec(
            num_scalar_prefetch=5,
            grid=(n // tm,),
            in_specs=[pl.BlockSpec((tm, dw), lambda i, *_: (i, 0))],
            out_specs=pl.BlockSpec(memory_space=pl.ANY),
            scratch_shapes=[pltpu.SemaphoreType.DMA((2,))],
        ),
        out_shape=jax.ShapeDtypeStruct((p, dw), hp.dtype),
        compiler_params=_params("arbitrary"),
        name="moe_dispatch",
    )(d0, d1, pad_first, pad_cnt, n_used, hp)


def _row_gather_copy(src_hbm, dst_ref, sem, src_row, dst_row):
    return pltpu.make_async_copy(src_hbm.at[pl.ds(src_row, 1), :], dst_ref.at[pl.ds(dst_row, 1), :], sem)


def _gather_row_sets(gathers, base, src_hbm, unroll=8):
    rows = gathers[0][1].shape[0]
    for r in range(rows):
        for k, (idx_ref, dst_ref, sem) in enumerate(gathers):
            level = (len(gathers) * (r % 2) + k) % DMA_PRIORITY_LEVELS
            _row_gather_copy(src_hbm, dst_ref, sem, idx_ref[base + r], r).start(priority=level)

    def drain(r, carry):
        for _, dst_ref, sem in gathers:
            _row_gather_copy(src_hbm, dst_ref, sem, 0, r).wait()
        return carry
    lax.fori_loop(0, rows, drain, 0, unroll=unroll)


def _expert_weight_copies(layer, e, w_hbm, stage, sem):
    return [pltpu.make_async_copy(w.at[layer, e], s, sem.at[k]) for k, (w, s) in enumerate(zip(w_hbm, stage))]


def _ffn_kernel(be_ref, first_ref, next_ref, nused_ref, x_ref, wg_hbm, wu_hbm, wd_hbm, o_ref,
                sg_ref, su_ref, sd_ref, wg_ref, wu_ref, wd_ref, sem, *, layer, k_chunk):
    i = pl.program_id(0)
    w_hbm = (wg_hbm, wu_hbm, wd_hbm)
    stage = (sg_ref, su_ref, sd_ref)
    resident = (wg_ref, wu_ref, wd_ref)

    @pl.when(i == 0)
    def _():
        for cp in _expert_weight_copies(layer, be_ref[0], w_hbm, stage, sem):
            cp.start()

    @pl.when(first_ref[i] == 1)
    def _():
        for cp in _expert_weight_copies(layer, be_ref[i], w_hbm, stage, sem):
            cp.wait()
        for s_ref, w_ref in zip(stage, resident):
            rows = math.gcd(k_chunk, s_ref.shape[0])

            def cast(c, carry, s_ref=s_ref, w_ref=w_ref, rows=rows):
                sl = pl.ds(pl.multiple_of(c * rows, rows), rows)
                w_ref[sl, :] = s_ref[sl, :].astype(BF16)
                return carry
            lax.fori_loop(0, s_ref.shape[0] // rows, cast, 0, unroll=True)

        @pl.when(next_ref[i] >= 0)
        def _():
            for cp in _expert_weight_copies(layer, next_ref[i], w_hbm, stage, sem):
                cp.start(priority=1)

    live = i < nused_ref[0]

    @pl.when(live)
    def _():
        lo, hi = _unpack_bf16_pairs(x_ref[...])
        lo, hi = lo.astype(BF16), hi.astype(BF16)
        half = lo.shape[1]
        g = (jnp.dot(lo, wg_ref[:half, :], preferred_element_type=F32)
             + jnp.dot(hi, wg_ref[half:, :], preferred_element_type=F32))
        u = (jnp.dot(lo, wu_ref[:half, :], preferred_element_type=F32)
             + jnp.dot(hi, wu_ref[half:, :], preferred_element_type=F32))
        h = (jax.nn.silu(g) * u).astype(BF16)
        y = jnp.dot(h, wd_ref[...], preferred_element_type=F32)
        o_ref[...] = _pack_bf16_pairs(y[:, :half], y[:, half:])

    @pl.when(jnp.logical_not(live))
    def _():
        o_ref[...] = jnp.zeros_like(o_ref)


def expert_ffn(xb, block_e, first, next_e, n_used, wg, wu, wd, layer, tm=MOE_TM):
    p, dw = xb.shape
    d, de = wg.shape[2], wg.shape[3]
    any_spec = pl.BlockSpec(memory_space=pl.ANY)
    return pl.pallas_call(
        functools.partial(_ffn_kernel, layer=layer, k_chunk=256),
        grid_spec=pltpu.PrefetchScalarGridSpec(
            num_scalar_prefetch=4,
            grid=(p // tm,),
            in_specs=[
                pl.BlockSpec((tm, dw), lambda i, be, fi, ne, nu: (jnp.minimum(i, nu[0] - 1), 0)),
                any_spec, any_spec, any_spec,
            ],
            out_specs=pl.BlockSpec((tm, dw), lambda i, *_: (i, 0)),
            scratch_shapes=[
                pltpu.VMEM((d, de), F32), pltpu.VMEM((d, de), F32), pltpu.VMEM((de, d), F32),
                pltpu.VMEM((d, de), BF16), pltpu.VMEM((d, de), BF16), pltpu.VMEM((de, d), BF16),
                pltpu.SemaphoreType.DMA((3,)),
            ],
        ),
        out_shape=jax.ShapeDtypeStruct((p, dw), jnp.uint32),
        compiler_params=_params("arbitrary"),
        name="moe_ffn",
    )(block_e, first, next_e, n_used, xb, wg, wu, wd)


def _combine_kernel(d0_ref, d1_ref, x_ref, meta_ref, yb_hbm, *rest, row0, with_norm):
    if with_norm:
        gain_ref, o_ref, h_ref, a_ref, b_ref, sem = rest
    else:
        o_ref, a_ref, b_ref, sem = rest
    base = row0 + pl.program_id(0) * x_ref.shape[0]
    _gather_row_sets([(d0_ref, a_ref, sem.at[0]), (d1_ref, b_ref, sem.at[1])], base, yb_hbm)
    a_lo, a_hi = _unpack_bf16_pairs(a_ref[...])
    b_lo, b_hi = _unpack_bf16_pairs(b_ref[...])
    g0 = meta_ref[:, META_G0:META_G0 + 1]
    g1 = meta_ref[:, META_G1:META_G1 + 1]
    half = a_lo.shape[1]
    lo = x_ref[:, :half] + (g0 * a_lo + g1 * b_lo)
    hi = x_ref[:, half:] + (g0 * a_hi + g1 * b_hi)
    o_ref[:, :half] = lo
    o_ref[:, half:] = hi
    if with_norm:
        ms = (jnp.sum(lo * lo, axis=-1, keepdims=True) + jnp.sum(hi * hi, axis=-1, keepdims=True)) / (2 * half)
        inv = lax.rsqrt(ms + EPS)
        h_ref[:, :half] = (lo * inv * gain_ref[:, :half]).astype(h_ref.dtype)
        h_ref[:, half:] = (hi * inv * gain_ref[:, half:]).astype(h_ref.dtype)


def moe_combine(x, meta, yb, d0, d1, row0, rows, next_gain=None, tm=512):
    d = x.shape[1]
    assert row0 % tm == 0
    blk0 = row0 // tm
    with_norm = next_gain is not None
    row_spec = pl.BlockSpec((tm, d), lambda i, a, b: (i, 0))
    in_specs = [
        pl.BlockSpec((tm, d), lambda i, a, b: (blk0 + i, 0)),
        pl.BlockSpec((tm, ROUTER_LANES), lambda i, a, b: (blk0 + i, 0)),
        pl.BlockSpec(memory_space=pl.ANY),
    ]
    args = [d0, d1, x, meta, yb]
    out_specs, out_shape = row_spec, jax.ShapeDtypeStruct((rows, d), F32)
    if with_norm:
        in_specs.append(pl.BlockSpec((1, d), lambda i, a, b: (0, 0)))
        args.append(next_gain.reshape(1, d))
        out_specs, out_shape = [row_spec, row_spec], [out_shape, jax.ShapeDtypeStruct((rows, d), BF16)]
    return pl.pallas_call(
        functools.partial(_combine_kernel, row0=row0, with_norm=with_norm),
        grid_spec=pltpu.PrefetchScalarGridSpec(
            num_scalar_prefetch=2,
            grid=(rows // tm,),
            in_specs=in_specs,
            out_specs=out_specs,
            scratch_shapes=[pltpu.VMEM((tm, d // 2), jnp.uint32), pltpu.VMEM((tm, d // 2), jnp.uint32),
                            pltpu.SemaphoreType.DMA((2,))],
        ),
        out_shape=out_shape,
        compiler_params=_params("arbitrary"),
        name="moe_combine",
    )(*args)


def moe_layout(meta, tot, tile, tm=MOE_TM):
    n = meta.shape[0]
    m = n * MOE_TOPK
    ids = jnp.arange(MOE_EXPERTS, dtype=jnp.int32)
    cnt = tot[:, 0, MOE_GROUPS:MOE_GROUPS + MOE_EXPERTS].astype(jnp.int32)
    counts = jnp.sum(cnt, axis=0)
    padded = (counts + tm - 1) // tm * tm
    pad_end = jnp.cumsum(padded)
    pad_start = pad_end - padded
    base = pad_start[None, :] + jnp.cumsum(cnt, axis=0) - cnt
    base_tok = jnp.repeat(base, tile, axis=0)

    def slots(e_col, r_col):
        e = meta[:, e_col].astype(jnp.int32)
        return jnp.sum(jnp.where(e[:, None] == ids[None, :], base_tok, 0), axis=1) + meta[:, r_col].astype(jnp.int32)

    d0 = slots(META_E0, META_R0)
    d1 = slots(META_E1, META_R1)
    n_blocks = (m + MOE_EXPERTS * (tm - 1) + tm - 1) // tm

    blk = jnp.arange(n_blocks, dtype=jnp.int32)
    n_used = (pad_end[-1] // tm).astype(jnp.int32)
    block_e = jnp.sum((pad_end[None, :] <= (blk * tm)[:, None]).astype(jnp.int32), axis=1)
    block_e = jnp.minimum(block_e, MOE_EXPERTS - 1)
    live = blk < n_used
    first = (live & ((blk == 0) | (block_e != jnp.roll(block_e, 1)))).astype(jnp.int32)
    has_rows = counts > 0
    later = jnp.arange(MOE_EXPERTS)[None, :] > jnp.arange(MOE_EXPERTS)[:, None]
    nxt_of_e = jnp.min(jnp.where(later & has_rows[None, :], jnp.arange(MOE_EXPERTS)[None, :], MOE_EXPERTS), axis=1)
    nxt_of_e = jnp.where(nxt_of_e == MOE_EXPERTS, -1, nxt_of_e).astype(jnp.int32)
    next_e = jnp.sum(jnp.where(block_e[:, None] == ids[None, :], nxt_of_e[None, :], 0), axis=1)
    pad_first = (pad_start + counts).astype(jnp.int32)
    pad_cnt = (padded - counts).astype(jnp.int32)
    return d0, d1, pad_first, pad_cnt, n_blocks * tm, block_e, first, next_e, n_used.reshape(1)


def hier_moe_residual(x, g, wr_g, br_g, wr_e, br_e, w_gate, w_up, w_down, layer, out_segments=None,
                      next_gain=None, tile=512):
    n, d = x.shape
    pad = ROUTER_LANES - MOE_GROUPS - MOE_EXPERTS
    wr = jnp.concatenate([wr_g, wr_e, jnp.zeros((d, pad), F32)], axis=1)
    br = jnp.concatenate([br_g, br_e, jnp.zeros((pad,), F32)]).reshape(1, ROUTER_LANES)
    hp, meta, tot = moe_route(x, g, wr, br, tile)
    d0, d1, pad_first, pad_cnt, p, block_e, first, next_e, n_used = moe_layout(meta, tot, tile)
    xb = moe_dispatch(hp, d0, d1, pad_first, pad_cnt, n_used, p)
    yb = expert_ffn(xb, block_e, first, next_e, n_used, w_gate, w_up, w_down, layer)
    if out_segments is None:
        return moe_combine(x, meta, yb, d0, d1, 0, n, next_gain)
    return [moe_combine(x, meta, yb, d0, d1, row0, rows) for row0, rows in out_segments]


def kernel(x_prompt, x_sample, norm_mix, norm_ffn, ev_w_in, ev_w_out, a_ln_g, a_ln_b, a_w_s, a_b_s, b_q_gain, b_k_gain, b_rpb, od_w_in, od_w_out, c_q_gain, c_k_gain, t5_table, moe_wr_g, moe_br_g, moe_wr_e, moe_br_e, moe_w_gate, moe_w_up, moe_w_down):
    d = x_prompt.shape[-1]
    segs = [(x_prompt.shape[0], x_prompt.shape[1]), (x_sample.shape[0], x_sample.shape[1])]
    xs = [x_prompt.reshape(-1, d), x_sample.reshape(-1, d)]
    depth = norm_mix.shape[0]
    h = rmsnorm(xs, norm_mix[0], BF16)
    for l in range(depth):
        i = l // 2
        last = l == depth - 1
        if l % 2 == 0:
            aw = a_ln_g.shape[1]
            z = matmul([[h]], [(ev_w_in[i], 0)])
            a_out = mixer_a(z, a_ln_g[i], a_ln_b[i], a_w_s[i], a_b_s[i])
            tab = natten_bias_table(b_rpb[i])
            b_out, row0 = [], 0
            for batch, seq in segs:
                b_out.append(natten(z, 2 * aw // LANES, row0, batch, seq, b_q_gain[i], b_k_gain[i], tab,
                                    rows_per_step=32 if seq // GRID_W >= 64 else 16))
                row0 += batch * seq
            x = matmul([[a_out], b_out], [(ev_w_out[i], 0), (ev_w_out[i], aw)], residuals=xs)
        else:
            qkv = matmul([[h]], [(od_w_in[i], 0)])
            tabs = [dilated_bias_table(t5_table, dil) for _, dil in C_PATTERNS]
            c_out, row0 = [], 0
            for batch, seq in segs:
                c_out.append(dilated_attention(qkv, row0, batch, seq, c_q_gain[i], c_k_gain[i], tabs))
                row0 += batch * seq
            x = matmul([c_out], [(od_w_out[i], 0)], residuals=xs)
        n0 = segs[0][0] * segs[0][1]
        res = hier_moe_residual(x, norm_ffn[l], moe_wr_g[l], moe_br_g[l], moe_wr_e[l], moe_br_e[l],
                                moe_w_gate, moe_w_up, moe_w_down, l,
                                out_segments=[(0, n0), (n0, segs[1][0] * segs[1][1])] if last else None,
                                next_gain=None if last else norm_mix[l + 1])
        if last:
            xs = res
        else:
            xs, h = [res[0]], res[1]
    return (xs[0].reshape(x_prompt.shape), xs[1].reshape(x_sample.shape))
```
